```python
import math
import jax
import jax.numpy as jnp
from jax import lax
import numpy as np

D_MODEL = 1024
BATCH = 8
SEQ = 4096
DEPTH = 1

MIX_WIDTH = D_MODEL
MLSTM_WIDTH = MIX_WIDTH // 2
MLSTM_HEADS = 4
MLSTM_HEAD_DIM = MLSTM_WIDTH // MLSTM_HEADS
MLSTM_CHUNK = 128
MLSTM_M_INIT = -1e30
HYENA_WIDTH = MIX_WIDTH - MLSTM_WIDTH
HYENA_GROUPS = 4
HYENA_GROUP_DIM = HYENA_WIDTH // HYENA_GROUPS
HYENA_ORDER = 2
HYENA_DIRS = 2
HYENA_BANDS = 16
HYENA_EMB_DIM = 1 + 2 * HYENA_BANDS
HYENA_FILTER_HIDDEN = 64
HYENA_WINDOW_SHIFT = 0.05
HYENA_SLOW_DECAY = -math.log(1e-2) / 1.5
HYENA_FAST_DECAY = -math.log(1e-2) / 0.3
SHORT_CONV = 3
N_GATE_COLS = 4 * MLSTM_HEADS
PROJ_COLS = 4 * MLSTM_WIDTH + N_GATE_COLS + (HYENA_ORDER + 1) * HYENA_WIDTH
N_EXPERTS = 16
EC_CAPACITY_FACTOR = 2
EXPERT_FF = 2 * D_MODEL
RMS_EPS = 1e-6

kernel_name = 'hybrid_mlstm_hyena_ec_moe_encoder'


def rms_norm(x, g):
    xf = x.astype(jnp.float32)
    y = xf * lax.rsqrt(jnp.mean(xf * xf, axis=-1, keepdims=True) + RMS_EPS)
    return (y * g.astype(jnp.float32)).astype(x.dtype)


def centred_short_conv(u, w, b):
    S = u.shape[1]
    r = SHORT_CONV // 2
    up = jnp.pad(u, ((0, 0), (r, r), (0, 0)))
    y = b
    for j in range(SHORT_CONV):
        y = y + up[:, j:j + S] * w[j]
    return y


def mlstm_scan(q, k, v, i_pre, logf):
    B, H, S, Dh = q.shape
    nc = S // MLSTM_CHUNK
    qc = q.reshape(B, H, nc, MLSTM_CHUNK, Dh)
    kc = k.reshape(B, H, nc, MLSTM_CHUNK, Dh)
    vc = v.reshape(B, H, nc, MLSTM_CHUNK, Dh)
    ic = i_pre.reshape(B, H, nc, MLSTM_CHUNK)
    b = jnp.cumsum(logf.reshape(B, H, nc, MLSTM_CHUNK), axis=-1)
    g = b[..., -1]
    w_end = g[..., None] - b + ic
    a = jnp.max(w_end, axis=-1)
    e_end = jnp.exp(w_end - a[..., None])
    c_loc = jnp.einsum('bhcsv,bhcsk->bhcvk', vc * e_end[..., None], kc)
    n_loc = jnp.einsum('bhcs,bhcsk->bhck', e_end, kc)

    def step(carry, inp):
        c_st, n_st, m_st = carry
        g_j, a_j, c_j, n_j = inp
        m_new = jnp.maximum(g_j + m_st, a_j)
        s_prev = jnp.exp(g_j + m_st - m_new)
        s_loc = jnp.exp(a_j - m_new)
        c_new = s_prev[..., None, None] * c_st + s_loc[..., None, None] * c_j
        n_new = s_prev[..., None] * n_st + s_loc[..., None] * n_j
        return (c_new, n_new, m_new), (c_st, n_st, m_st)

    init = (jnp.zeros((B, H, Dh, Dh), q.dtype), jnp.zeros((B, H, Dh), q.dtype),
            jnp.full((B, H), MLSTM_M_INIT, q.dtype))
    xs = tuple(jnp.moveaxis(t, 2, 0) for t in (g, a, c_loc, n_loc))
    _, (c_in, n_in, m_in) = lax.scan(step, init, xs)
    c_in = jnp.moveaxis(c_in, 0, 2)
    n_in = jnp.moveaxis(n_in, 0, 2)
    m_in = jnp.moveaxis(m_in, 0, 2)

    tri = jnp.tril(jnp.ones((MLSTM_CHUNK, MLSTM_CHUNK), dtype=bool))
    log_d = jnp.where(tri, b[..., :, None] - b[..., None, :] + ic[..., None, :], -jnp.inf)
    log_inter = b + m_in[..., None]
    m_t = jnp.maximum(log_inter, jnp.max(log_d, axis=-1))
    d = jnp.exp(log_d - m_t[..., None])
    e_inter = jnp.exp(log_inter - m_t)
    s = jnp.einsum('bhctd,bhcsd->bhcts', qc, kc) * d
    num = (jnp.einsum('bhcts,bhcsd->bhctd', s, vc)
           + e_inter[..., None] * jnp.einsum('bhcvk,bhctk->bhctv', c_in, qc))
    den = jnp.sum(s, axis=-1) + e_inter * jnp.einsum('bhck,bhctk->bhct', n_in, qc)
    nrm = jnp.maximum(jnp.abs(den), jnp.exp(-m_t))
    return (num / nrm[..., None]).reshape(B, H, S, Dh)


def mlstm_mixer(q, k, v, o, gates, conv_w, conv_b, norm_g):
    B, S, _ = q.shape
    H, Dh = MLSTM_HEADS, MLSTM_HEAD_DIM
    qk = jax.nn.silu(centred_short_conv(jnp.concatenate([q, k], axis=-1), conv_w, conv_b))
    q, k = jnp.split(qk, 2, axis=-1)

    def heads(t):
        return t.reshape(B, S, H, Dh).transpose(0, 2, 1, 3).astype(jnp.float32)

    qh, kh, vh = heads(q), heads(k) * (MLSTM_HEAD_DIM ** -0.5), heads(v)
    g4 = gates.reshape(B, S, 4, H).transpose(2, 0, 3, 1).astype(jnp.float32)
    i_f, f_f, i_b, f_b = g4[0], g4[1], g4[2], g4[3]
    h_f = mlstm_scan(qh, kh, vh, i_f, jax.nn.log_sigmoid(f_f))

    def fl(t):
        return jnp.flip(t, axis=2)

    h_b = fl(mlstm_scan(fl(qh), fl(kh), fl(vh), fl(i_b), fl(jax.nn.log_sigmoid(f_b))))
    hsum = (h_f + h_b).transpose(0, 2, 1, 3)
    hn = rms_norm(hsum, norm_g.reshape(H, Dh)).reshape(B, S, MLSTM_WIDTH)
    return (jax.nn.sigmoid(o.astype(jnp.float32)) * hn).astype(o.dtype)


def hyena_filters(L, w1, b1, w2, b2, w3, freq, deltas):
    f32 = jnp.float32
    t = jnp.linspace(0.0, 1.0, L, dtype=f32)[:, None]
    w = 2.0 * math.pi * jnp.arange(L, dtype=f32)[:, None] / L
    bands = jnp.linspace(1e-4, HYENA_BANDS - 1, HYENA_BANDS, dtype=f32)[None, :]
    z = jnp.concatenate([t, jnp.cos(bands * w), -jnp.sin(bands * w)], axis=-1)
    fr = freq.astype(f32)
    hid = jnp.sin(fr * (z @ w1.astype(f32) + b1.astype(f32)))
    hid = jnp.sin(fr * (hid @ w2.astype(f32) + b2.astype(f32)))
    h = hid @ w3.astype(f32)
    window = jnp.exp(-t * jnp.abs(deltas.astype(f32))) + HYENA_WINDOW_SHIFT
    return (h * window).reshape(L, HYENA_ORDER, HYENA_DIRS, HYENA_WIDTH)


def bidir_long_conv(u, h_fwd, h_bwd, bias):
    L, C = h_fwd.shape
    two_sided = jnp.concatenate([(h_fwd[0] + h_bwd[0])[None], h_fwd[1:],
                                 jnp.zeros((1, C), jnp.float32), jnp.flip(h_bwd[1:], axis=0)], axis=0)
    k_f = jnp.fft.rfft(two_sided, n=2 * L, axis=0)
    uf = u.astype(jnp.float32)
    u_f = jnp.fft.rfft(uf, n=2 * L, axis=1)
    y = jnp.fft.irfft(u_f * k_f[None], n=2 * L, axis=1)[:, :L]
    return (y + uf * bias.astype(jnp.float32)).astype(u.dtype)


def hyena_mixer(hx, conv_w, conv_b, filt, bias, norm_g):
    B, S, _ = hx.shape
    u = centred_short_conv(hx, conv_w, conv_b)
    x1, x2, z = jnp.split(u, HYENA_ORDER + 1, axis=-1)
    z = x1 * bidir_long_conv(z, filt[:, 0, 0], filt[:, 0, 1], bias[0])
    z = x2 * bidir_long_conv(z, filt[:, 1, 0], filt[:, 1, 1], bias[1])
    zg = z.reshape(B, S, HYENA_GROUPS, HYENA_GROUP_DIM)
    return rms_norm(zg, norm_g.reshape(HYENA_GROUPS, HYENA_GROUP_DIM)).reshape(B, S, HYENA_WIDTH)


def expert_choice_moe(xn, w_router, w_gate, w_up, w_down):
    B, T, D = xn.shape
    cap = EC_CAPACITY_FACTOR * T // N_EXPERTS
    aff = jax.nn.softmax((xn @ w_router).astype(jnp.float32), axis=-1)
    g, idx = lax.top_k(jnp.swapaxes(aff, 1, 2), cap)
    bidx = jnp.arange(B)[:, None, None]
    xs = xn[bidx, idx]
    hid = jax.nn.silu(jnp.einsum('becd,edf->becf', xs, w_gate)) * jnp.einsum('becd,edf->becf', xs, w_up)
    y = jnp.einsum('becf,efd->becd', hid, w_down) * g[..., None].astype(xn.dtype)
    return jnp.zeros_like(xn).at[bidx, idx].add(y)


def setup_inputs(seed: int = 0) -> dict:
    key = jax.random.key(seed)
    ks = jax.random.split(key, 32)
    f32 = jnp.float32
    D, L = D_MODEL, DEPTH

    def nrm(k, shape, scale):
        return jax.random.normal(k, shape, f32) * scale

    gate_off = 4 * MLSTM_WIDTH
    H = MLSTM_HEADS
    f_bias = jnp.linspace(3.0, 6.0, H, dtype=f32)
    b_in = nrm(ks[5], (L, PROJ_COLS), 0.02)
    b_in = b_in.at[:, gate_off + H:gate_off + 2 * H].add(f_bias)
    b_in = b_in.at[:, gate_off + 3 * H:gate_off + 4 * H].add(f_bias)
    n_filt = HYENA_ORDER * HYENA_DIRS * HYENA_WIDTH
    base_delta = jnp.tile(jnp.linspace(HYENA_SLOW_DECAY, HYENA_FAST_DECAY, HYENA_WIDTH, dtype=f32),
                          HYENA_ORDER * HYENA_DIRS)
    return {
        'x': nrm(ks[0], (BATCH, SEQ, D), 1.0),
        'c': nrm(ks[1], (BATCH, D), 1.0),
        'w_ada': nrm(ks[2], (L, D, 6 * D), 0.5 * D ** -0.5),
        'b_ada': nrm(ks[3], (L, 6 * D), 0.02),
        'g_mix': 1.0 + nrm(ks[4], (L, D), 0.02),
        'w_in': nrm(ks[6], (L, D, PROJ_COLS), D ** -0.5),
        'b_in': b_in,
        'conv_qk_w': nrm(ks[7], (L, SHORT_CONV, 2 * MLSTM_WIDTH), SHORT_CONV ** -0.5),
        'conv_qk_b': nrm(ks[8], (L, 2 * MLSTM_WIDTH), 0.02),
        'mlstm_norm_g': 1.0 + nrm(ks[9], (L, MLSTM_WIDTH), 0.02),
        'conv_hy_w': nrm(ks[10], (L, SHORT_CONV, (HYENA_ORDER + 1) * HYENA_WIDTH), SHORT_CONV ** -0.5),
        'conv_hy_b': nrm(ks[11], (L, (HYENA_ORDER + 1) * HYENA_WIDTH), 0.02),
        'hy_w1': nrm(ks[12], (L, HYENA_EMB_DIM, HYENA_FILTER_HIDDEN), HYENA_EMB_DIM ** -0.5),
        'hy_b1': nrm(ks[13], (L, HYENA_FILTER_HIDDEN), 0.1),
        'hy_w2': nrm(ks[14], (L, HYENA_FILTER_HIDDEN, HYENA_FILTER_HIDDEN), HYENA_FILTER_HIDDEN ** -0.5),
        'hy_b2': nrm(ks[15], (L, HYENA_FILTER_HIDDEN), 0.1),
        'hy_w3': nrm(ks[16], (L, HYENA_FILTER_HIDDEN, n_filt), 0.1 * HYENA_FILTER_HIDDEN ** -0.5),
        'hy_freq': 1.0 + nrm(ks[17], (L, HYENA_FILTER_HIDDEN), 0.02),
        'hy_deltas': base_delta[None] * (1.0 + nrm(ks[18], (L, n_filt), 0.02)),
        'hy_bias': nrm(ks[19], (L, HYENA_ORDER, HYENA_WIDTH), 0.1),
        'hyena_norm_g': 1.0 + nrm(ks[20], (L, HYENA_WIDTH), 0.02),
        'w_out': nrm(ks[21], (L, MIX_WIDTH, D), MIX_WIDTH ** -0.5),
        'g_ffn': 1.0 + nrm(ks[22], (L, D), 0.02),
        'w_router': nrm(ks[23], (L, D, N_EXPERTS), D ** -0.5),
        'w_gate': nrm(ks[24], (L, N_EXPERTS, D, EXPERT_FF), D ** -0.5),
        'w_up': nrm(ks[25], (L, N_EXPERTS, D, EXPERT_FF), D ** -0.5),
        'w_down': nrm(ks[26], (L, N_EXPERTS, EXPERT_FF, D), EXPERT_FF ** -0.5),
        'g_final': 1.0 + nrm(ks[27], (D,), 0.02),
    }


def reference(x, c, w_ada, b_ada, g_mix, w_in, b_in, conv_qk_w, conv_qk_b, mlstm_norm_g,
              conv_hy_w, conv_hy_b, hy_w1, hy_b1, hy_w2, hy_b2, hy_w3, hy_freq, hy_deltas, hy_bias,
              hyena_norm_g, w_out, g_ffn, w_router, w_gate, w_up, w_down, g_final):
    W = MLSTM_WIDTH
    split_pts = [W, 2 * W, 3 * W, 4 * W, 4 * W + N_GATE_COLS]
    seq_len = x.shape[1]
    for l in range(DEPTH):
        mod = (c @ w_ada[l] + b_ada[l])[:, None, :]
        sh1, sc1, gt1, sh2, sc2, gt2 = jnp.split(mod, 6, axis=-1)
        h = rms_norm(x, g_mix[l]) * (1 + sc1) + sh1
        p = h @ w_in[l] + b_in[l]
        q, k, v, o, gates, hx = jnp.split(p, split_pts, axis=-1)
        y_m = mlstm_mixer(q, k, v, o, gates, conv_qk_w[l], conv_qk_b[l], mlstm_norm_g[l])
        filt = hyena_filters(seq_len, hy_w1[l], hy_b1[l], hy_w2[l], hy_b2[l], hy_w3[l], hy_freq[l], hy_deltas[l])
        y_h = hyena_mixer(hx, conv_hy_w[l], conv_hy_b[l], filt, hy_bias[l], hyena_norm_g[l])
        mixed = jnp.concatenate([y_m, y_h], axis=-1) @ w_out[l]
        x = x + gt1 * mixed
        hf = rms_norm(x, g_ffn[l]) * (1 + sc2) + sh2
        x = x + gt2 * expert_choice_moe(hf, w_router[l], w_gate[l], w_up[l], w_down[l])
    return rms_norm(x, g_final)
```

```python
import functools
import math

import numpy as np
import jax
import jax.numpy as jnp
from jax import lax
from jax.experimental import pallas as pl
from jax.experimental.pallas import tpu as pltpu

F32 = jnp.float32
BF16 = jnp.bfloat16

MLSTM_HEADS = 4
HEAD_DIM = 128
MLSTM_CHUNK = 128
MLSTM_M_INIT = -1e30
HYENA_GROUP_DIM = 128
HYENA_ORDER = 2
HYENA_BANDS = 16
HYENA_WINDOW_SHIFT = 0.05
N_GATE_COLS = 4 * MLSTM_HEADS
N_EXPERTS = 16
EC_CAPACITY_FACTOR = 2
RMS_EPS = 1e-6

LANES = 128
SUBLANES = 8
FFT_N2 = 64
FFT_PITCH = 72
NEG_BIG = -1e30
VMEM_LIMIT = 56 * 1024 * 1024


def _cparams(sem, vmem=None):
    return pltpu.CompilerParams(dimension_semantics=sem, vmem_limit_bytes=vmem or VMEM_LIMIT)


def _split(a):
    hi = a.astype(BF16)
    lo = (a - hi.astype(F32)).astype(BF16)
    return hi, lo


def _dot(a, b):
    return jnp.dot(a, b, preferred_element_type=F32)


def _dot_nt(a, b):
    return lax.dot_general(a, b, (((1,), (1,)), ((), ())), preferred_element_type=F32)


def _dot3(a, b):
    ah, al = _split(a)
    bh, bl = _split(b)
    return _dot(ah, bh) + _dot(ah, bl) + _dot(al, bh)


def _dot3_nt(a, b):
    ah, al = _split(a)
    bh, bl = _split(b)
    return _dot_nt(ah, bh) + _dot_nt(ah, bl) + _dot_nt(al, bh)


def _dotc(w_hi, w_lo, x, passes):
    xh, xl = _split(x)
    out = _dot(w_hi, xh)
    if passes == 3:
        out = out + _dot(w_hi, xl) + _dot(w_lo, xh)
    return out


def _rms(x, g):
    return x * lax.rsqrt(jnp.mean(x * x, axis=-1, keepdims=True) + RMS_EPS) * g


def _sigmoid(x):
    return 1.0 / (1.0 + jnp.exp(-x))


def _log_sigmoid(x):
    return jnp.minimum(x, 0.0) - jnp.log(1.0 + jnp.exp(-jnp.abs(x)))


def _mod_kernel(c_ref, w_ref, b_ref, o_ref):
    o_ref[...] = _dot3(c_ref[...], w_ref[...]) + b_ref[...]


def _modulation(c, w_ada, b_ada):
    B, D = c.shape
    n = w_ada.shape[1]
    tn = 768
    return pl.pallas_call(
        _mod_kernel,
        grid=(n // tn,),
        in_specs=[pl.BlockSpec((B, D), lambda j: (0, 0)),
                  pl.BlockSpec((D, tn), lambda j: (0, j)),
                  pl.BlockSpec((1, tn), lambda j: (0, j))],
        out_specs=pl.BlockSpec((B, tn), lambda j: (0, j)),
        out_shape=jax.ShapeDtypeStruct((B, n), F32),
        compiler_params=_cparams(("parallel",)),
        name="adaln_mod",
    )(c, w_ada, b_ada.reshape(1, n))


def _inproj_kernel(x_ref, sc_ref, sh_ref, g_ref, wq_ref, bq_ref, wh_ref, bh_ref, wgt_ref, bgt_ref,
                   qkvo_ref, hx_ref, gt_ref):
    x = x_ref[0]
    h = _rms(x, g_ref[...]) * (1.0 + sc_ref[0]) + sh_ref[0]
    hb = h.astype(BF16)
    qkvo_ref[0] = _dot(hb, wq_ref[...]) + bq_ref[...]
    hx_ref[0] = _dot(hb, wh_ref[...]) + bh_ref[...]
    gt_ref[0] = _dot3_nt(wgt_ref[...], h) + bgt_ref[...]


def _in_projection(x, sc1, sh1, g_mix, w_in, b_in, ts=512):
    B, S, D = x.shape
    nq = 4 * MLSTM_HEADS * HEAD_DIM
    nh = w_in.shape[1] - nq - N_GATE_COLS
    wq = w_in[:, :nq].astype(BF16)
    wh = w_in[:, nq + N_GATE_COLS:].astype(BF16)
    bq = b_in[:nq].reshape(1, nq)
    bh = b_in[nq + N_GATE_COLS:].reshape(1, nh)
    ng = MLSTM_HEADS * SUBLANES
    pad = lambda t: jnp.pad(t.reshape(4, MLSTM_HEADS, -1).transpose(1, 0, 2), ((0, 0), (0, 4), (0, 0))).reshape(ng, -1)
    wgt = pad(w_in[:, nq:nq + N_GATE_COLS].T)
    bgt = pad(b_in[nq:nq + N_GATE_COLS].reshape(N_GATE_COLS, 1))
    const = lambda b, i: (0, 0)
    return pl.pallas_call(
        _inproj_kernel,
        grid=(B, S // ts),
        in_specs=[pl.BlockSpec((1, ts, D), lambda b, i: (b, i, 0)),
                  pl.BlockSpec((1, 1, D), lambda b, i: (b, 0, 0)),
                  pl.BlockSpec((1, 1, D), lambda b, i: (b, 0, 0)),
                  pl.BlockSpec((1, D), const),
                  pl.BlockSpec((D, nq), const), pl.BlockSpec((1, nq), const),
                  pl.BlockSpec((D, nh), const), pl.BlockSpec((1, nh), const),
                  pl.BlockSpec((ng, D), const), pl.BlockSpec((ng, 1), const)],
        out_specs=[pl.BlockSpec((1, ts, nq), lambda b, i: (b, i, 0)),
                   pl.BlockSpec((1, ts, nh), lambda b, i: (b, i, 0)),
                   pl.BlockSpec((1, ng, ts), lambda b, i: (b, 0, i))],
        out_shape=[jax.ShapeDtypeStruct((B, S, nq), F32),
                   jax.ShapeDtypeStruct((B, S, nh), F32),
                   jax.ShapeDtypeStruct((B, ng, S), F32)],
        compiler_params=_cparams(("parallel", "parallel")),
        name="in_proj",
    )(x, sc1.reshape(B, 1, D), sh1.reshape(B, 1, D), g_mix.reshape(1, D), wq, bq, wh, bh, wgt, bgt)


def _short_conv_chunk(src_ref, lead, c, nc, w_ref, b_ref, rows):
    L = rows.shape[0]
    S = nc * L
    t0 = pl.multiple_of(c * L, L)
    cur = src_ref[lead, pl.ds(t0, L), :]
    p0 = pl.multiple_of(jnp.maximum(t0 - SUBLANES, 0), SUBLANES)
    n0 = pl.multiple_of(jnp.minimum(t0 + L, S - SUBLANES), SUBLANES)
    prev_row = src_ref[lead, pl.ds(p0, SUBLANES), :][SUBLANES - 1:SUBLANES, :]
    next_row = src_ref[lead, pl.ds(n0, SUBLANES), :][0:1, :]
    prev_row = jnp.where(c > 0, prev_row, 0.0)
    next_row = jnp.where(c < nc - 1, next_row, 0.0)
    up = jnp.where(rows == 0, prev_row, pltpu.roll(cur, 1, 0))
    dn = jnp.where(rows == L - 1, next_row, pltpu.roll(cur, L - 1, 0))
    return b_ref[...] + up * w_ref[0:1, :] + cur * w_ref[1:2, :] + dn * w_ref[2:3, :]


def _mlstm_kernel(q_ref, k_ref, v_ref, o_ref, gt_ref, wq_ref, wk_ref, bq_ref, bk_ref, ng_ref,
                  y_ref, qc_ref, kc_ref, hb_ref, c_ref, *, nc):
    L = MLSTM_CHUNK
    rows = lax.broadcasted_iota(jnp.int32, (L, LANES), 0)
    cols = lax.broadcasted_iota(jnp.int32, (L, LANES), 1)
    eye = rows == cols
    k_scale = HEAD_DIM ** -0.5

    def conv_body(c, carry):
        t0 = pl.multiple_of(c * L, L)
        yq = _short_conv_chunk(q_ref, 0, c, nc, wq_ref, bq_ref, rows)
        qc_ref[pl.ds(t0, L), :] = yq * _sigmoid(yq)
        yk = _short_conv_chunk(k_ref, 0, c, nc, wk_ref, bk_ref, rows)
        kc_ref[pl.ds(t0, L), :] = yk * _sigmoid(yk) * k_scale
        return carry

    lax.fori_loop(0, nc, conv_body, 0)

    def chunk_step(c, rev, n, m):
        t0 = pl.multiple_of(c * L, L)
        q = qc_ref[pl.ds(t0, L), :]
        k = kc_ref[pl.ds(t0, L), :]
        v = v_ref[0, pl.ds(t0, L), :]
        i_row = gt_ref[0, 0, (2 if rev else 0):(3 if rev else 1), pl.ds(t0, L)]
        f_row = gt_ref[0, 0, (3 if rev else 1):(4 if rev else 2), pl.ds(t0, L)]
        logf = _log_sigmoid(f_row)
        tri = (cols >= rows) if rev else (cols <= rows)
        b_col = jnp.sum(jnp.where(tri, logf, 0.0), axis=1, keepdims=True)
        b_row = jnp.sum(jnp.where(eye, b_col, 0.0), axis=0, keepdims=True)
        i_col = jnp.sum(jnp.where(eye, i_row, 0.0), axis=1, keepdims=True)
        g = jnp.sum(logf, axis=1, keepdims=True)
        log_d = jnp.where(tri, b_col - b_row + i_row, NEG_BIG)
        log_inter = b_col + m
        m_t = jnp.maximum(log_inter, jnp.max(log_d, axis=1, keepdims=True))
        d = jnp.exp(log_d - m_t)
        e_inter = jnp.exp(log_inter - m_t)
        qb = q.astype(BF16)
        kb = k.astype(BF16)
        vb = v.astype(BF16)
        c_in = c_ref[1 if rev else 0]
        s = _dot_nt(qb, kb) * d
        num = _dot(s.astype(BF16), vb) + e_inter * _dot_nt(qb, c_in.astype(BF16))
        den = jnp.sum(s, axis=1, keepdims=True) + e_inter * jnp.sum(q * n, axis=1, keepdims=True)
        nrm = jnp.maximum(jnp.abs(den), jnp.exp(-m_t))
        h_out = num / nrm
        w_end = g - b_col + i_col
        a = jnp.max(w_end, axis=0, keepdims=True)
        e_end = jnp.exp(w_end - a)
        c_loc = lax.dot_general((v * e_end).astype(BF16), kb, (((0,), (0,)), ((), ())),
                                preferred_element_type=F32)
        n_loc = jnp.sum(k * e_end, axis=0, keepdims=True)
        m_new = jnp.maximum(g + m, a)
        s_prev = jnp.exp(g + m - m_new)
        s_loc = jnp.exp(a - m_new)
        c_ref[1 if rev else 0] = s_prev * c_in + s_loc * c_loc
        return h_out, s_prev * n + s_loc * n_loc, m_new

    c_ref[...] = jnp.zeros_like(c_ref)
    n0 = jnp.zeros((1, HEAD_DIM), F32)
    m0 = jnp.full((1, 1), MLSTM_M_INIT, F32)

    def scan_body(j, carry):
        n_f, m_f, n_b, m_b = carry
        h_f, n_f, m_f = chunk_step(j, False, n_f, m_f)
        y_ref[0, pl.ds(pl.multiple_of(j * L, L), L), :] = h_f
        cb = nc - 1 - j
        h_b, n_b, m_b = chunk_step(cb, True, n_b, m_b)
        hb_ref[pl.ds(pl.multiple_of(cb * L, L), L), :] = h_b
        return n_f, m_f, n_b, m_b

    lax.fori_loop(0, nc, scan_body, (n0, m0, n0, m0))

    def final_body(c, carry):
        t0 = pl.multiple_of(c * L, L)
        hs = y_ref[0, pl.ds(t0, L), :] + hb_ref[pl.ds(t0, L), :]
        y_ref[0, pl.ds(t0, L), :] = _sigmoid(o_ref[0, pl.ds(t0, L), :]) * _rms(hs, ng_ref[...])
        return carry

    lax.fori_loop(0, nc, final_body, 0)


def _mlstm(qkvo, gates_t, conv_w, conv_b, norm_g):
    B, S, _ = qkvo.shape
    H, Dh = MLSTM_HEADS, HEAD_DIM
    nc = S // MLSTM_CHUNK
    seq = lambda off: pl.BlockSpec((1, S, Dh), lambda b, h: (b, 0, off + h))
    vec = lambda r, off: pl.BlockSpec((r, Dh), lambda b, h: (0, off + h))
    return pl.pallas_call(
        functools.partial(_mlstm_kernel, nc=nc),
        grid=(B, H),
        in_specs=[seq(0), seq(H), seq(2 * H), seq(3 * H),
                  pl.BlockSpec((1, 1, SUBLANES, S), lambda b, h: (b, h, 0, 0)),
                  vec(3, 0), vec(3, H), vec(1, 0), vec(1, H), vec(1, 0)],
        out_specs=pl.BlockSpec((1, S, Dh), lambda b, h: (b, 0, h)),
        out_shape=jax.ShapeDtypeStruct((B, S, H * Dh), F32),
        scratch_shapes=[pltpu.VMEM((S, Dh), F32), pltpu.VMEM((S, Dh), F32), pltpu.VMEM((S, Dh), F32),
                        pltpu.VMEM((2, Dh, Dh), F32)],
        compiler_params=_cparams(("parallel", "arbitrary")),
        name="mlstm",
    )(qkvo, qkvo, qkvo, qkvo, gates_t.reshape(B, H, SUBLANES, S), conv_w, conv_w, conv_b.reshape(1, -1), conv_b.reshape(1, -1),
      norm_g.reshape(1, -1))


FFT_PASSES = 3


def _hilo(m):
    m32 = jnp.asarray(m, F32)
    hi = m32.astype(BF16)
    return hi, (m32 - hi.astype(F32)).astype(BF16)


def _stack_complex(m):
    return np.block([[m.real, -m.imag], [m.imag, m.real]])


def _dft_constants(S):
    N = 2 * S
    N2 = FFT_N2
    N1 = N // N2
    k1 = np.arange(N1)
    n2 = np.arange(N2)
    f1 = np.exp(-2j * np.pi * np.outer(k1, np.arange(N1)) / N1)
    f2 = np.exp(-2j * np.pi * np.outer(n2, n2) / N2)
    tw = np.exp(-2j * np.pi * np.outer(k1, n2) / N)
    g1 = np.conj(f1).T[:N1 // 2] / N
    consts = dict(
        f1_real=_hilo(np.concatenate([f1.real, f1.imag], axis=0)),
        f1_half=_hilo(_stack_complex(f1[:, :N1 // 2])),
        f2=_hilo(_stack_complex(f2)),
        f2_inv=_hilo(_stack_complex(np.conj(f2))),
        g1=_hilo(_stack_complex(g1)),
        tw_re=jnp.asarray(np.broadcast_to(tw.real[:, :, None], (N1, N2, LANES)), F32),
        tw_im=jnp.asarray(np.broadcast_to(tw.imag[:, :, None], (N1, N2, LANES)), F32),
    )
    return N1, N2, consts


def _hy_hidden_kernel(w1t_ref, w1c_ref, w1s_ref, b1_ref, w2_ref, b2_ref, fr_ref, o_ref, *, S, T):
    j = pl.program_id(0) * T + lax.broadcasted_iota(jnp.int32, (T, 1), 0)
    p = jnp.where(j < S, j, 2 * S - j).astype(F32)
    t = p / (S - 1)
    w = (2.0 * math.pi) * p / S
    band = lax.broadcasted_iota(jnp.int32, (1, HYENA_BANDS), 1).astype(F32)
    bands = 1e-4 + band * ((HYENA_BANDS - 1 - 1e-4) / (HYENA_BANDS - 1))
    arg = bands * w
    pre = t * w1t_ref[...] + _dot3(jnp.cos(arg), w1c_ref[...]) + _dot3(-jnp.sin(arg), w1s_ref[...]) + b1_ref[...]
    hid = jnp.sin(fr_ref[...] * pre)
    o_ref[...] = jnp.sin(fr_ref[...] * (_dot3(hid, w2_ref[...]) + b2_ref[...]))


def _hy_hidden(S, w1, b1, w2, b2, freq):
    N = 2 * S
    T = 1024
    Hd = w2.shape[0]
    full = lambda a: pl.BlockSpec(a.shape, lambda i: (0,) * a.ndim)
    args = (w1[0:1], w1[1:1 + HYENA_BANDS], w1[1 + HYENA_BANDS:], b1.reshape(1, Hd), w2, b2.reshape(1, Hd),
            freq.reshape(1, Hd))
    return pl.pallas_call(
        functools.partial(_hy_hidden_kernel, S=S, T=T),
        grid=(N // T,),
        in_specs=[full(a) for a in args],
        out_specs=pl.BlockSpec((T, Hd), lambda i: (i, 0)),
        out_shape=jax.ShapeDtypeStruct((N, Hd), F32),
        compiler_params=_cparams(("parallel",)),
        name="hyena_hidden",
    )(*args)


def _fft_stage2(a_re, a_im, k1, twr_ref, twi_ref, f2_hi, f2_lo):
    r0 = pl.multiple_of(k1 * FFT_PITCH, SUBLANES)
    ar = a_re[pl.ds(r0, FFT_N2), :]
    ai = a_im[pl.ds(r0, FFT_N2), :]
    twr = twr_ref[k1]
    twi = twi_ref[k1]
    t = jnp.concatenate([ar * twr - ai * twi, ar * twi + ai * twr], axis=0)
    return _dotc(f2_hi, f2_lo, t, FFT_PASSES)


def _hy_filter_kernel(hid_ref, w3f_ref, w3b_ref, df_ref, db_ref, bias_ref, f1h_ref, f1l_ref, f2h_ref, f2l_ref,
                      twr_ref, twi_ref, kr_ref, ki_ref, u_ref, a_re, a_im, *, S, N1):
    N2, P = FFT_N2, FFT_PITCH
    T = 512
    nb = T // N2

    def gen_body(i, carry):
        r0 = pl.multiple_of(i * T, T)
        hid = hid_ref[pl.ds(r0, T), :]
        j = r0 + lax.broadcasted_iota(jnp.int32, (T, 1), 0)
        t = jnp.where(j < S, j, 2 * S - j).astype(F32) / (S - 1)
        kf = _dot3(hid, w3f_ref[...]) * (jnp.exp(-t * jnp.abs(df_ref[...])) + HYENA_WINDOW_SHIFT)
        kb = _dot3(hid, w3b_ref[...]) * (jnp.exp(-t * jnp.abs(db_ref[...])) + HYENA_WINDOW_SHIFT)
        ker = (jnp.where(j < S, kf, 0.0) + jnp.where((j > S) | (j == 0), kb, 0.0)
               + jnp.where(j == 0, bias_ref[0], 0.0))
        for b in range(nb):
            u_ref[pl.ds(pl.multiple_of((i * nb + b) * P, SUBLANES), N2), :] = ker[b * N2:(b + 1) * N2]
        return carry

    lax.fori_loop(0, 2 * S // T, gen_body, 0)

    def stage1(n2, carry):
        x = u_ref[pl.ds(n2, N1, stride=P), :]
        a = _dotc(f1h_ref[...], f1l_ref[...], x, FFT_PASSES)
        a_re[pl.ds(n2, N1, stride=P), :] = a[:N1]
        a_im[pl.ds(n2, N1, stride=P), :] = a[N1:]
        return carry

    lax.fori_loop(0, N2, stage1, 0)

    def stage2(k1, carry):
        x = _fft_stage2(a_re, a_im, k1, twr_ref, twi_ref, f2h_ref[...], f2l_ref[...])
        r0 = pl.multiple_of(k1 * N2, N2)
        kr_ref[0, pl.ds(r0, N2), :] = x[:N2]
        ki_ref[0, pl.ds(r0, N2), :] = x[N2:]
        return carry

    lax.fori_loop(0, N1, stage2, 0)


def _hy_filter_spectrum(S, hid, w3, deltas, bias, N1, consts):
    N = 2 * S
    N2, P = FFT_N2, FFT_PITCH
    Hd = hid.shape[1]
    nblk = w3.shape[1] // (2 * HYENA_ORDER * LANES)
    f1h, f1l = consts["f1_real"]
    f2h, f2l = consts["f2"]
    full = lambda a: pl.BlockSpec(a.shape, lambda o, j: (0,) * a.ndim, pipeline_mode=pl.Buffered(1))
    colf = lambda o, j: (0, (2 * o) * nblk + j)
    colb = lambda o, j: (0, (2 * o + 1) * nblk + j)
    out = jax.ShapeDtypeStruct((HYENA_ORDER, N, nblk * LANES), F32)
    return pl.pallas_call(
        functools.partial(_hy_filter_kernel, S=S, N1=N1),
        grid=(HYENA_ORDER, nblk),
        in_specs=[full(hid),
                  pl.BlockSpec((Hd, LANES), colf), pl.BlockSpec((Hd, LANES), colb),
                  pl.BlockSpec((1, LANES), colf), pl.BlockSpec((1, LANES), colb),
                  pl.BlockSpec((1, 1, LANES), lambda o, j: (o, 0, j)),
                  full(f1h), full(f1l), full(f2h), full(f2l), full(consts["tw_re"]), full(consts["tw_im"])],
        out_specs=[pl.BlockSpec((1, N, LANES), lambda o, j: (o, 0, j))] * 2,
        out_shape=[out, out],
        scratch_shapes=[pltpu.VMEM((N1 * P, LANES), F32)] * 3,
        compiler_params=_cparams(("parallel", "parallel")),
        name="hyena_filter_spectrum",
    )(hid, w3, w3, deltas.reshape(1, -1), deltas.reshape(1, -1), bias.reshape(HYENA_ORDER, 1, -1),
      f1h, f1l, f2h, f2l, consts["tw_re"], consts["tw_im"])


def _hy_conv_kernel(u_ref, x_ref, kr_ref, ki_ref, wu_ref, bu_ref, wx_ref, bx_ref, ng_ref,
                    f1h_ref, f1l_ref, f2h_ref, f2l_ref, f2ih_ref, f2il_ref, g1h_ref, g1l_ref, twr_ref, twi_ref,
                    o_ref, u_re, u_im, a_re, a_im, *, N1, conv_u, last):
    N2, P = FFT_N2, FFT_PITCH
    nh = N1 // 2
    rows = lax.broadcasted_iota(jnp.int32, (N2, LANES), 0)

    def load_body(c, carry):
        r0 = pl.multiple_of(c * P, SUBLANES)
        for b, dst in ((0, u_re), (1, u_im)):
            if conv_u:
                dst[pl.ds(r0, N2), :] = _short_conv_chunk(u_ref, b, c, nh, wu_ref, bu_ref, rows)
            else:
                dst[pl.ds(r0, N2), :] = u_ref[b, pl.ds(pl.multiple_of(c * N2, N2), N2), :]
        return carry

    lax.fori_loop(0, nh, load_body, 0)

    def stage1(n2, carry):
        x = jnp.concatenate([u_re[pl.ds(n2, nh, stride=P), :], u_im[pl.ds(n2, nh, stride=P), :]], axis=0)
        a = _dotc(f1h_ref[...], f1l_ref[...], x, FFT_PASSES)
        a_re[pl.ds(n2, N1, stride=P), :] = a[:N1]
        a_im[pl.ds(n2, N1, stride=P), :] = a[N1:]
        return carry

    lax.fori_loop(0, N2, stage1, 0)

    def stage2(k1, carry):
        x = _fft_stage2(a_re, a_im, k1, twr_ref, twi_ref, f2h_ref[...], f2l_ref[...])
        s0 = pl.multiple_of(k1 * N2, N2)
        kr = kr_ref[0, pl.ds(s0, N2), :]
        ki = ki_ref[0, pl.ds(s0, N2), :]
        xr, xi = x[:N2], x[N2:]
        y = jnp.concatenate([xr * kr - xi * ki, xr * ki + xi * kr], axis=0)
        b = _dotc(f2ih_ref[...], f2il_ref[...], y, FFT_PASSES)
        br, bi = b[:N2], b[N2:]
        twr = twr_ref[k1]
        twi = twi_ref[k1]
        r0 = pl.multiple_of(k1 * P, SUBLANES)
        a_re[pl.ds(r0, N2), :] = br * twr + bi * twi
        a_im[pl.ds(r0, N2), :] = bi * twr - br * twi
        return carry

    lax.fori_loop(0, N1, stage2, 0)

    def inv2(n2, carry):
        b = jnp.concatenate([a_re[pl.ds(n2, N1, stride=P), :], a_im[pl.ds(n2, N1, stride=P), :]], axis=0)
        y = _dotc(g1h_ref[...], g1l_ref[...], b, FFT_PASSES)
        u_re[pl.ds(n2, nh, stride=P), :] = y[:nh]
        u_im[pl.ds(n2, nh, stride=P), :] = y[nh:]
        return carry

    lax.fori_loop(0, N2, inv2, 0)

    def out_body(c, carry):
        r0 = pl.multiple_of(c * P, SUBLANES)
        t0 = pl.multiple_of(c * N2, N2)
        for b, src in ((0, u_re), (1, u_im)):
            z = _short_conv_chunk(x_ref, b, c, nh, wx_ref, bx_ref, rows) * src[pl.ds(r0, N2), :]
            o_ref[b, pl.ds(t0, N2), :] = _rms(z, ng_ref[...]) if last else z
        return carry

    lax.fori_loop(0, nh, out_body, 0)


def _hy_conv(u, u_blk, x, x_blk, kr, ki, order, conv_w, conv_b, norm_g, N1, consts, conv_u, last):
    B, S, _ = u.shape
    N2, P = FFT_N2, FFT_PITCH
    N = 2 * S
    nblk = kr.shape[2] // LANES
    pair = 2
    single = pl.Buffered(1)
    full = lambda a: pl.BlockSpec(a.shape, lambda j, p: (0,) * a.ndim, pipeline_mode=single)
    seq = lambda off: pl.BlockSpec((pair, S, LANES), lambda j, p: (p, 0, off + j), pipeline_mode=single)
    vec = lambda r, off: pl.BlockSpec((r, LANES), lambda j, p: (0, off + j))
    spec = pl.BlockSpec((1, N, LANES), lambda j, p: (order, 0, j), pipeline_mode=single)
    mats = [*consts["f1_half"], *consts["f2"], *consts["f2_inv"], *consts["g1"], consts["tw_re"], consts["tw_im"]]
    return pl.pallas_call(
        functools.partial(_hy_conv_kernel, N1=N1, conv_u=conv_u, last=last),
        grid=(nblk, B // pair),
        in_specs=[seq(u_blk), seq(x_blk), spec, spec,
                  vec(3, u_blk if conv_u else 0), vec(1, u_blk if conv_u else 0), vec(3, x_blk), vec(1, x_blk),
                  pl.BlockSpec((1, LANES), lambda j, p: (0, j))] + [full(m) for m in mats],
        out_specs=pl.BlockSpec((pair, S, LANES), lambda j, p: (p, 0, j)),
        out_shape=jax.ShapeDtypeStruct((B, S, nblk * LANES), F32),
        scratch_shapes=[pltpu.VMEM((N1 // 2 * P, LANES), F32)] * 2 + [pltpu.VMEM((N1 * P, LANES), F32)] * 2,
        compiler_params=_cparams(("parallel", "parallel")),
        name=f"hyena_conv{order}",
    )(u, x, kr, ki, conv_w, conv_b.reshape(1, -1), conv_w, conv_b.reshape(1, -1), norm_g.reshape(1, -1), *mats)


def _hyena(hx, conv_w, conv_b, w1, b1, w2, b2, w3, freq, deltas, bias, norm_g):
    B, S, C3 = hx.shape
    nblk = C3 // (HYENA_ORDER + 1) // LANES
    N1, N2, consts = _dft_constants(S)
    hid = _hy_hidden(S, w1, b1, w2, b2, freq)
    kr, ki = _hy_filter_spectrum(S, hid, w3, deltas, bias, N1, consts)
    z1 = _hy_conv(hx, 2 * nblk, hx, 0, kr, ki, 0, conv_w, conv_b, norm_g, N1, consts, True, False)
    return _hy_conv(z1, 0, hx, nblk, kr, ki, 1, conv_w, conv_b, norm_g, N1, consts, False, True)


def _outproj_kernel(ym_ref, yh_ref, x_ref, gt_ref, sc_ref, sh_ref, g_ref, wm_ref, wh_ref, wr_ref,
                    x1_ref, hf_ref, aff_ref):
    mixed = _dot(ym_ref[0].astype(BF16), wm_ref[...]) + _dot(yh_ref[0].astype(BF16), wh_ref[...])
    x1 = x_ref[0] + gt_ref[0] * mixed
    x1_ref[0] = x1
    hf = _rms(x1, g_ref[...]) * (1.0 + sc_ref[0]) + sh_ref[0]
    hf_ref[0] = hf.astype(BF16)
    logits = _dot3_nt(wr_ref[...], hf)
    e = jnp.exp(logits - jnp.max(logits, axis=0, keepdims=True))
    aff_ref[0] = e / jnp.sum(e, axis=0, keepdims=True)


def _out_projection(y_m, y_h, x, gt1, sc2, sh2, g_ffn, w_out, w_router, ts=512):
    B, S, D = x.shape
    Wm = y_m.shape[2]
    E = w_router.shape[1]
    wm = w_out[:Wm].astype(BF16)
    wh = w_out[Wm:].astype(BF16)
    tile = lambda w: pl.BlockSpec((1, ts, w), lambda b, i: (b, i, 0))
    row = pl.BlockSpec((1, 1, D), lambda b, i: (b, 0, 0))
    full = lambda a: pl.BlockSpec(a.shape, lambda b, i: (0,) * a.ndim)
    wr = w_router.T
    g = g_ffn.reshape(1, D)
    return pl.pallas_call(
        _outproj_kernel,
        grid=(B, S // ts),
        in_specs=[tile(Wm), tile(y_h.shape[2]), tile(D), row, row, row, full(g), full(wm), full(wh), full(wr)],
        out_specs=[tile(D), tile(D), pl.BlockSpec((1, E, ts), lambda b, i: (b, 0, i))],
        out_shape=[jax.ShapeDtypeStruct((B, S, D), F32), jax.ShapeDtypeStruct((B, S, D), BF16),
                   jax.ShapeDtypeStruct((B, E, S), F32)],
        compiler_params=_cparams(("parallel", "parallel")),
        name="out_proj_router",
    )(y_m, y_h, x, gt1.reshape(B, 1, D), sc2.reshape(B, 1, D), sh2.reshape(B, 1, D), g, wm, wh, wr)


def _route_kernel(aff_ref, pos_ref, *, S, cap):
    aff = aff_ref[0]
    E = aff.shape[0]
    tpos = lax.broadcasted_iota(jnp.int32, (E, S), 1)
    count = lambda mask: jnp.sum(jnp.where(mask, 1.0, 0.0), axis=1, keepdims=True)
    as_float = lambda word: lax.bitcast_convert_type(word, F32)

    def value_step(i, prefix):
        cand = prefix | jnp.left_shift(1, 30 - i)
        return jnp.where(count(aff >= as_float(cand)) >= cap, cand, prefix)

    thresh = as_float(lax.fori_loop(0, 31, value_step, jnp.zeros((E, 1), jnp.int32)))
    gt = aff > thresh
    eq = aff == thresh
    need = cap - count(gt)
    nbits = S.bit_length() - 1

    def index_step(i, x):
        cand = x | jnp.left_shift(1, nbits - 1 - i)
        return jnp.where(count(eq & (tpos < cand)) < need, cand, x)

    last_tie = lax.fori_loop(0, nbits, index_step, jnp.zeros((E, 1), jnp.int32))
    sel = jnp.where(gt | (eq & (tpos <= last_tie)), 1.0, 0.0)
    before = (lax.broadcasted_iota(jnp.int32, (LANES, LANES), 0)
              < lax.broadcasted_iota(jnp.int32, (LANES, LANES), 1)).astype(BF16)
    carry = jnp.zeros((E, 1), F32)
    for c in range(S // LANES):
        sc = sel[:, c * LANES:(c + 1) * LANES]
        rank = _dot(sc.astype(BF16), before) + carry
        pos_ref[0, :, c * LANES:(c + 1) * LANES] = jnp.where(sc > 0.0, rank, -1.0).astype(jnp.int32)
        carry = carry + jnp.sum(sc, axis=1, keepdims=True)


def _route(aff_t, cap):
    B, E, S = aff_t.shape
    blk = pl.BlockSpec((1, E, S), lambda b: (b, 0, 0))
    return pl.pallas_call(
        functools.partial(_route_kernel, S=S, cap=cap),
        grid=(B,),
        in_specs=[blk],
        out_specs=blk,
        out_shape=jax.ShapeDtypeStruct((B, E, S), jnp.int32),
        compiler_params=_cparams(("parallel",)),
        name="route_topk",
    )(aff_t)


META_ROWS = 16


def _gather_kernel(pos_ref, aff_ref, hf_ref, xs_ref, meta_ref, acc_ref, macc_ref, *, S, cap, tk):
    slot = lax.broadcasted_iota(jnp.int32, (cap, tk), 0)
    mrow = lax.broadcasted_iota(jnp.int32, (META_ROWS, tk), 0)
    lane = lax.broadcasted_iota(jnp.int32, (1, tk), 1)
    acc_ref[...] = jnp.zeros_like(acc_ref)
    macc_ref[...] = jnp.zeros_like(macc_ref)

    def body(c, carry):
        t0 = pl.multiple_of(c * tk, tk)
        onehot = jnp.where(slot == pos_ref[0, 0, :, pl.ds(t0, tk)], 1.0, 0.0).astype(BF16)
        acc_ref[...] += _dot(onehot, hf_ref[0, pl.ds(t0, tk), :])
        a = aff_ref[0, 0, :, pl.ds(t0, tk)]
        hi = a.astype(BF16).astype(F32)
        mid = (a - hi).astype(BF16).astype(F32)
        lo = (a - hi - mid).astype(BF16).astype(F32)
        t = t0 + lane
        pieces = (hi, mid, lo, (t // 64).astype(F32), (t % 64).astype(F32))
        meta = jnp.zeros((META_ROWS, tk), F32)
        for r, piece in enumerate(pieces):
            meta = jnp.where(mrow == r, piece, meta)
        macc_ref[...] += _dot_nt(meta.astype(BF16), onehot)
        return carry

    lax.fori_loop(0, S // tk, body, 0)
    xs_ref[0, 0] = acc_ref[...].astype(BF16)
    meta_ref[0, 0] = macc_ref[...]


def _gather(pos, aff_t, hf, cap, tk=512):
    B, E, S = pos.shape
    D = hf.shape[2]
    row = pl.BlockSpec((1, 1, 1, S), lambda b, e: (b, e, 0, 0))
    return pl.pallas_call(
        functools.partial(_gather_kernel, S=S, cap=cap, tk=tk),
        grid=(B, E),
        in_specs=[row, row, pl.BlockSpec((1, S, D), lambda b, e: (b, 0, 0))],
        out_specs=[pl.BlockSpec((1, 1, cap, D), lambda b, e: (b, e, 0, 0)),
                   pl.BlockSpec((1, 1, META_ROWS, cap), lambda b, e: (b, e, 0, 0))],
        out_shape=[jax.ShapeDtypeStruct((B, E, cap, D), BF16), jax.ShapeDtypeStruct((B, E, META_ROWS, cap), F32)],
        scratch_shapes=[pltpu.VMEM((cap, D), F32), pltpu.VMEM((META_ROWS, cap), F32)],
        compiler_params=_cparams(("parallel", "arbitrary")),
        name="moe_gather",
    )(pos.reshape(B, E, 1, S), aff_t.reshape(B, E, 1, S), hf)


def _ffn_kernel(xs_ref, meta_ref, wg_ref, wu_ref, wd_ref, y_ref, *, tf):
    x = xs_ref[0, 0]
    cap = x.shape[0]
    FF = wg_ref.shape[2]
    y = jnp.zeros((cap, wd_ref.shape[2]), F32)
    for f in range(FF // tf):
        hg = _dot(x, wg_ref[0, :, f * tf:(f + 1) * tf])
        hu = _dot(x, wu_ref[0, :, f * tf:(f + 1) * tf])
        hid = (hg * _sigmoid(hg) * hu).astype(BF16)
        y = y + _dot(hid, wd_ref[0, f * tf:(f + 1) * tf, :])
    m = meta_ref[0, 0]
    g_row = m[0:1] + m[1:2] + m[2:3]
    eye = lax.broadcasted_iota(jnp.int32, (cap, cap), 0) == lax.broadcasted_iota(jnp.int32, (cap, cap), 1)
    g_col = jnp.sum(jnp.where(eye, g_row, 0.0), axis=1, keepdims=True)
    y_ref[0, 0] = (y * g_col).astype(BF16)


def _expert_ffn(xs, meta, w_gate, w_up, w_down, tf=512):
    B, E, cap, D = xs.shape
    FF = w_gate.shape[2]
    return pl.pallas_call(
        functools.partial(_ffn_kernel, tf=tf),
        grid=(E, B),
        in_specs=[pl.BlockSpec((1, 1, cap, D), lambda e, b: (b, e, 0, 0)),
                  pl.BlockSpec((1, 1, META_ROWS, cap), lambda e, b: (b, e, 0, 0)),
                  pl.BlockSpec((1, D, FF), lambda e, b: (e, 0, 0)),
                  pl.BlockSpec((1, D, FF), lambda e, b: (e, 0, 0)),
                  pl.BlockSpec((1, FF, D), lambda e, b: (e, 0, 0))],
        out_specs=pl.BlockSpec((1, 1, cap, D), lambda e, b: (b, e, 0, 0)),
        out_shape=jax.ShapeDtypeStruct((B, E, cap, D), BF16),
        compiler_params=_cparams(("parallel", "arbitrary")),
        name="moe_ffn",
    )(xs, meta, w_gate.astype(BF16), w_up.astype(BF16), w_down.astype(BF16))


def _scatter_kernel(y_ref, meta_ref, x1_ref, gt_ref, g_ref, o_ref, *, tt):
    E, cap = y_ref.shape[1], y_ref.shape[2]
    tok = pl.program_id(1) * tt + lax.broadcasted_iota(jnp.int32, (tt, cap), 0)

    def body(e, acc):
        m = meta_ref[0, e]
        idx = (m[3:4] * 64.0 + m[4:5]).astype(jnp.int32)
        onehot = jnp.where(tok == idx, 1.0, 0.0).astype(BF16)
        return acc + _dot(onehot, y_ref[0, e])

    moe = lax.fori_loop(0, E, body, jnp.zeros((tt, y_ref.shape[3]), F32))
    o_ref[0] = _rms(x1_ref[0] + gt_ref[0] * moe, g_ref[...])


def _scatter_final(y, meta, x1, gt2, g_final, tt=512):
    B, E, cap, D = y.shape
    S = x1.shape[1]
    tile = pl.BlockSpec((1, tt, D), lambda b, i: (b, i, 0))
    return pl.pallas_call(
        functools.partial(_scatter_kernel, tt=tt),
        grid=(B, S // tt),
        in_specs=[pl.BlockSpec((1, E, cap, D), lambda b, i: (b, 0, 0, 0)),
                  pl.BlockSpec((1, E, META_ROWS, cap), lambda b, i: (b, 0, 0, 0)),
                  tile, pl.BlockSpec((1, 1, D), lambda b, i: (b, 0, 0)), pl.BlockSpec((1, D), lambda b, i: (0, 0))],
        out_specs=tile,
        out_shape=jax.ShapeDtypeStruct((B, S, D), F32),
        compiler_params=_cparams(("parallel", "parallel")),
        name="moe_scatter_final",
    )(y, meta, x1, gt2.reshape(B, 1, D), g_final.reshape(1, D))


def kernel(x, c, w_ada, b_ada, g_mix, w_in, b_in, conv_qk_w, conv_qk_b, mlstm_norm_g, conv_hy_w, conv_hy_b,
           hy_w1, hy_b1, hy_w2, hy_b2, hy_w3, hy_freq, hy_deltas, hy_bias, hyena_norm_g, w_out, g_ffn,
           w_router, w_gate, w_up, w_down, g_final):
    S = x.shape[1]
    cap = EC_CAPACITY_FACTOR * S // N_EXPERTS
    assert w_ada.shape[0] == 1, "single-layer block: the final RMSNorm is fused into the MoE scatter"
    l = 0
    mod = _modulation(c, w_ada[l], b_ada[l])
    sh1, sc1, gt1, sh2, sc2, gt2 = jnp.split(mod, 6, axis=-1)
    qkvo, hx, gates_t = _in_projection(x, sc1, sh1, g_mix[l], w_in[l], b_in[l])
    y_m = _mlstm(qkvo, gates_t, conv_qk_w[l], conv_qk_b[l], mlstm_norm_g[l])
    y_h = _hyena(hx, conv_hy_w[l], conv_hy_b[l], hy_w1[l], hy_b1[l], hy_w2[l], hy_b2[l], hy_w3[l],
                 hy_freq[l], hy_deltas[l], hy_bias[l], hyena_norm_g[l])
    x1, hf, aff_t = _out_projection(y_m, y_h, x, gt1, sc2, sh2, g_ffn[l], w_out[l], w_router[l])
    pos = _route(aff_t, cap)
    xs, meta = _gather(pos, aff_t, hf, cap)
    y = _expert_ffn(xs, meta, w_gate[l], w_up[l], w_down[l])
    return _scatter_final(y, meta, x1, gt2, g_final)
```

```python
import functools
import math

import numpy as np
import jax
import jax.numpy as jnp
from jax import lax
from jax.experimental import pallas as pl
from jax.experimental.pallas import tpu as pltpu

F32 = jnp.float32
BF16 = jnp.bfloat16

MLSTM_HEADS = 4
HEAD_DIM = 128
MLSTM_CHUNK = 128
MLSTM_M_INIT = -1e30
HYENA_GROUP_DIM = 128
HYENA_ORDER = 2
HYENA_BANDS = 16
HYENA_WINDOW_SHIFT = 0.05
N_GATE_COLS = 4 * MLSTM_HEADS
N_EXPERTS = 16
EC_CAPACITY_FACTOR = 2
RMS_EPS = 1e-6

LANES = 128
SUBLANES = 8
FFT_N2 = 64
FFT_PITCH = 72
NEG_BIG = -1e30
VMEM_LIMIT = 56 * 1024 * 1024


def _cparams(sem, vmem=None):
    return pltpu.CompilerParams(dimension_semantics=sem, vmem_limit_bytes=vmem or VMEM_LIMIT)


def _split(a):
    hi = a.astype(BF16)
    lo = (a - hi.astype(F32)).astype(BF16)
    return hi, lo


def _dot(a, b):
    return jnp.dot(a, b, preferred_element_type=F32)


def _dot_nt(a, b):
    return lax.dot_general(a, b, (((1,), (1,)), ((), ())), preferred_element_type=F32)


def _dot3(a, b):
    ah, al = _split(a)
    bh, bl = _split(b)
    return _dot(ah, bh) + _dot(ah, bl) + _dot(al, bh)


def _dot3_nt(a, b):
    ah, al = _split(a)
    bh, bl = _split(b)
    return _dot_nt(ah, bh) + _dot_nt(ah, bl) + _dot_nt(al, bh)


def _dotc(w_hi, w_lo, x, passes):
    xh, xl = _split(x)
    out = _dot(w_hi, xh)
    if passes == 3:
        out = out + _dot(w_hi, xl) + _dot(w_lo, xh)
    return out


def _rms(x, g):
    return x * lax.rsqrt(jnp.mean(x * x, axis=-1, keepdims=True) + RMS_EPS) * g


def _sigmoid(x):
    return 1.0 / (1.0 + jnp.exp(-x))


def _log_sigmoid(x):
    return jnp.minimum(x, 0.0) - jnp.log(1.0 + jnp.exp(-jnp.abs(x)))


def _mod_kernel(c_ref, w_ref, b_ref, o_ref):
    o_ref[...] = _dot3(c_ref[...], w_ref[...]) + b_ref[...]


def _modulation(c, w_ada, b_ada):
    B, D = c.shape
    n = w_ada.shape[1]
    tn = 768
    return pl.pallas_call(
        _mod_kernel,
        grid=(n // tn,),
        in_specs=[pl.BlockSpec((B, D), lambda j: (0, 0)),
                  pl.BlockSpec((D, tn), lambda j: (0, j)),
                  pl.BlockSpec((1, tn), lambda j: (0, j))],
        out_specs=pl.BlockSpec((B, tn), lambda j: (0, j)),
        out_shape=jax.ShapeDtypeStruct((B, n), F32),
        compiler_params=_cparams(("parallel",)),
        name="adaln_mod",
    )(c, w_ada, b_ada.reshape(1, n))


def _inproj_kernel(x_ref, sc_ref, sh_ref, g_ref, wq_ref, bq_ref, wh_ref, bh_ref, wgt_ref, bgt_ref,
                   qkvo_ref, hx_ref, gt_ref):
    x = x_ref[0]
    h = _rms(x, g_ref[...]) * (1.0 + sc_ref[0]) + sh_ref[0]
    hb = h.astype(BF16)
    qkvo_ref[0] = _dot(hb, wq_ref[...]) + bq_ref[...]
    hx_ref[0] = _dot(hb, wh_ref[...]) + bh_ref[...]
    gt_ref[0] = _dot3_nt(wgt_ref[...], h) + bgt_ref[...]


def _in_projection(x, sc1, sh1, g_mix, w_in, b_in, ts=512):
    B, S, D = x.shape
    nq = 4 * MLSTM_HEADS * HEAD_DIM
    nh = w_in.shape[1] - nq - N_GATE_COLS
    wq = w_in[:, :nq].astype(BF16)
    wh = w_in[:, nq + N_GATE_COLS:].astype(BF16)
    bq = b_in[:nq].reshape(1, nq)
    bh = b_in[nq + N_GATE_COLS:].reshape(1, nh)
    ng = MLSTM_HEADS * SUBLANES
    pad = lambda t: jnp.pad(t.reshape(4, MLSTM_HEADS, -1).transpose(1, 0, 2), ((0, 0), (0, 4), (0, 0))).reshape(ng, -1)
    wgt = pad(w_in[:, nq:nq + N_GATE_COLS].T)
    bgt = pad(b_in[nq:nq + N_GATE_COLS].reshape(N_GATE_COLS, 1))
    const = lambda b, i: (0, 0)
    return pl.pallas_call(
        _inproj_kernel,
        grid=(B, S // ts),
        in_specs=[pl.BlockSpec((1, ts, D), lambda b, i: (b, i, 0)),
                  pl.BlockSpec((1, 1, D), lambda b, i: (b, 0, 0)),
                  pl.BlockSpec((1, 1, D), lambda b, i: (b, 0, 0)),
                  pl.BlockSpec((1, D), const),
                  pl.BlockSpec((D, nq), const), pl.BlockSpec((1, nq), const),
                  pl.BlockSpec((D, nh), const), pl.BlockSpec((1, nh), const),
                  pl.BlockSpec((ng, D), const), pl.BlockSpec((ng, 1), const)],
        out_specs=[pl.BlockSpec((1, ts, nq), lambda b, i: (b, i, 0)),
                   pl.BlockSpec((1, ts, nh), lambda b, i: (b, i, 0)),
                   pl.BlockSpec((1, ng, ts), lambda b, i: (b, 0, i))],
        out_shape=[jax.ShapeDtypeStruct((B, S, nq), F32),
                   jax.ShapeDtypeStruct((B, S, nh), F32),
                   jax.ShapeDtypeStruct((B, ng, S), F32)],
        compiler_params=_cparams(("parallel", "parallel")),
        name="in_proj",
    )(x, sc1.reshape(B, 1, D), sh1.reshape(B, 1, D), g_mix.reshape(1, D), wq, bq, wh, bh, wgt, bgt)


def _short_conv_chunk(src_ref, lead, c, nc, w_ref, b_ref, rows):
    L = rows.shape[0]
    S = nc * L
    t0 = pl.multiple_of(c * L, L)
    cur = src_ref[lead, pl.ds(t0, L), :]
    p0 = pl.multiple_of(jnp.maximum(t0 - SUBLANES, 0), SUBLANES)
    n0 = pl.multiple_of(jnp.minimum(t0 + L, S - SUBLANES), SUBLANES)
    prev_row = src_ref[lead, pl.ds(p0, SUBLANES), :][SUBLANES - 1:SUBLANES, :]
    next_row = src_ref[lead, pl.ds(n0, SUBLANES), :][0:1, :]
    prev_row = jnp.where(c > 0, prev_row, 0.0)
    next_row = jnp.where(c < nc - 1, next_row, 0.0)
    up = jnp.where(rows == 0, prev_row, pltpu.roll(cur, 1, 0))
    dn = jnp.where(rows == L - 1, next_row, pltpu.roll(cur, L - 1, 0))
    return b_ref[...] + up * w_ref[0:1, :] + cur * w_ref[1:2, :] + dn * w_ref[2:3, :]


def _mlstm_kernel(q_ref, k_ref, v_ref, o_ref, gt_ref, wq_ref, wk_ref, bq_ref, bk_ref, ng_ref,
                  y_ref, qc_ref, kc_ref, hb_ref, c_ref, *, nc):
    L = MLSTM_CHUNK
    rows = lax.broadcasted_iota(jnp.int32, (L, LANES), 0)
    cols = lax.broadcasted_iota(jnp.int32, (L, LANES), 1)
    eye = rows == cols
    k_scale = HEAD_DIM ** -0.5

    def conv_body(c, carry):
        t0 = pl.multiple_of(c * L, L)
        yq = _short_conv_chunk(q_ref, 0, c, nc, wq_ref, bq_ref, rows)
        qc_ref[pl.ds(t0, L), :] = yq * _sigmoid(yq)
        yk = _short_conv_chunk(k_ref, 0, c, nc, wk_ref, bk_ref, rows)
        kc_ref[pl.ds(t0, L), :] = yk * _sigmoid(yk) * k_scale
        return carry

    lax.fori_loop(0, nc, conv_body, 0)

    def chunk_step(c, rev, n, m):
        t0 = pl.multiple_of(c * L, L)
        q = qc_ref[pl.ds(t0, L), :]
        k = kc_ref[pl.ds(t0, L), :]
        v = v_ref[0, pl.ds(t0, L), :]
        i_row = gt_ref[0, 0, (2 if rev else 0):(3 if rev else 1), pl.ds(t0, L)]
        f_row = gt_ref[0, 0, (3 if rev else 1):(4 if rev else 2), pl.ds(t0, L)]
        logf = _log_sigmoid(f_row)
        tri = (cols >= rows) if rev else (cols <= rows)
        b_col = jnp.sum(jnp.where(tri, logf, 0.0), axis=1, keepdims=True)
        b_row = jnp.sum(jnp.where(eye, b_col, 0.0), axis=0, keepdims=True)
        i_col = jnp.sum(jnp.where(eye, i_row, 0.0), axis=1, keepdims=True)
        g = jnp.sum(logf, axis=1, keepdims=True)
        log_d = jnp.where(tri, b_col - b_row + i_row, NEG_BIG)
        log_inter = b_col + m
        m_t = jnp.maximum(log_inter, jnp.max(log_d, axis=1, keepdims=True))
        d = jnp.exp(log_d - m_t)
        e_inter = jnp.exp(log_inter - m_t)
        qb = q.astype(BF16)
        kb = k.astype(BF16)
        vb = v.astype(BF16)
        c_in = c_ref[1 if rev else 0]
        s = _dot_nt(qb, kb) * d
        num = _dot(s.astype(BF16), vb) + e_inter * _dot_nt(qb, c_in.astype(BF16))
        den = jnp.sum(s, axis=1, keepdims=True) + e_inter * jnp.sum(q * n, axis=1, keepdims=True)
        nrm = jnp.maximum(jnp.abs(den), jnp.exp(-m_t))
        h_out = num / nrm
        w_end = g - b_col + i_col
        a = jnp.max(w_end, axis=0, keepdims=True)
        e_end = jnp.exp(w_end - a)
        c_loc = lax.dot_general((v * e_end).astype(BF16), kb, (((0,), (0,)), ((), ())),
                                preferred_element_type=F32)
        n_loc = jnp.sum(k * e_end, axis=0, keepdims=True)
        m_new = jnp.maximum(g + m, a)
        s_prev = jnp.exp(g + m - m_new)
        s_loc = jnp.exp(a - m_new)
        c_ref[1 if rev else 0] = s_prev * c_in + s_loc * c_loc
        return h_out, s_prev * n + s_loc * n_loc, m_new

    c_ref[...] = jnp.zeros_like(c_ref)
    n0 = jnp.zeros((1, HEAD_DIM), F32)
    m0 = jnp.full((1, 1), MLSTM_M_INIT, F32)

    def scan_body(j, carry):
        n_f, m_f, n_b, m_b = carry
        h_f, n_f, m_f = chunk_step(j, False, n_f, m_f)
        y_ref[0, pl.ds(pl.multiple_of(j * L, L), L), :] = h_f
        cb = nc - 1 - j
        h_b, n_b, m_b = chunk_step(cb, True, n_b, m_b)
        hb_ref[pl.ds(pl.multiple_of(cb * L, L), L), :] = h_b
        return n_f, m_f, n_b, m_b

    lax.fori_loop(0, nc, scan_body, (n0, m0, n0, m0), unroll=2)

    def final_body(c, carry):
        t0 = pl.multiple_of(c * L, L)
        hs = y_ref[0, pl.ds(t0, L), :] + hb_ref[pl.ds(t0, L), :]
        y_ref[0, pl.ds(t0, L), :] = _sigmoid(o_ref[0, pl.ds(t0, L), :]) * _rms(hs, ng_ref[...])
        return carry

    lax.fori_loop(0, nc, final_body, 0)


def _mlstm(qkvo, gates_t, conv_w, conv_b, norm_g):
    B, S, _ = qkvo.shape
    H, Dh = MLSTM_HEADS, HEAD_DIM
    nc = S // MLSTM_CHUNK
    seq = lambda off: pl.BlockSpec((1, S, Dh), lambda b, h: (b, 0, off + h))
    vec = lambda r, off: pl.BlockSpec((r, Dh), lambda b, h: (0, off + h))
    return pl.pallas_call(
        functools.partial(_mlstm_kernel, nc=nc),
        grid=(B, H),
        in_specs=[seq(0), seq(H), seq(2 * H), seq(3 * H),
                  pl.BlockSpec((1, 1, SUBLANES, S), lambda b, h: (b, h, 0, 0)),
                  vec(3, 0), vec(3, H), vec(1, 0), vec(1, H), vec(1, 0)],
        out_specs=pl.BlockSpec((1, S, Dh), lambda b, h: (b, 0, h)),
        out_shape=jax.ShapeDtypeStruct((B, S, H * Dh), F32),
        scratch_shapes=[pltpu.VMEM((S, Dh), F32), pltpu.VMEM((S, Dh), F32), pltpu.VMEM((S, Dh), F32),
                        pltpu.VMEM((2, Dh, Dh), F32)],
        compiler_params=_cparams(("parallel", "arbitrary")),
        name="mlstm",
    )(qkvo, qkvo, qkvo, qkvo, gates_t.reshape(B, H, SUBLANES, S), conv_w, conv_w, conv_b.reshape(1, -1), conv_b.reshape(1, -1),
      norm_g.reshape(1, -1))


FFT_PASSES = 3
FFT_UNROLL = 4


def _hilo(m):
    m32 = jnp.asarray(m, F32)
    hi = m32.astype(BF16)
    return hi, (m32 - hi.astype(F32)).astype(BF16)


def _stack_complex(m):
    return np.block([[m.real, -m.imag], [m.imag, m.real]])


def _dft_constants(S):
    N = 2 * S
    N2 = FFT_N2
    N1 = N // N2
    k1 = np.arange(N1)
    n2 = np.arange(N2)
    f1 = np.exp(-2j * np.pi * np.outer(k1, np.arange(N1)) / N1)
    f2 = np.exp(-2j * np.pi * np.outer(n2, n2) / N2)
    tw = np.exp(-2j * np.pi * np.outer(k1, n2) / N)
    g1 = np.conj(f1).T[:N1 // 2] / N
    consts = dict(
        f1_real=_hilo(np.concatenate([f1.real, f1.imag], axis=0)),
        f1_half=_hilo(_stack_complex(f1[:, :N1 // 2])),
        f2=_hilo(_stack_complex(f2)),
        f2_inv=_hilo(_stack_complex(np.conj(f2))),
        g1=_hilo(_stack_complex(g1)),
        tw_re=jnp.asarray(np.broadcast_to(tw.real[:, :, None], (N1, N2, LANES)), F32),
        tw_im=jnp.asarray(np.broadcast_to(tw.imag[:, :, None], (N1, N2, LANES)), F32),
    )
    return N1, N2, consts


def _hy_hidden_kernel(w1t_ref, w1c_ref, w1s_ref, b1_ref, w2_ref, b2_ref, fr_ref, o_ref, *, S, T):
    j = pl.program_id(0) * T + lax.broadcasted_iota(jnp.int32, (T, 1), 0)
    p = jnp.where(j < S, j, 2 * S - j).astype(F32)
    t = p / (S - 1)
    w = (2.0 * math.pi) * p / S
    band = lax.broadcasted_iota(jnp.int32, (1, HYENA_BANDS), 1).astype(F32)
    bands = 1e-4 + band * ((HYENA_BANDS - 1 - 1e-4) / (HYENA_BANDS - 1))
    arg = bands * w
    pre = t * w1t_ref[...] + _dot3(jnp.cos(arg), w1c_ref[...]) + _dot3(-jnp.sin(arg), w1s_ref[...]) + b1_ref[...]
    hid = jnp.sin(fr_ref[...] * pre)
    o_ref[...] = jnp.sin(fr_ref[...] * (_dot3(hid, w2_ref[...]) + b2_ref[...]))


def _hy_hidden(S, w1, b1, w2, b2, freq):
    N = 2 * S
    T = 1024
    Hd = w2.shape[0]
    full = lambda a: pl.BlockSpec(a.shape, lambda i: (0,) * a.ndim)
    args = (w1[0:1], w1[1:1 + HYENA_BANDS], w1[1 + HYENA_BANDS:], b1.reshape(1, Hd), w2, b2.reshape(1, Hd),
            freq.reshape(1, Hd))
    return pl.pallas_call(
        functools.partial(_hy_hidden_kernel, S=S, T=T),
        grid=(N // T,),
        in_specs=[full(a) for a in args],
        out_specs=pl.BlockSpec((T, Hd), lambda i: (i, 0)),
        out_shape=jax.ShapeDtypeStruct((N, Hd), F32),
        compiler_params=_cparams(("parallel",)),
        name="hyena_hidden",
    )(*args)


def _fft_stage2(a_re, a_im, k1, twr_ref, twi_ref, f2_hi, f2_lo):
    r0 = pl.multiple_of(k1 * FFT_PITCH, SUBLANES)
    ar = a_re[pl.ds(r0, FFT_N2), :]
    ai = a_im[pl.ds(r0, FFT_N2), :]
    twr = twr_ref[k1]
    twi = twi_ref[k1]
    t = jnp.concatenate([ar * twr - ai * twi, ar * twi + ai * twr], axis=0)
    return _dotc(f2_hi, f2_lo, t, FFT_PASSES)


def _hy_filter_kernel(hid_ref, w3f_ref, w3b_ref, df_ref, db_ref, bias_ref, f1h_ref, f1l_ref, f2h_ref, f2l_ref,
                      twr_ref, twi_ref, kr_ref, ki_ref, u_ref, a_re, a_im, *, S, N1):
    N2, P = FFT_N2, FFT_PITCH
    T = 512
    nb = T // N2

    def gen_body(i, carry):
        r0 = pl.multiple_of(i * T, T)
        hid = hid_ref[pl.ds(r0, T), :]
        j = r0 + lax.broadcasted_iota(jnp.int32, (T, 1), 0)
        t = jnp.where(j < S, j, 2 * S - j).astype(F32) / (S - 1)
        kf = _dot3(hid, w3f_ref[...]) * (jnp.exp(-t * jnp.abs(df_ref[...])) + HYENA_WINDOW_SHIFT)
        kb = _dot3(hid, w3b_ref[...]) * (jnp.exp(-t * jnp.abs(db_ref[...])) + HYENA_WINDOW_SHIFT)
        ker = (jnp.where(j < S, kf, 0.0) + jnp.where((j > S) | (j == 0), kb, 0.0)
               + jnp.where(j == 0, bias_ref[0], 0.0))
        for b in range(nb):
            u_ref[pl.ds(pl.multiple_of((i * nb + b) * P, SUBLANES), N2), :] = ker[b * N2:(b + 1) * N2]
        return carry

    lax.fori_loop(0, 2 * S // T, gen_body, 0)

    def stage1(n2, carry):
        x = u_ref[pl.ds(n2, N1, stride=P), :]
        a = _dotc(f1h_ref[...], f1l_ref[...], x, FFT_PASSES)
        a_re[pl.ds(n2, N1, stride=P), :] = a[:N1]
        a_im[pl.ds(n2, N1, stride=P), :] = a[N1:]
        return carry

    lax.fori_loop(0, N2, stage1, 0, unroll=FFT_UNROLL)

    def stage2(k1, carry):
        x = _fft_stage2(a_re, a_im, k1, twr_ref, twi_ref, f2h_ref[...], f2l_ref[...])
        r0 = pl.multiple_of(k1 * N2, N2)
        kr_ref[0, pl.ds(r0, N2), :] = x[:N2]
        ki_ref[0, pl.ds(r0, N2), :] = x[N2:]
        return carry

    lax.fori_loop(0, N1, stage2, 0, unroll=2 * FFT_UNROLL)


def _hy_filter_spectrum(S, hid, w3, deltas, bias, N1, consts):
    N = 2 * S
    N2, P = FFT_N2, FFT_PITCH
    Hd = hid.shape[1]
    nblk = w3.shape[1] // (2 * HYENA_ORDER * LANES)
    f1h, f1l = consts["f1_real"]
    f2h, f2l = consts["f2"]
    full = lambda a: pl.BlockSpec(a.shape, lambda o, j: (0,) * a.ndim, pipeline_mode=pl.Buffered(1))
    colf = lambda o, j: (0, (2 * o) * nblk + j)
    colb = lambda o, j: (0, (2 * o + 1) * nblk + j)
    out = jax.ShapeDtypeStruct((HYENA_ORDER, N, nblk * LANES), F32)
    return pl.pallas_call(
        functools.partial(_hy_filter_kernel, S=S, N1=N1),
        grid=(HYENA_ORDER, nblk),
        in_specs=[full(hid),
                  pl.BlockSpec((Hd, LANES), colf), pl.BlockSpec((Hd, LANES), colb),
                  pl.BlockSpec((1, LANES), colf), pl.BlockSpec((1, LANES), colb),
                  pl.BlockSpec((1, 1, LANES), lambda o, j: (o, 0, j)),
                  full(f1h), full(f1l), full(f2h), full(f2l), full(consts["tw_re"]), full(consts["tw_im"])],
        out_specs=[pl.BlockSpec((1, N, LANES), lambda o, j: (o, 0, j))] * 2,
        out_shape=[out, out],
        scratch_shapes=[pltpu.VMEM((N1 * P, LANES), F32)] * 3,
        compiler_params=_cparams(("parallel", "parallel")),
        name="hyena_filter_spectrum",
    )(hid, w3, w3, deltas.reshape(1, -1), deltas.reshape(1, -1), bias.reshape(HYENA_ORDER, 1, -1),
      f1h, f1l, f2h, f2l, consts["tw_re"], consts["tw_im"])


def _hy_conv_kernel(u_ref, x_ref, kr_ref, ki_ref, wu_ref, bu_ref, wx_ref, bx_ref, ng_ref,
                    f1h_ref, f1l_ref, f2h_ref, f2l_ref, f2ih_ref, f2il_ref, g1h_ref, g1l_ref, twr_ref, twi_ref,
                    o_ref, u_re, u_im, a_re, a_im, *, N1, conv_u, last):
    N2, P = FFT_N2, FFT_PITCH
    nh = N1 // 2
    rows = lax.broadcasted_iota(jnp.int32, (N2, LANES), 0)

    def load_body(c, carry):
        r0 = pl.multiple_of(c * P, SUBLANES)
        for b, dst in ((0, u_re), (1, u_im)):
            if conv_u:
                dst[pl.ds(r0, N2), :] = _short_conv_chunk(u_ref, b, c, nh, wu_ref, bu_ref, rows)
            else:
                dst[pl.ds(r0, N2), :] = u_ref[b, pl.ds(pl.multiple_of(c * N2, N2), N2), :]
        return carry

    lax.fori_loop(0, nh, load_body, 0)

    def stage1(n2, carry):
        x = jnp.concatenate([u_re[pl.ds(n2, nh, stride=P), :], u_im[pl.ds(n2, nh, stride=P), :]], axis=0)
        a = _dotc(f1h_ref[...], f1l_ref[...], x, FFT_PASSES)
        a_re[pl.ds(n2, N1, stride=P), :] = a[:N1]
        a_im[pl.ds(n2, N1, stride=P), :] = a[N1:]
        return carry

    lax.fori_loop(0, N2, stage1, 0, unroll=FFT_UNROLL)

    def stage2(k1, carry):
        x = _fft_stage2(a_re, a_im, k1, twr_ref, twi_ref, f2h_ref[...], f2l_ref[...])
        s0 = pl.multiple_of(k1 * N2, N2)
        kr = kr_ref[0, pl.ds(s0, N2), :]
        ki = ki_ref[0, pl.ds(s0, N2), :]
        xr, xi = x[:N2], x[N2:]
        y = jnp.concatenate([xr * kr - xi * ki, xr * ki + xi * kr], axis=0)
        b = _dotc(f2ih_ref[...], f2il_ref[...], y, FFT_PASSES)
        br, bi = b[:N2], b[N2:]
        twr = twr_ref[k1]
        twi = twi_ref[k1]
        r0 = pl.multiple_of(k1 * P, SUBLANES)
        a_re[pl.ds(r0, N2), :] = br * twr + bi * twi
        a_im[pl.ds(r0, N2), :] = bi * twr - br * twi
        return carry

    lax.fori_loop(0, N1, stage2, 0, unroll=2 * FFT_UNROLL)

    def inv2(n2, carry):
        b = jnp.concatenate([a_re[pl.ds(n2, N1, stride=P), :], a_im[pl.ds(n2, N1, stride=P), :]], axis=0)
        y = _dotc(g1h_ref[...], g1l_ref[...], b, FFT_PASSES)
        u_re[pl.ds(n2, nh, stride=P), :] = y[:nh]
        u_im[pl.ds(n2, nh, stride=P), :] = y[nh:]
        return carry

    lax.fori_loop(0, N2, inv2, 0, unroll=FFT_UNROLL)

    def out_body(c, carry):
        r0 = pl.multiple_of(c * P, SUBLANES)
        t0 = pl.multiple_of(c * N2, N2)
        for b, src in ((0, u_re), (1, u_im)):
            z = _short_conv_chunk(x_ref, b, c, nh, wx_ref, bx_ref, rows) * src[pl.ds(r0, N2), :]
            o_ref[b, pl.ds(t0, N2), :] = _rms(z, ng_ref[...]) if last else z
        return carry

    lax.fori_loop(0, nh, out_body, 0)


def _hy_conv(u, u_blk, x, x_blk, kr, ki, order, conv_w, conv_b, norm_g, N1, consts, conv_u, last):
    B, S, _ = u.shape
    N2, P = FFT_N2, FFT_PITCH
    N = 2 * S
    nblk = kr.shape[2] // LANES
    pair = 2
    single = pl.Buffered(1)
    full = lambda a: pl.BlockSpec(a.shape, lambda j, p: (0,) * a.ndim, pipeline_mode=single)
    seq = lambda off: pl.BlockSpec((pair, S, LANES), lambda j, p: (p, 0, off + j), pipeline_mode=single)
    vec = lambda r, off: pl.BlockSpec((r, LANES), lambda j, p: (0, off + j))
    spec = pl.BlockSpec((1, N, LANES), lambda j, p: (order, 0, j), pipeline_mode=single)
    mats = [*consts["f1_half"], *consts["f2"], *consts["f2_inv"], *consts["g1"], consts["tw_re"], consts["tw_im"]]
    return pl.pallas_call(
        functools.partial(_hy_conv_kernel, N1=N1, conv_u=conv_u, last=last),
        grid=(nblk, B // pair),
        in_specs=[seq(u_blk), seq(x_blk), spec, spec,
                  vec(3, u_blk if conv_u else 0), vec(1, u_blk if conv_u else 0), vec(3, x_blk), vec(1, x_blk),
                  pl.BlockSpec((1, LANES), lambda j, p: (0, j))] + [full(m) for m in mats],
        out_specs=pl.BlockSpec((pair, S, LANES), lambda j, p: (p, 0, j)),
        out_shape=jax.ShapeDtypeStruct((B, S, nblk * LANES), F32),
        scratch_shapes=[pltpu.VMEM((N1 // 2 * P, LANES), F32)] * 2 + [pltpu.VMEM((N1 * P, LANES), F32)] * 2,
        compiler_params=_cparams(("parallel", "parallel")),
        name=f"hyena_conv{order}",
    )(u, x, kr, ki, conv_w, conv_b.reshape(1, -1), conv_w, conv_b.reshape(1, -1), norm_g.reshape(1, -1), *mats)


def _hyena(hx, conv_w, conv_b, w1, b1, w2, b2, w3, freq, deltas, bias, norm_g):
    B, S, C3 = hx.shape
    nblk = C3 // (HYENA_ORDER + 1) // LANES
    N1, N2, consts = _dft_constants(S)
    hid = _hy_hidden(S, w1, b1, w2, b2, freq)
    kr, ki = _hy_filter_spectrum(S, hid, w3, deltas, bias, N1, consts)
    z1 = _hy_conv(hx, 2 * nblk, hx, 0, kr, ki, 0, conv_w, conv_b, norm_g, N1, consts, True, False)
    return _hy_conv(z1, 0, hx, nblk, kr, ki, 1, conv_w, conv_b, norm_g, N1, consts, False, True)


def _outproj_kernel(ym_ref, yh_ref, x_ref, gt_ref, sc_ref, sh_ref, g_ref, wm_ref, wh_ref, wr_ref,
                    x1_ref, hf_ref, aff_ref):
    mixed = _dot(ym_ref[0].astype(BF16), wm_ref[...]) + _dot(yh_ref[0].astype(BF16), wh_ref[...])
    x1 = x_ref[0] + gt_ref[0] * mixed
    x1_ref[0] = x1
    hf = _rms(x1, g_ref[...]) * (1.0 + sc_ref[0]) + sh_ref[0]
    hf_ref[0] = hf.astype(BF16)
    logits = _dot3_nt(wr_ref[...], hf)
    e = jnp.exp(logits - jnp.max(logits, axis=0, keepdims=True))
    aff_ref[0] = e / jnp.sum(e, axis=0, keepdims=True)


def _out_projection(y_m, y_h, x, gt1, sc2, sh2, g_ffn, w_out, w_router, ts=512):
    B, S, D = x.shape
    Wm = y_m.shape[2]
    E = w_router.shape[1]
    wm = w_out[:Wm].astype(BF16)
    wh = w_out[Wm:].astype(BF16)
    tile = lambda w: pl.BlockSpec((1, ts, w), lambda b, i: (b, i, 0))
    row = pl.BlockSpec((1, 1, D), lambda b, i: (b, 0, 0))
    full = lambda a: pl.BlockSpec(a.shape, lambda b, i: (0,) * a.ndim)
    wr = w_router.T
    g = g_ffn.reshape(1, D)
    return pl.pallas_call(
        _outproj_kernel,
        grid=(B, S // ts),
        in_specs=[tile(Wm), tile(y_h.shape[2]), tile(D), row, row, row, full(g), full(wm), full(wh), full(wr)],
        out_specs=[tile(D), tile(D), pl.BlockSpec((1, E, ts), lambda b, i: (b, 0, i))],
        out_shape=[jax.ShapeDtypeStruct((B, S, D), F32), jax.ShapeDtypeStruct((B, S, D), BF16),
                   jax.ShapeDtypeStruct((B, E, S), F32)],
        compiler_params=_cparams(("parallel", "parallel")),
        name="out_proj_router",
    )(y_m, y_h, x, gt1.reshape(B, 1, D), sc2.reshape(B, 1, D), sh2.reshape(B, 1, D), g, wm, wh, wr)


def _route_kernel(aff_ref, pos_ref, *, S, cap):
    aff = aff_ref[0]
    E = aff.shape[0]
    tpos = lax.broadcasted_iota(jnp.int32, (E, S), 1)
    count = lambda mask: jnp.sum(jnp.where(mask, 1.0, 0.0), axis=1, keepdims=True)
    as_float = lambda word: lax.bitcast_convert_type(word, F32)

    def value_step(i, prefix):
        cand = prefix | jnp.left_shift(1, 30 - i)
        return jnp.where(count(aff >= as_float(cand)) >= cap, cand, prefix)

    thresh = as_float(lax.fori_loop(0, 31, value_step, jnp.zeros((E, 1), jnp.int32)))
    gt = aff > thresh
    eq = aff == thresh
    need = cap - count(gt)
    nbits = S.bit_length() - 1

    def index_step(i, x):
        cand = x | jnp.left_shift(1, nbits - 1 - i)
        return jnp.where(count(eq & (tpos < cand)) < need, cand, x)

    last_tie = lax.fori_loop(0, nbits, index_step, jnp.zeros((E, 1), jnp.int32))
    sel = jnp.where(gt | (eq & (tpos <= last_tie)), 1.0, 0.0)
    before = (lax.broadcasted_iota(jnp.int32, (LANES, LANES), 0)
              < lax.broadcasted_iota(jnp.int32, (LANES, LANES), 1)).astype(BF16)
    carry = jnp.zeros((E, 1), F32)
    for c in range(S // LANES):
        sc = sel[:, c * LANES:(c + 1) * LANES]
        rank = _dot(sc.astype(BF16), before) + carry
        pos_ref[0, :, c * LANES:(c + 1) * LANES] = jnp.where(sc > 0.0, rank, -1.0).astype(jnp.int32)
        carry = carry + jnp.sum(sc, axis=1, keepdims=True)


def _route(aff_t, cap):
    B, E, S = aff_t.shape
    blk = pl.BlockSpec((1, E, S), lambda b: (b, 0, 0))
    return pl.pallas_call(
        functools.partial(_route_kernel, S=S, cap=cap),
        grid=(B,),
        in_specs=[blk],
        out_specs=blk,
        out_shape=jax.ShapeDtypeStruct((B, E, S), jnp.int32),
        compiler_params=_cparams(("parallel",)),
        name="route_topk",
    )(aff_t)


META_ROWS = 16


def _gather_kernel(pos_ref, aff_ref, hf_ref, xs_ref, meta_ref, acc_ref, macc_ref, *, S, cap, tk):
    slot = lax.broadcasted_iota(jnp.int32, (cap, tk), 0)
    mrow = lax.broadcasted_iota(jnp.int32, (META_ROWS, tk), 0)
    lane = lax.broadcasted_iota(jnp.int32, (1, tk), 1)
    acc_ref[...] = jnp.zeros_like(acc_ref)
    macc_ref[...] = jnp.zeros_like(macc_ref)

    def body(c, carry):
        t0 = pl.multiple_of(c * tk, tk)
        onehot = jnp.where(slot == pos_ref[0, 0, :, pl.ds(t0, tk)], 1.0, 0.0).astype(BF16)
        acc_ref[...] += _dot(onehot, hf_ref[0, pl.ds(t0, tk), :])
        a = aff_ref[0, 0, :, pl.ds(t0, tk)]
        hi = a.astype(BF16).astype(F32)
        mid = (a - hi).astype(BF16).astype(F32)
        lo = (a - hi - mid).astype(BF16).astype(F32)
        t = t0 + lane
        pieces = (hi, mid, lo, (t // 64).astype(F32), (t % 64).astype(F32))
        meta = jnp.zeros((META_ROWS, tk), F32)
        for r, piece in enumerate(pieces):
            meta = jnp.where(mrow == r, piece, meta)
        macc_ref[...] += _dot_nt(meta.astype(BF16), onehot)
        return carry

    lax.fori_loop(0, S // tk, body, 0)
    xs_ref[0, 0] = acc_ref[...].astype(BF16)
    meta_ref[0, 0] = macc_ref[...]


def _gather(pos, aff_t, hf, cap, tk=512):
    B, E, S = pos.shape
    D = hf.shape[2]
    row = pl.BlockSpec((1, 1, 1, S), lambda b, e: (b, e, 0, 0))
    return pl.pallas_call(
        functools.partial(_gather_kernel, S=S, cap=cap, tk=tk),
        grid=(B, E),
        in_specs=[row, row, pl.BlockSpec((1, S, D), lambda b, e: (b, 0, 0))],
        out_specs=[pl.BlockSpec((1, 1, cap, D), lambda b, e: (b, e, 0, 0)),
                   pl.BlockSpec((1, 1, META_ROWS, cap), lambda b, e: (b, e, 0, 0))],
        out_shape=[jax.ShapeDtypeStruct((B, E, cap, D), BF16), jax.ShapeDtypeStruct((B, E, META_ROWS, cap), F32)],
        scratch_shapes=[pltpu.VMEM((cap, D), F32), pltpu.VMEM((META_ROWS, cap), F32)],
        compiler_params=_cparams(("parallel", "arbitrary")),
        name="moe_gather",
    )(pos.reshape(B, E, 1, S), aff_t.reshape(B, E, 1, S), hf)


def _ffn_kernel(xs_ref, meta_ref, wg_ref, wu_ref, wd_ref, y_ref, *, tf):
    x = xs_ref[0, 0]
    cap = x.shape[0]
    FF = wg_ref.shape[2]
    y = jnp.zeros((cap, wd_ref.shape[2]), F32)
    for f in range(FF // tf):
        hg = _dot(x, wg_ref[0, :, f * tf:(f + 1) * tf])
        hu = _dot(x, wu_ref[0, :, f * tf:(f + 1) * tf])
        hid = (hg * _sigmoid(hg) * hu).astype(BF16)
        y = y + _dot(hid, wd_ref[0, f * tf:(f + 1) * tf, :])
    m = meta_ref[0, 0]
    g_row = m[0:1] + m[1:2] + m[2:3]
    eye = lax.broadcasted_iota(jnp.int32, (cap, cap), 0) == lax.broadcasted_iota(jnp.int32, (cap, cap), 1)
    g_col = jnp.sum(jnp.where(eye, g_row, 0.0), axis=1, keepdims=True)
    y_ref[0, 0] = (y * g_col).astype(BF16)


def _expert_ffn(xs, meta, w_gate, w_up, w_down, tf=512):
    B, E, cap, D = xs.shape
    FF = w_gate.shape[2]
    return pl.pallas_call(
        functools.partial(_ffn_kernel, tf=tf),
        grid=(E, B),
        in_specs=[pl.BlockSpec((1, 1, cap, D), lambda e, b: (b, e, 0, 0)),
                  pl.BlockSpec((1, 1, META_ROWS, cap), lambda e, b: (b, e, 0, 0)),
                  pl.BlockSpec((1, D, FF), lambda e, b: (e, 0, 0)),
                  pl.BlockSpec((1, D, FF), lambda e, b: (e, 0, 0)),
                  pl.BlockSpec((1, FF, D), lambda e, b: (e, 0, 0))],
        out_specs=pl.BlockSpec((1, 1, cap, D), lambda e, b: (b, e, 0, 0)),
        out_shape=jax.ShapeDtypeStruct((B, E, cap, D), BF16),
        compiler_params=_cparams(("parallel", "arbitrary")),
        name="moe_ffn",
    )(xs, meta, w_gate.astype(BF16), w_up.astype(BF16), w_down.astype(BF16))


def _scatter_kernel(y_ref, meta_ref, x1_ref, gt_ref, g_ref, o_ref, *, tt):
    E, cap = y_ref.shape[1], y_ref.shape[2]
    tok = pl.program_id(1) * tt + lax.broadcasted_iota(jnp.int32, (tt, cap), 0)

    def body(e, acc):
        m = meta_ref[0, e]
        idx = (m[3:4] * 64.0 + m[4:5]).astype(jnp.int32)
        onehot = jnp.where(tok == idx, 1.0, 0.0).astype(BF16)
        return acc + _dot(onehot, y_ref[0, e])

    moe = lax.fori_loop(0, E, body, jnp.zeros((tt, y_ref.shape[3]), F32))
    o_ref[0] = _rms(x1_ref[0] + gt_ref[0] * moe, g_ref[...])


def _scatter_final(y, meta, x1, gt2, g_final, tt=512):
    B, E, cap, D = y.shape
    S = x1.shape[1]
    tile = pl.BlockSpec((1, tt, D), lambda b, i: (b, i, 0))
    return pl.pallas_call(
        functools.partial(_scatter_kernel, tt=tt),
        grid=(B, S // tt),
        in_specs=[pl.BlockSpec((1, E, cap, D), lambda b, i: (b, 0, 0, 0)),
                  pl.BlockSpec((1, E, META_ROWS, cap), lambda b, i: (b, 0, 0, 0)),
                  tile, pl.BlockSpec((1, 1, D), lambda b, i: (b, 0, 0)), pl.BlockSpec((1, D), lambda b, i: (0, 0))],
        out_specs=tile,
        out_shape=jax.ShapeDtypeStruct((B, S, D), F32),
        compiler_params=_cparams(("parallel", "parallel")),
        name="moe_scatter_final",
    )(y, meta, x1, gt2.reshape(B, 1, D), g_final.reshape(1, D))


def kernel(x, c, w_ada, b_ada, g_mix, w_in, b_in, conv_qk_w, conv_qk_b, mlstm_norm_g, conv_hy_w, conv_hy_b,
           hy_w1, hy_b1, hy_w2, hy_b2, hy_w3, hy_freq, hy_deltas, hy_bias, hyena_norm_g, w_out, g_ffn,
           w_router, w_gate, w_up, w_down, g_final):
    S = x.shape[1]
    cap = EC_CAPACITY_FACTOR * S // N_EXPERTS
    assert w_ada.shape[0] == 1, "single-layer block: the final RMSNorm is fused into the MoE scatter"
    l = 0
    mod = _modulation(c, w_ada[l], b_ada[l])
    sh1, sc1, gt1, sh2, sc2, gt2 = jnp.split(mod, 6, axis=-1)
    qkvo, hx, gates_t = _in_projection(x, sc1, sh1, g_mix[l], w_in[l], b_in[l])
    y_m = _mlstm(qkvo, gates_t, conv_qk_w[l], conv_qk_b[l], mlstm_norm_g[l])
    y_h = _hyena(hx, conv_hy_w[l], conv_hy_b[l], hy_w1[l], hy_b1[l], hy_w2[l], hy_b2[l], hy_w3[l],
                 hy_freq[l], hy_deltas[l], hy_bias[l], hyena_norm_g[l])
    x1, hf, aff_t = _out_projection(y_m, y_h, x, gt1, sc2, sh2, g_ffn[l], w_out[l], w_router[l])
    pos = _route(aff_t, cap)
    xs, meta = _gather(pos, aff_t, hf, cap)
    y = _expert_ffn(xs, meta, w_gate[l], w_up[l], w_down[l])
    return _scatter_final(y, meta, x1, gt2, g_final)
```

```python
import functools
import math

import numpy as np
import jax
import jax.numpy as jnp
from jax import lax
from jax.experimental import pallas as pl
from jax.experimental.pallas import tpu as pltpu

F32 = jnp.float32
BF16 = jnp.bfloat16

MLSTM_HEADS = 4
HEAD_DIM = 128
MLSTM_CHUNK = 128
MLSTM_M_INIT = -1e30
HYENA_GROUP_DIM = 128
HYENA_ORDER = 2
HYENA_BANDS = 16
HYENA_WINDOW_SHIFT = 0.05
N_GATE_COLS = 4 * MLSTM_HEADS
N_EXPERTS = 16
EC_CAPACITY_FACTOR = 2
RMS_EPS = 1e-6

LANES = 128
SUBLANES = 8
FFT_N2 = 64
FFT_PITCH = 72
NEG_BIG = -1e30
VMEM_LIMIT = 56 * 1024 * 1024


def _cparams(sem, vmem=None):
    return pltpu.CompilerParams(dimension_semantics=sem, vmem_limit_bytes=vmem or VMEM_LIMIT)


def _split(a):
    hi = a.astype(BF16)
    lo = (a - hi.astype(F32)).astype(BF16)
    return hi, lo


def _dot(a, b):
    return jnp.dot(a, b, preferred_element_type=F32)


def _dot_nt(a, b):
    return lax.dot_general(a, b, (((1,), (1,)), ((), ())), preferred_element_type=F32)


def _dot3(a, b):
    ah, al = _split(a)
    bh, bl = _split(b)
    return _dot(ah, bh) + _dot(ah, bl) + _dot(al, bh)


def _dot3_nt(a, b):
    ah, al = _split(a)
    bh, bl = _split(b)
    return _dot_nt(ah, bh) + _dot_nt(ah, bl) + _dot_nt(al, bh)


def _dotc(w_hi, w_lo, x, passes):
    xh, xl = _split(x)
    out = _dot(w_hi, xh)
    if passes == 3:
        out = out + _dot(w_hi, xl) + _dot(w_lo, xh)
    return out


def _rms(x, g):
    return x * lax.rsqrt(jnp.mean(x * x, axis=-1, keepdims=True) + RMS_EPS) * g


def _sigmoid(x):
    return 1.0 / (1.0 + jnp.exp(-x))


def _log_sigmoid(x):
    return jnp.minimum(x, 0.0) - jnp.log(1.0 + jnp.exp(-jnp.abs(x)))


def _mod_kernel(c_ref, w_ref, b_ref, o_ref):
    o_ref[...] = _dot3(c_ref[...], w_ref[...]) + b_ref[...]


def _modulation(c, w_ada, b_ada):
    B, D = c.shape
    n = w_ada.shape[1]
    tn = 768
    return pl.pallas_call(
        _mod_kernel,
        grid=(n // tn,),
        in_specs=[pl.BlockSpec((B, D), lambda j: (0, 0)),
                  pl.BlockSpec((D, tn), lambda j: (0, j)),
                  pl.BlockSpec((1, tn), lambda j: (0, j))],
        out_specs=pl.BlockSpec((B, tn), lambda j: (0, j)),
        out_shape=jax.ShapeDtypeStruct((B, n), F32),
        compiler_params=_cparams(("parallel",)),
        name="adaln_mod",
    )(c, w_ada, b_ada.reshape(1, n))


def _inproj_kernel(x_ref, sc_ref, sh_ref, g_ref, wq_ref, bq_ref, wh_ref, bh_ref, wgt_ref, bgt_ref,
                   qkvo_ref, hx_ref, gt_ref):
    x = x_ref[0]
    h = _rms(x, g_ref[...]) * (1.0 + sc_ref[0]) + sh_ref[0]
    hb = h.astype(BF16)
    qkvo_ref[0] = _dot(hb, wq_ref[...]) + bq_ref[...]
    hx_ref[0] = _dot(hb, wh_ref[...]) + bh_ref[...]
    gt_ref[0] = _dot3_nt(wgt_ref[...], h) + bgt_ref[...]


def _in_projection(x, sc1, sh1, g_mix, w_in, b_in, ts=512):
    B, S, D = x.shape
    nq = 4 * MLSTM_HEADS * HEAD_DIM
    nh = w_in.shape[1] - nq - N_GATE_COLS
    wq = w_in[:, :nq].astype(BF16)
    wh = w_in[:, nq + N_GATE_COLS:].astype(BF16)
    bq = b_in[:nq].reshape(1, nq)
    bh = b_in[nq + N_GATE_COLS:].reshape(1, nh)
    ng = MLSTM_HEADS * SUBLANES
    pad = lambda t: jnp.pad(t.reshape(4, MLSTM_HEADS, -1).transpose(1, 0, 2), ((0, 0), (0, 4), (0, 0))).reshape(ng, -1)
    wgt = pad(w_in[:, nq:nq + N_GATE_COLS].T)
    bgt = pad(b_in[nq:nq + N_GATE_COLS].reshape(N_GATE_COLS, 1))
    const = lambda b, i: (0, 0)
    return pl.pallas_call(
        _inproj_kernel,
        grid=(B, S // ts),
        in_specs=[pl.BlockSpec((1, ts, D), lambda b, i: (b, i, 0)),
                  pl.BlockSpec((1, 1, D), lambda b, i: (b, 0, 0)),
                  pl.BlockSpec((1, 1, D), lambda b, i: (b, 0, 0)),
                  pl.BlockSpec((1, D), const),
                  pl.BlockSpec((D, nq), const), pl.BlockSpec((1, nq), const),
                  pl.BlockSpec((D, nh), const), pl.BlockSpec((1, nh), const),
                  pl.BlockSpec((ng, D), const), pl.BlockSpec((ng, 1), const)],
        out_specs=[pl.BlockSpec((1, ts, nq), lambda b, i: (b, i, 0)),
                   pl.BlockSpec((1, ts, nh), lambda b, i: (b, i, 0)),
                   pl.BlockSpec((1, ng, ts), lambda b, i: (b, 0, i))],
        out_shape=[jax.ShapeDtypeStruct((B, S, nq), F32),
                   jax.ShapeDtypeStruct((B, S, nh), F32),
                   jax.ShapeDtypeStruct((B, ng, S), F32)],
        compiler_params=_cparams(("parallel", "parallel")),
        name="in_proj",
    )(x, sc1.reshape(B, 1, D), sh1.reshape(B, 1, D), g_mix.reshape(1, D), wq, bq, wh, bh, wgt, bgt)


def _short_conv_chunk(src_ref, lead, c, nc, w_ref, b_ref, rows):
    L = rows.shape[0]
    S = nc * L
    t0 = pl.multiple_of(c * L, L)
    cur = src_ref[lead, pl.ds(t0, L), :]
    p0 = pl.multiple_of(jnp.maximum(t0 - SUBLANES, 0), SUBLANES)
    n0 = pl.multiple_of(jnp.minimum(t0 + L, S - SUBLANES), SUBLANES)
    prev_row = src_ref[lead, pl.ds(p0, SUBLANES), :][SUBLANES - 1:SUBLANES, :]
    next_row = src_ref[lead, pl.ds(n0, SUBLANES), :][0:1, :]
    prev_row = jnp.where(c > 0, prev_row, 0.0)
    next_row = jnp.where(c < nc - 1, next_row, 0.0)
    up = jnp.where(rows == 0, prev_row, pltpu.roll(cur, 1, 0))
    dn = jnp.where(rows == L - 1, next_row, pltpu.roll(cur, L - 1, 0))
    return b_ref[...] + up * w_ref[0:1, :] + cur * w_ref[1:2, :] + dn * w_ref[2:3, :]


def _mlstm_kernel(q_ref, k_ref, v_ref, o_ref, gt_ref, wq_ref, wk_ref, bq_ref, bk_ref, ng_ref,
                  y_ref, qc_ref, kc_ref, hb_ref, c_ref, *, nc):
    L = MLSTM_CHUNK
    rows = lax.broadcasted_iota(jnp.int32, (L, LANES), 0)
    cols = lax.broadcasted_iota(jnp.int32, (L, LANES), 1)
    eye = rows == cols
    k_scale = HEAD_DIM ** -0.5

    def conv_body(c, carry):
        t0 = pl.multiple_of(c * L, L)
        yq = _short_conv_chunk(q_ref, 0, c, nc, wq_ref, bq_ref, rows)
        qc_ref[pl.ds(t0, L), :] = yq * _sigmoid(yq)
        yk = _short_conv_chunk(k_ref, 0, c, nc, wk_ref, bk_ref, rows)
        kc_ref[pl.ds(t0, L), :] = yk * _sigmoid(yk) * k_scale
        return carry

    lax.fori_loop(0, nc, conv_body, 0)

    def chunk_step(c, rev, n, m):
        t0 = pl.multiple_of(c * L, L)
        q = qc_ref[pl.ds(t0, L), :]
        k = kc_ref[pl.ds(t0, L), :]
        v = v_ref[0, pl.ds(t0, L), :]
        i_row = gt_ref[0, 0, (2 if rev else 0):(3 if rev else 1), pl.ds(t0, L)]
        f_row = gt_ref[0, 0, (3 if rev else 1):(4 if rev else 2), pl.ds(t0, L)]
        logf = _log_sigmoid(f_row)
        tri = (cols >= rows) if rev else (cols <= rows)
        b_col = jnp.sum(jnp.where(tri, logf, 0.0), axis=1, keepdims=True)
        b_row = jnp.sum(jnp.where(eye, b_col, 0.0), axis=0, keepdims=True)
        i_col = jnp.sum(jnp.where(eye, i_row, 0.0), axis=1, keepdims=True)
        g = jnp.sum(logf, axis=1, keepdims=True)
        log_d = jnp.where(tri, b_col - b_row + i_row, NEG_BIG)
        log_inter = b_col + m
        m_t = jnp.maximum(log_inter, jnp.max(log_d, axis=1, keepdims=True))
        d = jnp.exp(log_d - m_t)
        e_inter = jnp.exp(log_inter - m_t)
        qb = q.astype(BF16)
        kb = k.astype(BF16)
        vb = v.astype(BF16)
        c_in = c_ref[1 if rev else 0]
        s = _dot_nt(qb, kb) * d
        num = _dot(s.astype(BF16), vb) + e_inter * _dot_nt(qb, c_in.astype(BF16))
        den = jnp.sum(s, axis=1, keepdims=True) + e_inter * jnp.sum(q * n, axis=1, keepdims=True)
        nrm = jnp.maximum(jnp.abs(den), jnp.exp(-m_t))
        h_out = num / nrm
        w_end = g - b_col + i_col
        a = jnp.max(w_end, axis=0, keepdims=True)
        e_end = jnp.exp(w_end - a)
        c_loc = lax.dot_general((v * e_end).astype(BF16), kb, (((0,), (0,)), ((), ())),
                                preferred_element_type=F32)
        n_loc = jnp.sum(k * e_end, axis=0, keepdims=True)
        m_new = jnp.maximum(g + m, a)
        s_prev = jnp.exp(g + m - m_new)
        s_loc = jnp.exp(a - m_new)
        c_ref[1 if rev else 0] = s_prev * c_in + s_loc * c_loc
        return h_out, s_prev * n + s_loc * n_loc, m_new

    c_ref[...] = jnp.zeros_like(c_ref)
    n0 = jnp.zeros((1, HEAD_DIM), F32)
    m0 = jnp.full((1, 1), MLSTM_M_INIT, F32)

    def scan_body(j, carry):
        n_f, m_f, n_b, m_b = carry
        h_f, n_f, m_f = chunk_step(j, False, n_f, m_f)
        y_ref[0, pl.ds(pl.multiple_of(j * L, L), L), :] = h_f
        cb = nc - 1 - j
        h_b, n_b, m_b = chunk_step(cb, True, n_b, m_b)
        hb_ref[pl.ds(pl.multiple_of(cb * L, L), L), :] = h_b
        return n_f, m_f, n_b, m_b

    lax.fori_loop(0, nc, scan_body, (n0, m0, n0, m0), unroll=2)

    def final_body(c, carry):
        t0 = pl.multiple_of(c * L, L)
        hs = y_ref[0, pl.ds(t0, L), :] + hb_ref[pl.ds(t0, L), :]
        y_ref[0, pl.ds(t0, L), :] = _sigmoid(o_ref[0, pl.ds(t0, L), :]) * _rms(hs, ng_ref[...])
        return carry

    lax.fori_loop(0, nc, final_body, 0)


def _mlstm(qkvo, gates_t, conv_w, conv_b, norm_g):
    B, S, _ = qkvo.shape
    H, Dh = MLSTM_HEADS, HEAD_DIM
    nc = S // MLSTM_CHUNK
    seq = lambda off: pl.BlockSpec((1, S, Dh), lambda b, h: (b, 0, off + h))
    vec = lambda r, off: pl.BlockSpec((r, Dh), lambda b, h: (0, off + h))
    return pl.pallas_call(
        functools.partial(_mlstm_kernel, nc=nc),
        grid=(B, H),
        in_specs=[seq(0), seq(H), seq(2 * H), seq(3 * H),
                  pl.BlockSpec((1, 1, SUBLANES, S), lambda b, h: (b, h, 0, 0)),
                  vec(3, 0), vec(3, H), vec(1, 0), vec(1, H), vec(1, 0)],
        out_specs=pl.BlockSpec((1, S, Dh), lambda b, h: (b, 0, h)),
        out_shape=jax.ShapeDtypeStruct((B, S, H * Dh), F32),
        scratch_shapes=[pltpu.VMEM((S, Dh), F32), pltpu.VMEM((S, Dh), F32), pltpu.VMEM((S, Dh), F32),
                        pltpu.VMEM((2, Dh, Dh), F32)],
        compiler_params=_cparams(("parallel", "arbitrary")),
        name="mlstm",
    )(qkvo, qkvo, qkvo, qkvo, gates_t.reshape(B, H, SUBLANES, S), conv_w, conv_w, conv_b.reshape(1, -1), conv_b.reshape(1, -1),
      norm_g.reshape(1, -1))


FILTER_PASSES = 3
CONV_PASSES = 1
FFT_UNROLL = 8


def _hilo(m):
    m32 = jnp.asarray(m, F32)
    hi = m32.astype(BF16)
    return hi, (m32 - hi.astype(F32)).astype(BF16)


def _stack_complex(m):
    return np.block([[m.real, -m.imag], [m.imag, m.real]])


def _dft_constants(S):
    N = 2 * S
    N2 = FFT_N2
    N1 = N // N2
    k1 = np.arange(N1)
    n2 = np.arange(N2)
    f1 = np.exp(-2j * np.pi * np.outer(k1, np.arange(N1)) / N1)
    f2 = np.exp(-2j * np.pi * np.outer(n2, n2) / N2)
    tw = np.exp(-2j * np.pi * np.outer(k1, n2) / N)
    g1 = np.conj(f1).T[:N1 // 2] / N
    consts = dict(
        f1_real=_hilo(np.concatenate([f1.real, f1.imag], axis=0)),
        f1_half=_hilo(_stack_complex(f1[:, :N1 // 2])),
        f2=_hilo(_stack_complex(f2)),
        f2_inv=_hilo(_stack_complex(np.conj(f2))),
        g1=_hilo(_stack_complex(g1)),
        tw_re=jnp.asarray(np.broadcast_to(tw.real[:, :, None], (N1, N2, LANES)), F32),
        tw_im=jnp.asarray(np.broadcast_to(tw.imag[:, :, None], (N1, N2, LANES)), F32),
    )
    return N1, N2, consts


def _hy_hidden_kernel(w1t_ref, w1c_ref, w1s_ref, b1_ref, w2_ref, b2_ref, fr_ref, o_ref, *, S, T):
    j = pl.program_id(0) * T + lax.broadcasted_iota(jnp.int32, (T, 1), 0)
    p = jnp.where(j < S, j, 2 * S - j).astype(F32)
    t = p / (S - 1)
    w = (2.0 * math.pi) * p / S
    band = lax.broadcasted_iota(jnp.int32, (1, HYENA_BANDS), 1).astype(F32)
    bands = 1e-4 + band * ((HYENA_BANDS - 1 - 1e-4) / (HYENA_BANDS - 1))
    arg = bands * w
    pre = t * w1t_ref[...] + _dot3(jnp.cos(arg), w1c_ref[...]) + _dot3(-jnp.sin(arg), w1s_ref[...]) + b1_ref[...]
    hid = jnp.sin(fr_ref[...] * pre)
    o_ref[...] = jnp.sin(fr_ref[...] * (_dot3(hid, w2_ref[...]) + b2_ref[...]))


def _hy_hidden(S, w1, b1, w2, b2, freq):
    N = 2 * S
    T = 1024
    Hd = w2.shape[0]
    full = lambda a: pl.BlockSpec(a.shape, lambda i: (0,) * a.ndim)
    args = (w1[0:1], w1[1:1 + HYENA_BANDS], w1[1 + HYENA_BANDS:], b1.reshape(1, Hd), w2, b2.reshape(1, Hd),
            freq.reshape(1, Hd))
    return pl.pallas_call(
        functools.partial(_hy_hidden_kernel, S=S, T=T),
        grid=(N // T,),
        in_specs=[full(a) for a in args],
        out_specs=pl.BlockSpec((T, Hd), lambda i: (i, 0)),
        out_shape=jax.ShapeDtypeStruct((N, Hd), F32),
        compiler_params=_cparams(("parallel",)),
        name="hyena_hidden",
    )(*args)


def _fft_stage2(a_re, a_im, k1, twr_ref, twi_ref, f2_hi, f2_lo, passes):
    r0 = pl.multiple_of(k1 * FFT_PITCH, SUBLANES)
    ar = a_re[pl.ds(r0, FFT_N2), :]
    ai = a_im[pl.ds(r0, FFT_N2), :]
    twr = twr_ref[k1]
    twi = twi_ref[k1]
    t = jnp.concatenate([ar * twr - ai * twi, ar * twi + ai * twr], axis=0)
    return _dotc(f2_hi, f2_lo, t, passes)


def _hy_filter_kernel(hid_ref, w3f_ref, w3b_ref, df_ref, db_ref, bias_ref, f1h_ref, f1l_ref, f2h_ref, f2l_ref,
                      twr_ref, twi_ref, kr_ref, ki_ref, u_ref, a_re, a_im, *, S, N1):
    N2, P = FFT_N2, FFT_PITCH
    T = 512
    nb = T // N2

    def gen_body(i, carry):
        r0 = pl.multiple_of(i * T, T)
        hid = hid_ref[pl.ds(r0, T), :]
        j = r0 + lax.broadcasted_iota(jnp.int32, (T, 1), 0)
        t = jnp.where(j < S, j, 2 * S - j).astype(F32) / (S - 1)
        kf = _dot3(hid, w3f_ref[...]) * (jnp.exp(-t * jnp.abs(df_ref[...])) + HYENA_WINDOW_SHIFT)
        kb = _dot3(hid, w3b_ref[...]) * (jnp.exp(-t * jnp.abs(db_ref[...])) + HYENA_WINDOW_SHIFT)
        ker = (jnp.where(j < S, kf, 0.0) + jnp.where((j > S) | (j == 0), kb, 0.0)
               + jnp.where(j == 0, bias_ref[0], 0.0))
        for b in range(nb):
            u_ref[pl.ds(pl.multiple_of((i * nb + b) * P, SUBLANES), N2), :] = ker[b * N2:(b + 1) * N2]
        return carry

    lax.fori_loop(0, 2 * S // T, gen_body, 0)

    def stage1(n2, carry):
        x = u_ref[pl.ds(n2, N1, stride=P), :]
        a = _dotc(f1h_ref[...], f1l_ref[...], x, FILTER_PASSES)
        a_re[pl.ds(n2, N1, stride=P), :] = a[:N1]
        a_im[pl.ds(n2, N1, stride=P), :] = a[N1:]
        return carry

    lax.fori_loop(0, N2, stage1, 0, unroll=FFT_UNROLL)

    def stage2(k1, carry):
        x = _fft_stage2(a_re, a_im, k1, twr_ref, twi_ref, f2h_ref[...], f2l_ref[...], FILTER_PASSES)
        r0 = pl.multiple_of(k1 * N2, N2)
        kr_ref[0, pl.ds(r0, N2), :] = x[:N2]
        ki_ref[0, pl.ds(r0, N2), :] = x[N2:]
        return carry

    lax.fori_loop(0, N1, stage2, 0, unroll=2 * FFT_UNROLL)


def _hy_filter_spectrum(S, hid, w3, deltas, bias, N1, consts):
    N = 2 * S
    N2, P = FFT_N2, FFT_PITCH
    Hd = hid.shape[1]
    nblk = w3.shape[1] // (2 * HYENA_ORDER * LANES)
    f1h, f1l = consts["f1_real"]
    f2h, f2l = consts["f2"]
    full = lambda a: pl.BlockSpec(a.shape, lambda o, j: (0,) * a.ndim, pipeline_mode=pl.Buffered(1))
    colf = lambda o, j: (0, (2 * o) * nblk + j)
    colb = lambda o, j: (0, (2 * o + 1) * nblk + j)
    out = jax.ShapeDtypeStruct((HYENA_ORDER, N, nblk * LANES), F32)
    return pl.pallas_call(
        functools.partial(_hy_filter_kernel, S=S, N1=N1),
        grid=(HYENA_ORDER, nblk),
        in_specs=[full(hid),
                  pl.BlockSpec((Hd, LANES), colf), pl.BlockSpec((Hd, LANES), colb),
                  pl.BlockSpec((1, LANES), colf), pl.BlockSpec((1, LANES), colb),
                  pl.BlockSpec((1, 1, LANES), lambda o, j: (o, 0, j)),
                  full(f1h), full(f1l), full(f2h), full(f2l), full(consts["tw_re"]), full(consts["tw_im"])],
        out_specs=[pl.BlockSpec((1, N, LANES), lambda o, j: (o, 0, j))] * 2,
        out_shape=[out, out],
        scratch_shapes=[pltpu.VMEM((N1 * P, LANES), F32)] * 3,
        compiler_params=_cparams(("parallel", "parallel")),
        name="hyena_filter_spectrum",
    )(hid, w3, w3, deltas.reshape(1, -1), deltas.reshape(1, -1), bias.reshape(HYENA_ORDER, 1, -1),
      f1h, f1l, f2h, f2l, consts["tw_re"], consts["tw_im"])


def _hy_conv_kernel(u_ref, x_ref, kr_ref, ki_ref, wu_ref, bu_ref, wx_ref, bx_ref, ng_ref,
                    f1h_ref, f1l_ref, f2h_ref, f2l_ref, f2ih_ref, f2il_ref, g1h_ref, g1l_ref, twr_ref, twi_ref,
                    o_ref, u_re, u_im, a_re, a_im, *, N1, conv_u, last):
    N2, P = FFT_N2, FFT_PITCH
    nh = N1 // 2
    rows = lax.broadcasted_iota(jnp.int32, (N2, LANES), 0)

    def load_body(c, carry):
        r0 = pl.multiple_of(c * P, SUBLANES)
        for b, dst in ((0, u_re), (1, u_im)):
            if conv_u:
                dst[pl.ds(r0, N2), :] = _short_conv_chunk(u_ref, b, c, nh, wu_ref, bu_ref, rows)
            else:
                dst[pl.ds(r0, N2), :] = u_ref[b, pl.ds(pl.multiple_of(c * N2, N2), N2), :]
        return carry

    lax.fori_loop(0, nh, load_body, 0)

    def stage1(n2, carry):
        x = jnp.concatenate([u_re[pl.ds(n2, nh, stride=P), :], u_im[pl.ds(n2, nh, stride=P), :]], axis=0)
        a = _dotc(f1h_ref[...], f1l_ref[...], x, CONV_PASSES)
        a_re[pl.ds(n2, N1, stride=P), :] = a[:N1]
        a_im[pl.ds(n2, N1, stride=P), :] = a[N1:]
        return carry

    lax.fori_loop(0, N2, stage1, 0, unroll=FFT_UNROLL)

    def stage2(k1, carry):
        x = _fft_stage2(a_re, a_im, k1, twr_ref, twi_ref, f2h_ref[...], f2l_ref[...], CONV_PASSES)
        s0 = pl.multiple_of(k1 * N2, N2)
        kr = kr_ref[0, pl.ds(s0, N2), :]
        ki = ki_ref[0, pl.ds(s0, N2), :]
        xr, xi = x[:N2], x[N2:]
        y = jnp.concatenate([xr * kr - xi * ki, xr * ki + xi * kr], axis=0)
        b = _dotc(f2ih_ref[...], f2il_ref[...], y, CONV_PASSES)
        br, bi = b[:N2], b[N2:]
        twr = twr_ref[k1]
        twi = twi_ref[k1]
        r0 = pl.multiple_of(k1 * P, SUBLANES)
        a_re[pl.ds(r0, N2), :] = br * twr + bi * twi
        a_im[pl.ds(r0, N2), :] = bi * twr - br * twi
        return carry

    lax.fori_loop(0, N1, stage2, 0, unroll=2 * FFT_UNROLL)

    def inv2(n2, carry):
        b = jnp.concatenate([a_re[pl.ds(n2, N1, stride=P), :], a_im[pl.ds(n2, N1, stride=P), :]], axis=0)
        y = _dotc(g1h_ref[...], g1l_ref[...], b, CONV_PASSES)
        u_re[pl.ds(n2, nh, stride=P), :] = y[:nh]
        u_im[pl.ds(n2, nh, stride=P), :] = y[nh:]
        return carry

    lax.fori_loop(0, N2, inv2, 0, unroll=FFT_UNROLL)

    def out_body(c, carry):
        r0 = pl.multiple_of(c * P, SUBLANES)
        t0 = pl.multiple_of(c * N2, N2)
        for b, src in ((0, u_re), (1, u_im)):
            z = _short_conv_chunk(x_ref, b, c, nh, wx_ref, bx_ref, rows) * src[pl.ds(r0, N2), :]
            o_ref[b, pl.ds(t0, N2), :] = _rms(z, ng_ref[...]) if last else z
        return carry

    lax.fori_loop(0, nh, out_body, 0)


def _hy_conv(u, u_blk, x, x_blk, kr, ki, order, conv_w, conv_b, norm_g, N1, consts, conv_u, last):
    B, S, _ = u.shape
    N2, P = FFT_N2, FFT_PITCH
    N = 2 * S
    nblk = kr.shape[2] // LANES
    pair = 2
    single = pl.Buffered(1)
    full = lambda a: pl.BlockSpec(a.shape, lambda j, p: (0,) * a.ndim, pipeline_mode=single)
    seq = lambda off: pl.BlockSpec((pair, S, LANES), lambda j, p: (p, 0, off + j), pipeline_mode=single)
    vec = lambda r, off: pl.BlockSpec((r, LANES), lambda j, p: (0, off + j))
    spec = pl.BlockSpec((1, N, LANES), lambda j, p: (order, 0, j), pipeline_mode=single)
    mats = [*consts["f1_half"], *consts["f2"], *consts["f2_inv"], *consts["g1"], consts["tw_re"], consts["tw_im"]]
    return pl.pallas_call(
        functools.partial(_hy_conv_kernel, N1=N1, conv_u=conv_u, last=last),
        grid=(nblk, B // pair),
        in_specs=[seq(u_blk), seq(x_blk), spec, spec,
                  vec(3, u_blk if conv_u else 0), vec(1, u_blk if conv_u else 0), vec(3, x_blk), vec(1, x_blk),
                  pl.BlockSpec((1, LANES), lambda j, p: (0, j))] + [full(m) for m in mats],
        out_specs=pl.BlockSpec((pair, S, LANES), lambda j, p: (p, 0, j)),
        out_shape=jax.ShapeDtypeStruct((B, S, nblk * LANES), F32),
        scratch_shapes=[pltpu.VMEM((N1 // 2 * P, LANES), F32)] * 2 + [pltpu.VMEM((N1 * P, LANES), F32)] * 2,
        compiler_params=_cparams(("parallel", "parallel")),
        name=f"hyena_conv{order}",
    )(u, x, kr, ki, conv_w, conv_b.reshape(1, -1), conv_w, conv_b.reshape(1, -1), norm_g.reshape(1, -1), *mats)


def _hyena(hx, conv_w, conv_b, w1, b1, w2, b2, w3, freq, deltas, bias, norm_g):
    B, S, C3 = hx.shape
    nblk = C3 // (HYENA_ORDER + 1) // LANES
    N1, N2, consts = _dft_constants(S)
    hid = _hy_hidden(S, w1, b1, w2, b2, freq)
    kr, ki = _hy_filter_spectrum(S, hid, w3, deltas, bias, N1, consts)
    z1 = _hy_conv(hx, 2 * nblk, hx, 0, kr, ki, 0, conv_w, conv_b, norm_g, N1, consts, True, False)
    return _hy_conv(z1, 0, hx, nblk, kr, ki, 1, conv_w, conv_b, norm_g, N1, consts, False, True)


def _outproj_kernel(ym_ref, yh_ref, x_ref, gt_ref, sc_ref, sh_ref, g_ref, wm_ref, wh_ref, wr_ref,
                    x1_ref, hf_ref, aff_ref):
    mixed = _dot(ym_ref[0].astype(BF16), wm_ref[...]) + _dot(yh_ref[0].astype(BF16), wh_ref[...])
    x1 = x_ref[0] + gt_ref[0] * mixed
    x1_ref[0] = x1
    hf = _rms(x1, g_ref[...]) * (1.0 + sc_ref[0]) + sh_ref[0]
    hf_ref[0] = hf.astype(BF16)
    logits = _dot3_nt(wr_ref[...], hf)
    e = jnp.exp(logits - jnp.max(logits, axis=0, keepdims=True))
    aff_ref[0] = e / jnp.sum(e, axis=0, keepdims=True)


def _out_projection(y_m, y_h, x, gt1, sc2, sh2, g_ffn, w_out, w_router, ts=512):
    B, S, D = x.shape
    Wm = y_m.shape[2]
    E = w_router.shape[1]
    wm = w_out[:Wm].astype(BF16)
    wh = w_out[Wm:].astype(BF16)
    tile = lambda w: pl.BlockSpec((1, ts, w), lambda b, i: (b, i, 0))
    row = pl.BlockSpec((1, 1, D), lambda b, i: (b, 0, 0))
    full = lambda a: pl.BlockSpec(a.shape, lambda b, i: (0,) * a.ndim)
    wr = w_router.T
    g = g_ffn.reshape(1, D)
    return pl.pallas_call(
        _outproj_kernel,
        grid=(B, S // ts),
        in_specs=[tile(Wm), tile(y_h.shape[2]), tile(D), row, row, row, full(g), full(wm), full(wh), full(wr)],
        out_specs=[tile(D), tile(D), pl.BlockSpec((1, E, ts), lambda b, i: (b, 0, i))],
        out_shape=[jax.ShapeDtypeStruct((B, S, D), F32), jax.ShapeDtypeStruct((B, S, D), BF16),
                   jax.ShapeDtypeStruct((B, E, S), F32)],
        compiler_params=_cparams(("parallel", "parallel")),
        name="out_proj_router",
    )(y_m, y_h, x, gt1.reshape(B, 1, D), sc2.reshape(B, 1, D), sh2.reshape(B, 1, D), g, wm, wh, wr)


def _route_kernel(aff_ref, pos_ref, cnt_ref, *, S, cap):
    aff = aff_ref[0]
    E = aff.shape[0]
    tpos = lax.broadcasted_iota(jnp.int32, (E, S), 1)
    count = lambda mask: jnp.sum(jnp.where(mask, 1.0, 0.0), axis=1, keepdims=True)
    as_float = lambda word: lax.bitcast_convert_type(word, F32)

    def value_step(i, prefix):
        cand = prefix | jnp.left_shift(1, 30 - i)
        return jnp.where(count(aff >= as_float(cand)) >= cap, cand, prefix)

    thresh = as_float(lax.fori_loop(0, 31, value_step, jnp.zeros((E, 1), jnp.int32)))
    gt = aff > thresh
    eq = aff == thresh
    need = cap - count(gt)
    nbits = S.bit_length() - 1

    def index_step(i, x):
        cand = x | jnp.left_shift(1, nbits - 1 - i)
        return jnp.where(count(eq & (tpos < cand)) < need, cand, x)

    last_tie = lax.fori_loop(0, nbits, index_step, jnp.zeros((E, 1), jnp.int32))
    sel = jnp.where(gt | (eq & (tpos <= last_tie)), 1.0, 0.0)
    before = (lax.broadcasted_iota(jnp.int32, (LANES, LANES), 0)
              < lax.broadcasted_iota(jnp.int32, (LANES, LANES), 1)).astype(BF16)
    carry = jnp.zeros((E, 1), F32)
    lane = lax.broadcasted_iota(jnp.int32, (E, LANES), 1)
    starts = jnp.zeros((E, LANES), F32)
    for c in range(S // LANES):
        sc = sel[:, c * LANES:(c + 1) * LANES]
        rank = _dot(sc.astype(BF16), before) + carry
        pos_ref[0, :, c * LANES:(c + 1) * LANES] = jnp.where(sc > 0.0, rank, -1.0).astype(jnp.int32)
        starts = jnp.where(lane == c, carry, starts)
        carry = carry + jnp.sum(sc, axis=1, keepdims=True)
    cnt_ref[0] = jnp.where(lane >= S // LANES, carry, starts).astype(jnp.int32)


def _route(aff_t, cap):
    B, E, S = aff_t.shape
    assert S // LANES < LANES
    blk = pl.BlockSpec((1, E, S), lambda b: (b, 0, 0))
    return pl.pallas_call(
        functools.partial(_route_kernel, S=S, cap=cap),
        grid=(B,),
        in_specs=[blk],
        out_specs=[blk, pl.BlockSpec((1, E, LANES), lambda b: (b, 0, 0))],
        out_shape=[jax.ShapeDtypeStruct((B, E, S), jnp.int32), jax.ShapeDtypeStruct((B, E, LANES), jnp.int32)],
        compiler_params=_cparams(("parallel",)),
        name="route_topk",
    )(aff_t)


def _window_tables(cnt, S, cap):
    B, E, _ = cnt.shape
    nch = S // LANES
    bounds = cnt[:, :, 1:nch + 1]
    first = jnp.arange(cap // GATHER_SLOTS, dtype=jnp.int32) * GATHER_SLOTS
    chunk_of = lambda slot: jnp.sum((bounds[:, :, None, :] <= slot[None, None, :, None]).astype(jnp.int32), axis=-1)
    per = GATHER_ALIGN // LANES
    step = GATHER_TOKENS // GATHER_ALIGN
    lo_a = chunk_of(first) // per
    hi_a = chunk_of(first + GATHER_SLOTS - 1) // per
    g_hi = (hi_a - lo_a) // step + 1
    g_lo = jnp.minimum(lo_a, S // GATHER_ALIGN - step * g_hi)
    edges = cnt[:, :, 0:nch + 1:SCATTER_TOKENS // LANES]
    lo, hi = edges[:, :, :-1], edges[:, :, 1:]
    s_lo = lo // SCATTER_SLOTS
    s_n = jnp.where(hi > lo, (hi - 1) // SCATTER_SLOTS - s_lo + 1, 0)
    to_tiles = lambda t: jnp.swapaxes(t, 1, 2).reshape(-1)
    return g_lo.reshape(-1), g_hi.reshape(-1), to_tiles(s_lo), to_tiles(s_n)


META_ROWS = 16


GATHER_SLOTS = 128
GATHER_TOKENS = 512
GATHER_ALIGN = 256
SCATTER_SLOTS = 256
SCATTER_TOKENS = 512


def _gather_kernel(lo_ref, hi_ref, pos_ref, aff_ref, hf_ref, xs_ref, meta_ref, acc_ref, macc_ref, *, cap):
    R, tk = GATHER_SLOTS, GATHER_TOKENS
    nblk = cap // R
    base = (pl.program_id(0) * pl.num_programs(1) + pl.program_id(1)) * nblk
    mrow = lax.broadcasted_iota(jnp.int32, (META_ROWS, tk), 0)
    lane = lax.broadcasted_iota(jnp.int32, (1, tk), 1)
    for j in range(nblk):
        slot = j * R + lax.broadcasted_iota(jnp.int32, (R, tk), 0)
        acc_ref[...] = jnp.zeros_like(acc_ref)
        macc_ref[...] = jnp.zeros_like(macc_ref)

        first = lo_ref[base + j]

        def body(w, carry):
            t0 = pl.multiple_of(first * GATHER_ALIGN + w * tk, GATHER_ALIGN)
            onehot = jnp.where(slot == pos_ref[0, 0, :, pl.ds(t0, tk)], 1.0, 0.0).astype(BF16)
            acc_ref[...] += _dot(onehot, hf_ref[0, pl.ds(t0, tk), :])
            a = aff_ref[0, 0, :, pl.ds(t0, tk)]
            hi = a.astype(BF16).astype(F32)
            mid = (a - hi).astype(BF16).astype(F32)
            lo = (a - hi - mid).astype(BF16).astype(F32)
            t = t0 + lane
            pieces = (hi, mid, lo, (t // 64).astype(F32), (t % 64).astype(F32))
            meta = jnp.zeros((META_ROWS, tk), F32)
            for r, piece in enumerate(pieces):
                meta = jnp.where(mrow == r, piece, meta)
            macc_ref[...] += _dot_nt(meta.astype(BF16), onehot)
            return carry

        lax.fori_loop(0, hi_ref[base + j], body, 0)
        xs_ref[0, 0, j * R:(j + 1) * R, :] = acc_ref[...].astype(BF16)
        meta_ref[0, 0, :, j * R:(j + 1) * R] = macc_ref[...]


def _gather(g_lo, g_hi, pos, aff_t, hf, cap):
    B, E, S = pos.shape
    D = hf.shape[2]
    row = pl.BlockSpec((1, 1, 1, S), lambda b, e, *_: (b, e, 0, 0))
    return pl.pallas_call(
        functools.partial(_gather_kernel, cap=cap),
        grid_spec=pltpu.PrefetchScalarGridSpec(
            num_scalar_prefetch=2,
            grid=(B, E),
            in_specs=[row, row, pl.BlockSpec((1, S, D), lambda b, e, *_: (b, 0, 0))],
            out_specs=[pl.BlockSpec((1, 1, cap, D), lambda b, e, *_: (b, e, 0, 0)),
                       pl.BlockSpec((1, 1, META_ROWS, cap), lambda b, e, *_: (b, e, 0, 0))],
            scratch_shapes=[pltpu.VMEM((GATHER_SLOTS, D), F32), pltpu.VMEM((META_ROWS, GATHER_SLOTS), F32)]),
        out_shape=[jax.ShapeDtypeStruct((B, E, cap, D), BF16), jax.ShapeDtypeStruct((B, E, META_ROWS, cap), F32)],
        compiler_params=_cparams(("parallel", "arbitrary")),
        name="moe_gather",
    )(g_lo, g_hi, pos.reshape(B, E, 1, S), aff_t.reshape(B, E, 1, S), hf)


def _ffn_kernel(xs_ref, meta_ref, wg_ref, wu_ref, wd_ref, y_ref, *, tf):
    x = xs_ref[0, 0]
    cap = x.shape[0]
    FF = wg_ref.shape[2]
    y = jnp.zeros((cap, wd_ref.shape[2]), F32)
    for f in range(FF // tf):
        hg = _dot(x, wg_ref[0, :, f * tf:(f + 1) * tf])
        hu = _dot(x, wu_ref[0, :, f * tf:(f + 1) * tf])
        hid = (hg * _sigmoid(hg) * hu).astype(BF16)
        y = y + _dot(hid, wd_ref[0, f * tf:(f + 1) * tf, :])
    m = meta_ref[0, 0]
    g_row = m[0:1] + m[1:2] + m[2:3]
    eye = lax.broadcasted_iota(jnp.int32, (cap, cap), 0) == lax.broadcasted_iota(jnp.int32, (cap, cap), 1)
    g_col = jnp.sum(jnp.where(eye, g_row, 0.0), axis=1, keepdims=True)
    y_ref[0, 0] = (y * g_col).astype(BF16)


def _expert_ffn(xs, meta, w_gate, w_up, w_down, tf=512):
    B, E, cap, D = xs.shape
    FF = w_gate.shape[2]
    return pl.pallas_call(
        functools.partial(_ffn_kernel, tf=tf),
        grid=(E, B),
        in_specs=[pl.BlockSpec((1, 1, cap, D), lambda e, b: (b, e, 0, 0)),
                  pl.BlockSpec((1, 1, META_ROWS, cap), lambda e, b: (b, e, 0, 0)),
                  pl.BlockSpec((1, D, FF), lambda e, b: (e, 0, 0)),
                  pl.BlockSpec((1, D, FF), lambda e, b: (e, 0, 0)),
                  pl.BlockSpec((1, FF, D), lambda e, b: (e, 0, 0))],
        out_specs=pl.BlockSpec((1, 1, cap, D), lambda e, b: (b, e, 0, 0)),
        out_shape=jax.ShapeDtypeStruct((B, E, cap, D), BF16),
        compiler_params=_cparams(("parallel", "arbitrary")),
        name="moe_ffn",
    )(xs, meta, w_gate.astype(BF16), w_up.astype(BF16), w_down.astype(BF16))


def _scatter_kernel(lo_ref, n_ref, y_ref, meta_ref, x1_ref, gt_ref, g_ref, o_ref, acc_ref):
    E = y_ref.shape[1]
    tt, R = SCATTER_TOKENS, SCATTER_SLOTS
    tile = pl.program_id(0) * pl.num_programs(1) + pl.program_id(1)
    tok = pl.program_id(1) * tt + lax.broadcasted_iota(jnp.int32, (tt, R), 0)
    acc_ref[...] = jnp.zeros_like(acc_ref)

    def expert_body(e, carry):
        first = lo_ref[tile * E + e]

        def block_body(k, carry):
            r0 = pl.multiple_of((first + k) * R, R)
            m = meta_ref[0, e, :, pl.ds(r0, R)]
            idx = (m[3:4] * 64.0 + m[4:5]).astype(jnp.int32)
            onehot = jnp.where(tok == idx, 1.0, 0.0).astype(BF16)
            acc_ref[...] += _dot(onehot, y_ref[0, e, pl.ds(r0, R), :])
            return carry

        return lax.fori_loop(0, n_ref[tile * E + e], block_body, carry)

    lax.fori_loop(0, E, expert_body, 0)
    o_ref[0] = _rms(x1_ref[0] + gt_ref[0] * acc_ref[...], g_ref[...])


def _scatter_final(s_lo, s_n, y, meta, x1, gt2, g_final):
    B, E, cap, D = y.shape
    S = x1.shape[1]
    tt = SCATTER_TOKENS
    tile = pl.BlockSpec((1, tt, D), lambda b, i, *_: (b, i, 0))
    return pl.pallas_call(
        _scatter_kernel,
        grid_spec=pltpu.PrefetchScalarGridSpec(
            num_scalar_prefetch=2,
            grid=(B, S // tt),
            in_specs=[pl.BlockSpec((1, E, cap, D), lambda b, i, *_: (b, 0, 0, 0)),
                      pl.BlockSpec((1, E, META_ROWS, cap), lambda b, i, *_: (b, 0, 0, 0)),
                      tile, pl.BlockSpec((1, 1, D), lambda b, i, *_: (b, 0, 0)),
                      pl.BlockSpec((1, D), lambda b, i, *_: (0, 0))],
            out_specs=tile,
            scratch_shapes=[pltpu.VMEM((tt, D), F32)]),
        out_shape=jax.ShapeDtypeStruct((B, S, D), F32),
        compiler_params=_cparams(("parallel", "parallel")),
        name="moe_scatter_final",
    )(s_lo, s_n, y, meta, x1, gt2.reshape(B, 1, D), g_final.reshape(1, D))


def kernel(x, c, w_ada, b_ada, g_mix, w_in, b_in, conv_qk_w, conv_qk_b, mlstm_norm_g, conv_hy_w, conv_hy_b,
           hy_w1, hy_b1, hy_w2, hy_b2, hy_w3, hy_freq, hy_deltas, hy_bias, hyena_norm_g, w_out, g_ffn,
           w_router, w_gate, w_up, w_down, g_final):
    S = x.shape[1]
    cap = EC_CAPACITY_FACTOR * S // N_EXPERTS
    assert w_ada.shape[0] == 1, "single-layer block: the final RMSNorm is fused into the MoE scatter"
    l = 0
    mod = _modulation(c, w_ada[l], b_ada[l])
    sh1, sc1, gt1, sh2, sc2, gt2 = jnp.split(mod, 6, axis=-1)
    qkvo, hx, gates_t = _in_projection(x, sc1, sh1, g_mix[l], w_in[l], b_in[l])
    y_m = _mlstm(qkvo, gates_t, conv_qk_w[l], conv_qk_b[l], mlstm_norm_g[l])
    y_h = _hyena(hx, conv_hy_w[l], conv_hy_b[l], hy_w1[l], hy_b1[l], hy_w2[l], hy_b2[l], hy_w3[l],
                 hy_freq[l], hy_deltas[l], hy_bias[l], hyena_norm_g[l])
    x1, hf, aff_t = _out_projection(y_m, y_h, x, gt1, sc2, sh2, g_ffn[l], w_out[l], w_router[l])
    pos, cnt = _route(aff_t, cap)
    g_lo, g_hi, s_lo, s_n = _window_tables(cnt, S, cap)
    xs, meta = _gather(g_lo, g_hi, pos, aff_t, hf, cap)
    y = _expert_ffn(xs, meta, w_gate[l], w_up[l], w_down[l])
    return _scatter_final(s_lo, s_n, y, meta, x1, gt2, g_final)
```

```python
import functools
import math

import numpy as np
import jax
import jax.numpy as jnp
from jax import lax
from jax.experimental import pallas as pl
from jax.experimental.pallas import tpu as pltpu

F32 = jnp.float32
BF16 = jnp.bfloat16

MLSTM_HEADS = 4
HEAD_DIM = 128
MLSTM_CHUNK = 128
MLSTM_M_INIT = -1e30
HYENA_GROUP_DIM = 128
HYENA_ORDER = 2
HYENA_BANDS = 16
HYENA_WINDOW_SHIFT = 0.05
N_GATE_COLS = 4 * MLSTM_HEADS
N_EXPERTS = 16
EC_CAPACITY_FACTOR = 2
RMS_EPS = 1e-6

LANES = 128
SUBLANES = 8
FFT_N2 = 64
FFT_PITCH = 72
NEG_BIG = -1e30
VMEM_LIMIT = 56 * 1024 * 1024


def _cparams(sem, vmem=None, flags=None):
    return pltpu.CompilerParams(dimension_semantics=sem, vmem_limit_bytes=vmem or VMEM_LIMIT, flags=flags)


def _split(a):
    hi = a.astype(BF16)
    lo = (a - hi.astype(F32)).astype(BF16)
    return hi, lo


def _dot(a, b):
    return jnp.dot(a, b, preferred_element_type=F32)


def _dot_nt(a, b):
    return lax.dot_general(a, b, (((1,), (1,)), ((), ())), preferred_element_type=F32)


def _dot3(a, b):
    ah, al = _split(a)
    bh, bl = _split(b)
    return _dot(ah, bh) + _dot(ah, bl) + _dot(al, bh)


def _dot3_nt(a, b):
    ah, al = _split(a)
    bh, bl = _split(b)
    return _dot_nt(ah, bh) + _dot_nt(ah, bl) + _dot_nt(al, bh)


def _dotc(w_hi, w_lo, x, passes):
    xh, xl = _split(x)
    out = _dot(w_hi, xh)
    if passes == 3:
        out = out + _dot(w_hi, xl) + _dot(w_lo, xh)
    return out


def _rms(x, g):
    return x * lax.rsqrt(jnp.mean(x * x, axis=-1, keepdims=True) + RMS_EPS) * g


def _sigmoid(x):
    return 1.0 / (1.0 + jnp.exp(-x))


def _log_sigmoid(x):
    return jnp.minimum(x, 0.0) - jnp.log(1.0 + jnp.exp(-jnp.abs(x)))


def _mod_kernel(c_ref, w_ref, b_ref, o_ref):
    o_ref[...] = _dot3(c_ref[...], w_ref[...]) + b_ref[...]


def _modulation(c, w_ada, b_ada):
    B, D = c.shape
    n = w_ada.shape[1]
    tn = 768
    return pl.pallas_call(
        _mod_kernel,
        grid=(n // tn,),
        in_specs=[pl.BlockSpec((B, D), lambda j: (0, 0)),
                  pl.BlockSpec((D, tn), lambda j: (0, j)),
                  pl.BlockSpec((1, tn), lambda j: (0, j))],
        out_specs=pl.BlockSpec((B, tn), lambda j: (0, j)),
        out_shape=jax.ShapeDtypeStruct((B, n), F32),
        compiler_params=_cparams(("parallel",)),
        name="adaln_mod",
    )(c, w_ada, b_ada.reshape(1, n))


def _inproj_kernel(x_ref, sc_ref, sh_ref, g_ref, wq_ref, bq_ref, wh_ref, bh_ref, wgt_ref, bgt_ref,
                   qkvo_ref, hx_ref, gt_ref):
    x = x_ref[0]
    h = _rms(x, g_ref[...]) * (1.0 + sc_ref[0]) + sh_ref[0]
    hb = h.astype(BF16)
    qkvo_ref[0] = _dot(hb, wq_ref[...]) + bq_ref[...]
    hx_ref[0] = _dot(hb, wh_ref[...]) + bh_ref[...]
    gt_ref[0] = _dot3_nt(wgt_ref[...], h) + bgt_ref[...]


def _in_projection(x, sc1, sh1, g_mix, w_in, b_in, ts=512):
    B, S, D = x.shape
    nq = 4 * MLSTM_HEADS * HEAD_DIM
    nh = w_in.shape[1] - nq - N_GATE_COLS
    wq = w_in[:, :nq].astype(BF16)
    wh = w_in[:, nq + N_GATE_COLS:].astype(BF16)
    bq = b_in[:nq].reshape(1, nq)
    bh = b_in[nq + N_GATE_COLS:].reshape(1, nh)
    ng = MLSTM_HEADS * SUBLANES
    pad = lambda t: jnp.pad(t.reshape(4, MLSTM_HEADS, -1).transpose(1, 0, 2), ((0, 0), (0, 4), (0, 0))).reshape(ng, -1)
    wgt = pad(w_in[:, nq:nq + N_GATE_COLS].T)
    bgt = pad(b_in[nq:nq + N_GATE_COLS].reshape(N_GATE_COLS, 1))
    const = lambda b, i: (0, 0)
    return pl.pallas_call(
        _inproj_kernel,
        grid=(B, S // ts),
        in_specs=[pl.BlockSpec((1, ts, D), lambda b, i: (b, i, 0)),
                  pl.BlockSpec((1, 1, D), lambda b, i: (b, 0, 0)),
                  pl.BlockSpec((1, 1, D), lambda b, i: (b, 0, 0)),
                  pl.BlockSpec((1, D), const),
                  pl.BlockSpec((D, nq), const), pl.BlockSpec((1, nq), const),
                  pl.BlockSpec((D, nh), const), pl.BlockSpec((1, nh), const),
                  pl.BlockSpec((ng, D), const), pl.BlockSpec((ng, 1), const)],
        out_specs=[pl.BlockSpec((1, ts, nq), lambda b, i: (b, i, 0)),
                   pl.BlockSpec((1, ts, nh), lambda b, i: (b, i, 0)),
                   pl.BlockSpec((1, ng, ts), lambda b, i: (b, 0, i))],
        out_shape=[jax.ShapeDtypeStruct((B, S, nq), F32),
                   jax.ShapeDtypeStruct((B, S, nh), F32),
                   jax.ShapeDtypeStruct((B, ng, S), F32)],
        compiler_params=_cparams(("parallel", "parallel")),
        name="in_proj",
    )(x, sc1.reshape(B, 1, D), sh1.reshape(B, 1, D), g_mix.reshape(1, D), wq, bq, wh, bh, wgt, bgt)


def _short_conv_chunk(src_ref, lead, c, nc, w_ref, b_ref, rows):
    L = rows.shape[0]
    S = nc * L
    t0 = pl.multiple_of(c * L, L)
    cur = src_ref[lead, pl.ds(t0, L), :]
    p0 = pl.multiple_of(jnp.maximum(t0 - SUBLANES, 0), SUBLANES)
    n0 = pl.multiple_of(jnp.minimum(t0 + L, S - SUBLANES), SUBLANES)
    prev_row = src_ref[lead, pl.ds(p0, SUBLANES), :][SUBLANES - 1:SUBLANES, :]
    next_row = src_ref[lead, pl.ds(n0, SUBLANES), :][0:1, :]
    prev_row = jnp.where(c > 0, prev_row, 0.0)
    next_row = jnp.where(c < nc - 1, next_row, 0.0)
    up = jnp.where(rows == 0, prev_row, pltpu.roll(cur, 1, 0))
    dn = jnp.where(rows == L - 1, next_row, pltpu.roll(cur, L - 1, 0))
    return b_ref[...] + up * w_ref[0:1, :] + cur * w_ref[1:2, :] + dn * w_ref[2:3, :]


def _mlstm_kernel(q_ref, k_ref, v_ref, o_ref, gt_ref, wq_ref, wk_ref, bq_ref, bk_ref, ng_ref,
                  y_ref, qt_ref, kb_ref, vt_ref, c_ref, cl_ref, ld_ref, st_ref, pc_ref, cs_ref, kq_ref, *, nc):
    L = MLSTM_CHUNK
    rows = lax.broadcasted_iota(jnp.int32, (L, LANES), 0)
    cols = lax.broadcasted_iota(jnp.int32, (L, LANES), 1)
    pad = 2 * SUBLANES
    piece_row = lax.broadcasted_iota(jnp.int32, (SUBLANES, LANES), 0)
    k_scale = HEAD_DIM ** -0.5

    def conv_body(c, carry):
        t0 = pl.multiple_of(c * L, L)
        yq = _short_conv_chunk(q_ref, 0, c, nc, wq_ref, bq_ref, rows)
        qt_ref[c] = (yq * _sigmoid(yq)).T.astype(BF16)
        yk = _short_conv_chunk(k_ref, 0, c, nc, wk_ref, bk_ref, rows)
        kb_ref[pl.ds(t0, L), :] = (yk * _sigmoid(yk) * k_scale).astype(BF16)
        vt_ref[c] = v_ref[0, pl.ds(t0, L), :].T
        gates = gt_ref[0, 0, :, pl.ds(t0, L)]
        lf = _log_sigmoid(jnp.where(piece_row < 3, gates[1:2], gates[3:4]))
        hi = lf.astype(BF16).astype(F32)
        mid = (lf - hi).astype(BF16).astype(F32)
        kind = piece_row % 3
        pc_ref[pl.ds(pl.multiple_of(c * SUBLANES, SUBLANES), SUBLANES), :] = jnp.where(
            piece_row >= 6, 0.0, jnp.where(kind == 0, hi, jnp.where(kind == 1, mid, lf - hi - mid)))
        return carry

    lax.fori_loop(0, nc, conv_body, 0, unroll=2)

    pieces = pc_ref[...].astype(BF16)
    fwd_rows = lax.broadcasted_iota(jnp.int32, pc_ref.shape, 0) % SUBLANES < 3
    cs_ref[...] = jnp.where(fwd_rows, _dot(pieces, (rows <= cols).astype(BF16)),
                            _dot(pieces, (rows >= cols).astype(BF16)))

    ROW_B, ROW_MAX, ROW_G, ROW_A, ROW_N, ROW_M = range(6)

    def stat_tile(rows_by_index, base=None):
        tile = jnp.zeros((SUBLANES, LANES), F32) if base is None else base
        for r, value in rows_by_index.items():
            tile = jnp.where(piece_row == r, value, tile)
        return tile

    def local_pass(c, rev):
        t0 = pl.multiple_of(c * L, L)
        slot = 2 * c + (1 if rev else 0)
        kb = kb_ref[pl.ds(t0, L), :]
        gates = gt_ref[0, 0, :, pl.ds(t0, L)]
        i_row = gates[(2 if rev else 0):(3 if rev else 1)]
        cs = cs_ref[pl.ds(pl.multiple_of(c * SUBLANES, SUBLANES), SUBLANES), :]
        o = 3 if rev else 0
        b_row = cs[o:o + 1] + cs[o + 1:o + 2] + cs[o + 2:o + 3]
        keep = (rows >= cols) if rev else (rows <= cols)
        g = b_row[:, 0:1] if rev else b_row[:, L - 1:L]
        w_tile = jnp.broadcast_to(i_row - b_row, (L, LANES)).T
        log_d = jnp.where(keep, w_tile + b_row, NEG_BIG)
        ld_ref[slot] = log_d
        w_end = g - b_row + i_row
        a = jnp.max(w_end, axis=1, keepdims=True)
        e_end = jnp.exp(w_end - a)
        ve = jnp.concatenate([vt_ref[c] * e_end, jnp.broadcast_to(e_end, (pad, L))], axis=0).astype(BF16)
        cn = _dot(ve, kb)
        cl_ref[slot] = cn[:HEAD_DIM]
        st_ref[slot] = stat_tile({ROW_B: b_row, ROW_MAX: jnp.max(log_d, axis=0, keepdims=True), ROW_G: g,
                                  ROW_A: a, ROW_N: cn[HEAD_DIM:HEAD_DIM + 1]})

    def local_body(c, carry):
        kq_ref[c] = _dot(kb_ref[pl.ds(pl.multiple_of(c * L, L), L), :], qt_ref[c])
        local_pass(c, False)
        local_pass(c, True)
        return carry

    lax.fori_loop(0, nc, local_body, 0, unroll=8)

    def scan_step(c, rev, n, m):
        slot = 2 * c + (1 if rev else 0)
        side = 1 if rev else 0
        stats = st_ref[slot]
        g = stats[ROW_G:ROW_G + 1]
        a = stats[ROW_A:ROW_A + 1]
        m_new = jnp.maximum(g + m, a)
        s_prev = jnp.exp(g + m - m_new)
        s_loc = jnp.exp(a - m_new)
        c_in = c_ref[side]
        c_ref[side] = s_prev * c_in + s_loc * cl_ref[slot]
        cl_ref[slot] = c_in
        n_new = s_prev * n + s_loc * stats[ROW_N:ROW_N + 1]
        st_ref[slot] = stat_tile({ROW_N: n, ROW_M: m}, base=stats)
        return n_new, m_new

    c_ref[...] = jnp.zeros_like(c_ref)
    n0 = jnp.zeros((1, HEAD_DIM), F32)
    m0 = jnp.full((1, LANES), MLSTM_M_INIT, F32)

    def scan_body(j, carry):
        n_f, m_f, n_b, m_b = carry
        n_f, m_f = scan_step(j, False, n_f, m_f)
        n_b, m_b = scan_step(nc - 1 - j, True, n_b, m_b)
        return n_f, m_f, n_b, m_b

    lax.fori_loop(0, nc, scan_body, (n0, m0, n0, m0))

    def output_pass(c, rev):
        t0 = pl.multiple_of(c * L, L)
        slot = 2 * c + (1 if rev else 0)
        qt = qt_ref[c]
        stats = st_ref[slot]
        log_inter = stats[ROW_B:ROW_B + 1] + stats[ROW_M:ROW_M + 1]
        m_t = jnp.maximum(log_inter, stats[ROW_MAX:ROW_MAX + 1])
        d = jnp.exp(ld_ref[slot] - m_t)
        e_inter = jnp.exp(log_inter - m_t)
        s = (kq_ref[c] * d).astype(BF16)
        vn = _dot(jnp.concatenate([vt_ref[c].astype(BF16), jnp.ones((pad, L), BF16)], axis=0), s)
        n_in = stats[ROW_N:ROW_N + 1]
        cq = _dot(jnp.concatenate([cl_ref[slot], jnp.broadcast_to(n_in, (pad, LANES))], axis=0).astype(BF16), qt)
        den = vn[HEAD_DIM:HEAD_DIM + 1] + e_inter * cq[HEAD_DIM:HEAD_DIM + 1]
        scale = 1.0 / jnp.maximum(jnp.abs(den), jnp.exp(-m_t))
        return ((vn[:HEAD_DIM] + e_inter * cq[:HEAD_DIM]) * scale).T

    def output_body(c, carry):
        t0 = pl.multiple_of(c * L, L)
        hs = output_pass(c, False) + output_pass(c, True)
        y_ref[0, pl.ds(t0, L), :] = _sigmoid(o_ref[0, pl.ds(t0, L), :]) * _rms(hs, ng_ref[...])
        return carry

    lax.fori_loop(0, nc, output_body, 0, unroll=8)


def _mlstm(qkvo, gates_t, conv_w, conv_b, norm_g):
    B, S, _ = qkvo.shape
    H, Dh = MLSTM_HEADS, HEAD_DIM
    nc = S // MLSTM_CHUNK
    seq = lambda off: pl.BlockSpec((1, S, Dh), lambda b, h: (b, 0, off + h))
    vec = lambda r, off: pl.BlockSpec((r, Dh), lambda b, h: (0, off + h))
    return pl.pallas_call(
        functools.partial(_mlstm_kernel, nc=nc),
        grid=(B, H),
        in_specs=[seq(0), seq(H), seq(2 * H), seq(3 * H),
                  pl.BlockSpec((1, 1, SUBLANES, S), lambda b, h: (b, h, 0, 0)),
                  vec(3, 0), vec(3, H), vec(1, 0), vec(1, H), vec(1, 0)],
        out_specs=pl.BlockSpec((1, S, Dh), lambda b, h: (b, 0, h)),
        out_shape=jax.ShapeDtypeStruct((B, S, H * Dh), F32),
        scratch_shapes=[pltpu.VMEM((nc, Dh, MLSTM_CHUNK), BF16), pltpu.VMEM((S, Dh), BF16),
                        pltpu.VMEM((nc, Dh, MLSTM_CHUNK), F32), pltpu.VMEM((2, Dh, Dh), F32),
                        pltpu.VMEM((2 * nc, Dh, Dh), F32), pltpu.VMEM((2 * nc, MLSTM_CHUNK, MLSTM_CHUNK), F32),
                        pltpu.VMEM((2 * nc, SUBLANES, LANES), F32), pltpu.VMEM((nc * SUBLANES, LANES), F32),
                        pltpu.VMEM((nc * SUBLANES, LANES), F32), pltpu.VMEM((nc, MLSTM_CHUNK, MLSTM_CHUNK), F32)],
        compiler_params=_cparams(("parallel", "arbitrary")),
        name="mlstm",
    )(qkvo, qkvo, qkvo, qkvo, gates_t.reshape(B, H, SUBLANES, S), conv_w, conv_w, conv_b.reshape(1, -1), conv_b.reshape(1, -1),
      norm_g.reshape(1, -1))


FILTER_PASSES = 3
CONV_PASSES = 1
FFT_UNROLL = 8


def _hilo(m):
    m32 = jnp.asarray(m, F32)
    hi = m32.astype(BF16)
    return hi, (m32 - hi.astype(F32)).astype(BF16)


def _stack_complex(m):
    return np.block([[m.real, -m.imag], [m.imag, m.real]])


def _dft_constants(S):
    N = 2 * S
    N2 = FFT_N2
    N1 = N // N2
    k1 = np.arange(N1)
    n2 = np.arange(N2)
    f1 = np.exp(-2j * np.pi * np.outer(k1, np.arange(N1)) / N1)
    f2 = np.exp(-2j * np.pi * np.outer(n2, n2) / N2)
    tw = np.exp(-2j * np.pi * np.outer(k1, n2) / N)
    g1 = np.conj(f1).T[:N1 // 2] / N
    consts = dict(
        f1_real=_hilo(np.concatenate([f1.real, f1.imag], axis=0)),
        f1_half=_hilo(_stack_complex(f1[:, :N1 // 2])),
        f2=_hilo(_stack_complex(f2)),
        f2_inv=_hilo(_stack_complex(np.conj(f2))),
        g1=_hilo(_stack_complex(g1)),
        tw_re=jnp.asarray(np.broadcast_to(tw.real[:, :, None], (N1, N2, LANES)), F32),
        tw_im=jnp.asarray(np.broadcast_to(tw.imag[:, :, None], (N1, N2, LANES)), F32),
    )
    return N1, N2, consts


def _hy_hidden_kernel(w1t_ref, w1c_ref, w1s_ref, b1_ref, w2_ref, b2_ref, fr_ref, o_ref, *, S, T):
    j = pl.program_id(0) * T + lax.broadcasted_iota(jnp.int32, (T, 1), 0)
    p = jnp.where(j < S, j, 2 * S - j).astype(F32)
    t = p / (S - 1)
    w = (2.0 * math.pi) * p / S
    band = lax.broadcasted_iota(jnp.int32, (1, HYENA_BANDS), 1).astype(F32)
    bands = 1e-4 + band * ((HYENA_BANDS - 1 - 1e-4) / (HYENA_BANDS - 1))
    arg = bands * w
    pre = t * w1t_ref[...] + _dot3(jnp.cos(arg), w1c_ref[...]) + _dot3(-jnp.sin(arg), w1s_ref[...]) + b1_ref[...]
    hid = jnp.sin(fr_ref[...] * pre)
    o_ref[...] = jnp.sin(fr_ref[...] * (_dot3(hid, w2_ref[...]) + b2_ref[...]))


def _hy_hidden(S, w1, b1, w2, b2, freq):
    N = 2 * S
    T = 1024
    Hd = w2.shape[0]
    full = lambda a: pl.BlockSpec(a.shape, lambda i: (0,) * a.ndim)
    args = (w1[0:1], w1[1:1 + HYENA_BANDS], w1[1 + HYENA_BANDS:], b1.reshape(1, Hd), w2, b2.reshape(1, Hd),
            freq.reshape(1, Hd))
    return pl.pallas_call(
        functools.partial(_hy_hidden_kernel, S=S, T=T),
        grid=(N // T,),
        in_specs=[full(a) for a in args],
        out_specs=pl.BlockSpec((T, Hd), lambda i: (i, 0)),
        out_shape=jax.ShapeDtypeStruct((N, Hd), F32),
        compiler_params=_cparams(("parallel",)),
        name="hyena_hidden",
    )(*args)


def _fft_stage2(a_re, a_im, k1, twr_ref, twi_ref, f2_hi, f2_lo, passes):
    r0 = pl.multiple_of(k1 * FFT_PITCH, SUBLANES)
    ar = a_re[pl.ds(r0, FFT_N2), :]
    ai = a_im[pl.ds(r0, FFT_N2), :]
    twr = twr_ref[k1]
    twi = twi_ref[k1]
    t = jnp.concatenate([ar * twr - ai * twi, ar * twi + ai * twr], axis=0)
    return _dotc(f2_hi, f2_lo, t, passes)


def _hy_filter_kernel(hid_ref, w3f_ref, w3b_ref, df_ref, db_ref, bias_ref, f1h_ref, f1l_ref, f2h_ref, f2l_ref,
                      twr_ref, twi_ref, kr_ref, ki_ref, u_ref, a_re, a_im, *, S, N1):
    N2, P = FFT_N2, FFT_PITCH
    T = 512
    nb = T // N2

    def gen_body(i, carry):
        r0 = pl.multiple_of(i * T, T)
        hid = hid_ref[pl.ds(r0, T), :]
        j = r0 + lax.broadcasted_iota(jnp.int32, (T, 1), 0)
        t = jnp.where(j < S, j, 2 * S - j).astype(F32) / (S - 1)
        kf = _dot3(hid, w3f_ref[...]) * (jnp.exp(-t * jnp.abs(df_ref[...])) + HYENA_WINDOW_SHIFT)
        kb = _dot3(hid, w3b_ref[...]) * (jnp.exp(-t * jnp.abs(db_ref[...])) + HYENA_WINDOW_SHIFT)
        ker = (jnp.where(j < S, kf, 0.0) + jnp.where((j > S) | (j == 0), kb, 0.0)
               + jnp.where(j == 0, bias_ref[0], 0.0))
        for b in range(nb):
            u_ref[pl.ds(pl.multiple_of((i * nb + b) * P, SUBLANES), N2), :] = ker[b * N2:(b + 1) * N2]
        return carry

    lax.fori_loop(0, 2 * S // T, gen_body, 0)

    def stage1(n2, carry):
        x = u_ref[pl.ds(n2, N1, stride=P), :]
        a = _dotc(f1h_ref[...], f1l_ref[...], x, FILTER_PASSES)
        a_re[pl.ds(n2, N1, stride=P), :] = a[:N1]
        a_im[pl.ds(n2, N1, stride=P), :] = a[N1:]
        return carry

    lax.fori_loop(0, N2, stage1, 0, unroll=FFT_UNROLL)

    def stage2(k1, carry):
        x = _fft_stage2(a_re, a_im, k1, twr_ref, twi_ref, f2h_ref[...], f2l_ref[...], FILTER_PASSES)
        r0 = pl.multiple_of(k1 * N2, N2)
        kr_ref[0, pl.ds(r0, N2), :] = x[:N2]
        ki_ref[0, pl.ds(r0, N2), :] = x[N2:]
        return carry

    lax.fori_loop(0, N1, stage2, 0, unroll=2 * FFT_UNROLL)


def _hy_filter_spectrum(S, hid, w3, deltas, bias, N1, consts):
    N = 2 * S
    N2, P = FFT_N2, FFT_PITCH
    Hd = hid.shape[1]
    nblk = w3.shape[1] // (2 * HYENA_ORDER * LANES)
    f1h, f1l = consts["f1_real"]
    f2h, f2l = consts["f2"]
    full = lambda a: pl.BlockSpec(a.shape, lambda o, j: (0,) * a.ndim, pipeline_mode=pl.Buffered(1))
    colf = lambda o, j: (0, (2 * o) * nblk + j)
    colb = lambda o, j: (0, (2 * o + 1) * nblk + j)
    out = jax.ShapeDtypeStruct((HYENA_ORDER, N, nblk * LANES), F32)
    return pl.pallas_call(
        functools.partial(_hy_filter_kernel, S=S, N1=N1),
        grid=(HYENA_ORDER, nblk),
        in_specs=[full(hid),
                  pl.BlockSpec((Hd, LANES), colf), pl.BlockSpec((Hd, LANES), colb),
                  pl.BlockSpec((1, LANES), colf), pl.BlockSpec((1, LANES), colb),
                  pl.BlockSpec((1, 1, LANES), lambda o, j: (o, 0, j)),
                  full(f1h), full(f1l), full(f2h), full(f2l), full(consts["tw_re"]), full(consts["tw_im"])],
        out_specs=[pl.BlockSpec((1, N, LANES), lambda o, j: (o, 0, j))] * 2,
        out_shape=[out, out],
        scratch_shapes=[pltpu.VMEM((N1 * P, LANES), F32)] * 3,
        compiler_params=_cparams(("parallel", "parallel")),
        name="hyena_filter_spectrum",
    )(hid, w3, w3, deltas.reshape(1, -1), deltas.reshape(1, -1), bias.reshape(HYENA_ORDER, 1, -1),
      f1h, f1l, f2h, f2l, consts["tw_re"], consts["tw_im"])


def _hy_conv_kernel(u_ref, x_ref, kr_ref, ki_ref, wu_ref, bu_ref, wx_ref, bx_ref, ng_ref,
                    f1h_ref, f1l_ref, f2h_ref, f2l_ref, f2ih_ref, f2il_ref, g1h_ref, g1l_ref, twr_ref, twi_ref,
                    o_ref, u_re, u_im, a_re, a_im, *, N1, conv_u, last):
    N2, P = FFT_N2, FFT_PITCH
    nh = N1 // 2
    rows = lax.broadcasted_iota(jnp.int32, (N2, LANES), 0)

    def load_body(c, carry):
        r0 = pl.multiple_of(c * P, SUBLANES)
        for b, dst in ((0, u_re), (1, u_im)):
            if conv_u:
                dst[pl.ds(r0, N2), :] = _short_conv_chunk(u_ref, b, c, nh, wu_ref, bu_ref, rows)
            else:
                dst[pl.ds(r0, N2), :] = u_ref[b, pl.ds(pl.multiple_of(c * N2, N2), N2), :]
        return carry

    lax.fori_loop(0, nh, load_body, 0)

    def stage1(n2, carry):
        x = jnp.concatenate([u_re[pl.ds(n2, nh, stride=P), :], u_im[pl.ds(n2, nh, stride=P), :]], axis=0)
        a = _dotc(f1h_ref[...], f1l_ref[...], x, CONV_PASSES)
        a_re[pl.ds(n2, N1, stride=P), :] = a[:N1]
        a_im[pl.ds(n2, N1, stride=P), :] = a[N1:]
        return carry

    lax.fori_loop(0, N2, stage1, 0, unroll=FFT_UNROLL)

    def stage2(k1, carry):
        x = _fft_stage2(a_re, a_im, k1, twr_ref, twi_ref, f2h_ref[...], f2l_ref[...], CONV_PASSES)
        s0 = pl.multiple_of(k1 * N2, N2)
        kr = kr_ref[0, pl.ds(s0, N2), :]
        ki = ki_ref[0, pl.ds(s0, N2), :]
        xr, xi = x[:N2], x[N2:]
        y = jnp.concatenate([xr * kr - xi * ki, xr * ki + xi * kr], axis=0)
        b = _dotc(f2ih_ref[...], f2il_ref[...], y, CONV_PASSES)
        br, bi = b[:N2], b[N2:]
        twr = twr_ref[k1]
        twi = twi_ref[k1]
        r0 = pl.multiple_of(k1 * P, SUBLANES)
        a_re[pl.ds(r0, N2), :] = br * twr + bi * twi
        a_im[pl.ds(r0, N2), :] = bi * twr - br * twi
        return carry

    lax.fori_loop(0, N1, stage2, 0, unroll=2 * FFT_UNROLL)

    def inv2(n2, carry):
        b = jnp.concatenate([a_re[pl.ds(n2, N1, stride=P), :], a_im[pl.ds(n2, N1, stride=P), :]], axis=0)
        y = _dotc(g1h_ref[...], g1l_ref[...], b, CONV_PASSES)
        u_re[pl.ds(n2, nh, stride=P), :] = y[:nh]
        u_im[pl.ds(n2, nh, stride=P), :] = y[nh:]
        return carry

    lax.fori_loop(0, N2, inv2, 0, unroll=FFT_UNROLL)

    def out_body(c, carry):
        r0 = pl.multiple_of(c * P, SUBLANES)
        t0 = pl.multiple_of(c * N2, N2)
        for b, src in ((0, u_re), (1, u_im)):
            z = _short_conv_chunk(x_ref, b, c, nh, wx_ref, bx_ref, rows) * src[pl.ds(r0, N2), :]
            o_ref[b, pl.ds(t0, N2), :] = _rms(z, ng_ref[...]) if last else z
        return carry

    lax.fori_loop(0, nh, out_body, 0)


def _hy_conv(u, u_blk, x, x_blk, kr, ki, order, conv_w, conv_b, norm_g, N1, consts, conv_u, last):
    B, S, _ = u.shape
    N2, P = FFT_N2, FFT_PITCH
    N = 2 * S
    nblk = kr.shape[2] // LANES
    pair = 2
    single = pl.Buffered(1)
    full = lambda a: pl.BlockSpec(a.shape, lambda j, p: (0,) * a.ndim, pipeline_mode=single)
    seq = lambda off: pl.BlockSpec((pair, S, LANES), lambda j, p: (p, 0, off + j))
    vec = lambda r, off: pl.BlockSpec((r, LANES), lambda j, p: (0, off + j))
    spec = pl.BlockSpec((1, N, LANES), lambda j, p: (order, 0, j), pipeline_mode=single)
    mats = [*consts["f1_half"], *consts["f2"], *consts["f2_inv"], *consts["g1"], consts["tw_re"], consts["tw_im"]]
    return pl.pallas_call(
        functools.partial(_hy_conv_kernel, N1=N1, conv_u=conv_u, last=last),
        grid=(nblk, B // pair),
        in_specs=[seq(u_blk), seq(x_blk), spec, spec,
                  vec(3, u_blk if conv_u else 0), vec(1, u_blk if conv_u else 0), vec(3, x_blk), vec(1, x_blk),
                  pl.BlockSpec((1, LANES), lambda j, p: (0, j))] + [full(m) for m in mats],
        out_specs=pl.BlockSpec((pair, S, LANES), lambda j, p: (p, 0, j)),
        out_shape=jax.ShapeDtypeStruct((B, S, nblk * LANES), F32),
        scratch_shapes=[pltpu.VMEM((N1 // 2 * P, LANES), F32)] * 2 + [pltpu.VMEM((N1 * P, LANES), F32)] * 2,
        compiler_params=_cparams(("parallel", "parallel")),
        name=f"hyena_conv{order}",
    )(u, x, kr, ki, conv_w, conv_b.reshape(1, -1), conv_w, conv_b.reshape(1, -1), norm_g.reshape(1, -1), *mats)


def _hyena(hx, conv_w, conv_b, w1, b1, w2, b2, w3, freq, deltas, bias, norm_g):
    B, S, C3 = hx.shape
    nblk = C3 // (HYENA_ORDER + 1) // LANES
    N1, N2, consts = _dft_constants(S)
    hid = _hy_hidden(S, w1, b1, w2, b2, freq)
    kr, ki = _hy_filter_spectrum(S, hid, w3, deltas, bias, N1, consts)
    z1 = _hy_conv(hx, 2 * nblk, hx, 0, kr, ki, 0, conv_w, conv_b, norm_g, N1, consts, True, False)
    return _hy_conv(z1, 0, hx, nblk, kr, ki, 1, conv_w, conv_b, norm_g, N1, consts, False, True)


def _outproj_kernel(ym_ref, yh_ref, x_ref, gt_ref, sc_ref, sh_ref, g_ref, wm_ref, wh_ref, wr_ref,
                    x1_ref, hf_ref, aff_ref):
    mixed = _dot(ym_ref[0].astype(BF16), wm_ref[...]) + _dot(yh_ref[0].astype(BF16), wh_ref[...])
    x1 = x_ref[0] + gt_ref[0] * mixed
    x1_ref[0] = x1
    hf = _rms(x1, g_ref[...]) * (1.0 + sc_ref[0]) + sh_ref[0]
    hf_ref[0] = hf.astype(BF16)
    logits = _dot3_nt(wr_ref[...], hf)
    e = jnp.exp(logits - jnp.max(logits, axis=0, keepdims=True))
    aff_ref[0] = e / jnp.sum(e, axis=0, keepdims=True)


def _out_projection(y_m, y_h, x, gt1, sc2, sh2, g_ffn, w_out, w_router, ts=512):
    B, S, D = x.shape
    Wm = y_m.shape[2]
    E = w_router.shape[1]
    wm = w_out[:Wm].astype(BF16)
    wh = w_out[Wm:].astype(BF16)
    tile = lambda w: pl.BlockSpec((1, ts, w), lambda b, i: (b, i, 0))
    row = pl.BlockSpec((1, 1, D), lambda b, i: (b, 0, 0))
    full = lambda a: pl.BlockSpec(a.shape, lambda b, i: (0,) * a.ndim)
    wr = w_router.T
    g = g_ffn.reshape(1, D)
    return pl.pallas_call(
        _outproj_kernel,
        grid=(B, S // ts),
        in_specs=[tile(Wm), tile(y_h.shape[2]), tile(D), row, row, row, full(g), full(wm), full(wh), full(wr)],
        out_specs=[tile(D), tile(D), pl.BlockSpec((1, E, ts), lambda b, i: (b, 0, i))],
        out_shape=[jax.ShapeDtypeStruct((B, S, D), F32), jax.ShapeDtypeStruct((B, S, D), BF16),
                   jax.ShapeDtypeStruct((B, E, S), F32)],
        compiler_params=_cparams(("parallel", "parallel")),
        name="out_proj_router",
    )(y_m, y_h, x, gt1.reshape(B, 1, D), sc2.reshape(B, 1, D), sh2.reshape(B, 1, D), g, wm, wh, wr)


def _route_kernel(aff_ref, pos_ref, cnt_ref, *, S, cap):
    aff = aff_ref[0]
    E = aff.shape[0]
    tpos = lax.broadcasted_iota(jnp.int32, (E, S), 1)
    count = lambda mask: jnp.sum(jnp.where(mask, 1.0, 0.0), axis=1, keepdims=True)
    as_float = lambda word: lax.bitcast_convert_type(word, F32)

    def value_step(i, prefix):
        cand = prefix | jnp.left_shift(1, 30 - i)
        return jnp.where(count(aff >= as_float(cand)) >= cap, cand, prefix)

    thresh = as_float(lax.fori_loop(0, 31, value_step, jnp.zeros((E, 1), jnp.int32)))
    gt = aff > thresh
    eq = aff == thresh
    need = cap - count(gt)
    nbits = S.bit_length() - 1

    def index_step(i, x):
        cand = x | jnp.left_shift(1, nbits - 1 - i)
        return jnp.where(count(eq & (tpos < cand)) < need, cand, x)

    last_tie = lax.fori_loop(0, nbits, index_step, jnp.zeros((E, 1), jnp.int32))
    sel = jnp.where(gt | (eq & (tpos <= last_tie)), 1.0, 0.0)
    before = (lax.broadcasted_iota(jnp.int32, (LANES, LANES), 0)
              < lax.broadcasted_iota(jnp.int32, (LANES, LANES), 1)).astype(BF16)
    carry = jnp.zeros((E, 1), F32)
    lane = lax.broadcasted_iota(jnp.int32, (E, LANES), 1)
    starts = jnp.zeros((E, LANES), F32)
    for c in range(S // LANES):
        sc = sel[:, c * LANES:(c + 1) * LANES]
        rank = _dot(sc.astype(BF16), before) + carry
        pos_ref[0, :, c * LANES:(c + 1) * LANES] = jnp.where(sc > 0.0, rank, -1.0).astype(jnp.int32)
        starts = jnp.where(lane == c, carry, starts)
        carry = carry + jnp.sum(sc, axis=1, keepdims=True)
    cnt_ref[0] = jnp.where(lane >= S // LANES, carry, starts).astype(jnp.int32)


def _route(aff_t, cap):
    B, E, S = aff_t.shape
    assert S // LANES < LANES
    blk = pl.BlockSpec((1, E, S), lambda b: (b, 0, 0))
    return pl.pallas_call(
        functools.partial(_route_kernel, S=S, cap=cap),
        grid=(B,),
        in_specs=[blk],
        out_specs=[blk, pl.BlockSpec((1, E, LANES), lambda b: (b, 0, 0))],
        out_shape=[jax.ShapeDtypeStruct((B, E, S), jnp.int32), jax.ShapeDtypeStruct((B, E, LANES), jnp.int32)],
        compiler_params=_cparams(("parallel",)),
        name="route_topk",
    )(aff_t)


def _window_tables(cnt, S, cap):
    B, E, _ = cnt.shape
    nch = S // LANES
    bounds = cnt[:, :, 1:nch + 1]
    first = jnp.arange(cap // GATHER_SLOTS, dtype=jnp.int32) * GATHER_SLOTS
    chunk_of = lambda slot: jnp.sum((bounds[:, :, None, :] <= slot[None, None, :, None]).astype(jnp.int32), axis=-1)
    per = GATHER_ALIGN // LANES
    step = GATHER_TOKENS // GATHER_ALIGN
    lo_a = chunk_of(first) // per
    hi_a = chunk_of(first + GATHER_SLOTS - 1) // per
    g_hi = (hi_a - lo_a) // step + 1
    g_lo = jnp.minimum(lo_a, S // GATHER_ALIGN - step * g_hi)
    edges = cnt[:, :, 0:nch + 1:SCATTER_TOKENS // LANES]
    lo, hi = edges[:, :, :-1], edges[:, :, 1:]
    s_lo = lo // SCATTER_SLOTS
    s_n = jnp.where(hi > lo, (hi - 1) // SCATTER_SLOTS - s_lo + 1, 0)
    to_tiles = lambda t: jnp.swapaxes(t, 1, 2).reshape(-1)
    return g_lo.reshape(-1), g_hi.reshape(-1), to_tiles(s_lo), to_tiles(s_n)


META_ROWS = 16


GATHER_SLOTS = 128
GATHER_TOKENS = 512
GATHER_ALIGN = 256
SCATTER_SLOTS = 256
SCATTER_TOKENS = 512


def _gather_kernel(lo_ref, hi_ref, pos_ref, aff_ref, hf_ref, xs_ref, meta_ref, acc_ref, macc_ref, *, cap):
    R, tk = GATHER_SLOTS, GATHER_TOKENS
    nblk = cap // R
    base = (pl.program_id(0) * pl.num_programs(1) + pl.program_id(1)) * nblk
    mrow = lax.broadcasted_iota(jnp.int32, (META_ROWS, tk), 0)
    lane = lax.broadcasted_iota(jnp.int32, (1, tk), 1)
    for j in range(nblk):
        slot = j * R + lax.broadcasted_iota(jnp.int32, (R, tk), 0)
        acc_ref[...] = jnp.zeros_like(acc_ref)
        macc_ref[...] = jnp.zeros_like(macc_ref)

        first = lo_ref[base + j]

        def body(w, carry):
            t0 = pl.multiple_of(first * GATHER_ALIGN + w * tk, GATHER_ALIGN)
            onehot = jnp.where(slot == pos_ref[0, 0, :, pl.ds(t0, tk)], 1.0, 0.0).astype(BF16)
            acc_ref[...] += _dot(onehot, hf_ref[0, pl.ds(t0, tk), :])
            a = aff_ref[0, 0, :, pl.ds(t0, tk)]
            hi = a.astype(BF16).astype(F32)
            mid = (a - hi).astype(BF16).astype(F32)
            lo = (a - hi - mid).astype(BF16).astype(F32)
            t = t0 + lane
            pieces = (hi, mid, lo, (t // 64).astype(F32), (t % 64).astype(F32))
            meta = jnp.zeros((META_ROWS, tk), F32)
            for r, piece in enumerate(pieces):
                meta = jnp.where(mrow == r, piece, meta)
            macc_ref[...] += _dot_nt(meta.astype(BF16), onehot)
            return carry

        lax.fori_loop(0, hi_ref[base + j], body, 0)
        xs_ref[0, 0, j * R:(j + 1) * R, :] = acc_ref[...].astype(BF16)
        meta_ref[0, 0, :, j * R:(j + 1) * R] = macc_ref[...]


def _gather(g_lo, g_hi, pos, aff_t, hf, cap):
    B, E, S = pos.shape
    D = hf.shape[2]
    row = pl.BlockSpec((1, 1, 1, S), lambda b, e, *_: (b, e, 0, 0))
    return pl.pallas_call(
        functools.partial(_gather_kernel, cap=cap),
        grid_spec=pltpu.PrefetchScalarGridSpec(
            num_scalar_prefetch=2,
            grid=(B, E),
            in_specs=[row, row, pl.BlockSpec((1, S, D), lambda b, e, *_: (b, 0, 0))],
            out_specs=[pl.BlockSpec((1, 1, cap, D), lambda b, e, *_: (b, e, 0, 0)),
                       pl.BlockSpec((1, 1, META_ROWS, cap), lambda b, e, *_: (b, e, 0, 0))],
            scratch_shapes=[pltpu.VMEM((GATHER_SLOTS, D), F32), pltpu.VMEM((META_ROWS, GATHER_SLOTS), F32)]),
        out_shape=[jax.ShapeDtypeStruct((B, E, cap, D), BF16), jax.ShapeDtypeStruct((B, E, META_ROWS, cap), F32)],
        compiler_params=_cparams(("parallel", "arbitrary")),
        name="moe_gather",
    )(g_lo, g_hi, pos.reshape(B, E, 1, S), aff_t.reshape(B, E, 1, S), hf)


def _ffn_kernel(xs_ref, meta_ref, wg_ref, wu_ref, wd_ref, y_ref, *, tf):
    x = xs_ref[0, 0]
    cap = x.shape[0]
    FF = wg_ref.shape[2]
    y = jnp.zeros((cap, wd_ref.shape[2]), F32)
    for f in range(FF // tf):
        hg = _dot(x, wg_ref[0, :, f * tf:(f + 1) * tf])
        hu = _dot(x, wu_ref[0, :, f * tf:(f + 1) * tf])
        hid = (hg * _sigmoid(hg) * hu).astype(BF16)
        y = y + _dot(hid, wd_ref[0, f * tf:(f + 1) * tf, :])
    m = meta_ref[0, 0]
    g_row = m[0:1] + m[1:2] + m[2:3]
    eye = lax.broadcasted_iota(jnp.int32, (cap, cap), 0) == lax.broadcasted_iota(jnp.int32, (cap, cap), 1)
    g_col = jnp.sum(jnp.where(eye, g_row, 0.0), axis=1, keepdims=True)
    y_ref[0, 0] = (y * g_col).astype(BF16)


def _expert_ffn(xs, meta, w_gate, w_up, w_down, tf=512):
    B, E, cap, D = xs.shape
    FF = w_gate.shape[2]
    return pl.pallas_call(
        functools.partial(_ffn_kernel, tf=tf),
        grid=(E, B),
        in_specs=[pl.BlockSpec((1, 1, cap, D), lambda e, b: (b, e, 0, 0)),
                  pl.BlockSpec((1, 1, META_ROWS, cap), lambda e, b: (b, e, 0, 0)),
                  pl.BlockSpec((1, D, FF), lambda e, b: (e, 0, 0)),
                  pl.BlockSpec((1, D, FF), lambda e, b: (e, 0, 0)),
                  pl.BlockSpec((1, FF, D), lambda e, b: (e, 0, 0))],
        out_specs=pl.BlockSpec((1, 1, cap, D), lambda e, b: (b, e, 0, 0)),
        out_shape=jax.ShapeDtypeStruct((B, E, cap, D), BF16),
        compiler_params=_cparams(("parallel", "arbitrary")),
        name="moe_ffn",
    )(xs, meta, w_gate.astype(BF16), w_up.astype(BF16), w_down.astype(BF16))


def _scatter_kernel(lo_ref, n_ref, y_ref, meta_ref, x1_ref, gt_ref, g_ref, o_ref, acc_ref):
    E = y_ref.shape[1]
    tt, R = SCATTER_TOKENS, SCATTER_SLOTS
    tile = pl.program_id(0) * pl.num_programs(1) + pl.program_id(1)
    tok = pl.program_id(1) * tt + lax.broadcasted_iota(jnp.int32, (tt, R), 0)
    acc_ref[...] = jnp.zeros_like(acc_ref)

    def expert_body(e, carry):
        first = lo_ref[tile * E + e]

        def block_body(k, carry):
            r0 = pl.multiple_of((first + k) * R, R)
            m = meta_ref[0, e, :, pl.ds(r0, R)]
            idx = (m[3:4] * 64.0 + m[4:5]).astype(jnp.int32)
            onehot = jnp.where(tok == idx, 1.0, 0.0).astype(BF16)
            acc_ref[...] += _dot(onehot, y_ref[0, e, pl.ds(r0, R), :])
            return carry

        return lax.fori_loop(0, n_ref[tile * E + e], block_body, carry)

    lax.fori_loop(0, E, expert_body, 0)
    o_ref[0] = _rms(x1_ref[0] + gt_ref[0] * acc_ref[...], g_ref[...])


def _scatter_final(s_lo, s_n, y, meta, x1, gt2, g_final):
    B, E, cap, D = y.shape
    S = x1.shape[1]
    tt = SCATTER_TOKENS
    tile = pl.BlockSpec((1, tt, D), lambda b, i, *_: (b, i, 0))
    return pl.pallas_call(
        _scatter_kernel,
        grid_spec=pltpu.PrefetchScalarGridSpec(
            num_scalar_prefetch=2,
            grid=(B, S // tt),
            in_specs=[pl.BlockSpec((1, E, cap, D), lambda b, i, *_: (b, 0, 0, 0)),
                      pl.BlockSpec((1, E, META_ROWS, cap), lambda b, i, *_: (b, 0, 0, 0)),
                      tile, pl.BlockSpec((1, 1, D), lambda b, i, *_: (b, 0, 0)),
                      pl.BlockSpec((1, D), lambda b, i, *_: (0, 0))],
            out_specs=tile,
            scratch_shapes=[pltpu.VMEM((tt, D), F32)]),
        out_shape=jax.ShapeDtypeStruct((B, S, D), F32),
        compiler_params=_cparams(("parallel", "parallel")),
        name="moe_scatter_final",
    )(s_lo, s_n, y, meta, x1, gt2.reshape(B, 1, D), g_final.reshape(1, D))


def kernel(x, c, w_ada, b_ada, g_mix, w_in, b_in, conv_qk_w, conv_qk_b, mlstm_norm_g, conv_hy_w, conv_hy_b,
           hy_w1, hy_b1, hy_w2, hy_b2, hy_w3, hy_freq, hy_deltas, hy_bias, hyena_norm_g, w_out, g_ffn,
           w_router, w_gate, w_up, w_down, g_final):
    S = x.shape[1]
    cap = EC_CAPACITY_FACTOR * S // N_EXPERTS
    assert w_ada.shape[0] == 1, "single-layer block: the final RMSNorm is fused into the MoE scatter"
    l = 0
    mod = _modulation(c, w_ada[l], b_ada[l])
    sh1, sc1, gt1, sh2, sc2, gt2 = jnp.split(mod, 6, axis=-1)
    qkvo, hx, gates_t = _in_projection(x, sc1, sh1, g_mix[l], w_in[l], b_in[l])
    y_m = _mlstm(qkvo, gates_t, conv_qk_w[l], conv_qk_b[l], mlstm_norm_g[l])
    y_h = _hyena(hx, conv_hy_w[l], conv_hy_b[l], hy_w1[l], hy_b1[l], hy_w2[l], hy_b2[l], hy_w3[l],
                 hy_freq[l], hy_deltas[l], hy_bias[l], hyena_norm_g[l])
    x1, hf, aff_t = _out_projection(y_m, y_h, x, gt1, sc2, sh2, g_ffn[l], w_out[l], w_router[l])
    pos, cnt = _route(aff_t, cap)
    g_lo, g_hi, s_lo, s_n = _window_tables(cnt, S, cap)
    xs, meta = _gather(g_lo, g_hi, pos, aff_t, hf, cap)
    y = _expert_ffn(xs, meta, w_gate[l], w_up[l], w_down[l])
    return _scatter_final(s_lo, s_n, y, meta, x1, gt2, g_final)
```

```python
import functools
import math

import numpy as np
import jax
import jax.numpy as jnp
from jax import lax
from jax.experimental import pallas as pl
from jax.experimental.pallas import tpu as pltpu

F32 = jnp.float32
BF16 = jnp.bfloat16

MLSTM_HEADS = 4
HEAD_DIM = 128
MLSTM_CHUNK = 128
MLSTM_M_INIT = -1e30
HYENA_GROUP_DIM = 128
HYENA_ORDER = 2
HYENA_BANDS = 16
HYENA_WINDOW_SHIFT = 0.05
N_GATE_COLS = 4 * MLSTM_HEADS
N_EXPERTS = 16
EC_CAPACITY_FACTOR = 2
RMS_EPS = 1e-6

LANES = 128
SUBLANES = 8
FFT_N2 = 64
FFT_PITCH = 72
NEG_BIG = -1e30
VMEM_LIMIT = 56 * 1024 * 1024


def _cparams(sem, vmem=None, flags=None):
    return pltpu.CompilerParams(dimension_semantics=sem, vmem_limit_bytes=vmem or VMEM_LIMIT, flags=flags)


def _split(a):
    hi = a.astype(BF16)
    lo = (a - hi.astype(F32)).astype(BF16)
    return hi, lo


def _dot(a, b):
    return jnp.dot(a, b, preferred_element_type=F32)


def _dot_nt(a, b):
    return lax.dot_general(a, b, (((1,), (1,)), ((), ())), preferred_element_type=F32)


def _dot3(a, b):
    ah, al = _split(a)
    bh, bl = _split(b)
    return _dot(ah, bh) + _dot(ah, bl) + _dot(al, bh)


def _dot3_nt(a, b):
    ah, al = _split(a)
    bh, bl = _split(b)
    return _dot_nt(ah, bh) + _dot_nt(ah, bl) + _dot_nt(al, bh)


def _dotc(w_hi, w_lo, x, passes):
    xh, xl = _split(x)
    out = _dot(w_hi, xh)
    if passes == 3:
        out = out + _dot(w_hi, xl) + _dot(w_lo, xh)
    return out


def _rms(x, g):
    return x * lax.rsqrt(jnp.mean(x * x, axis=-1, keepdims=True) + RMS_EPS) * g


def _sigmoid(x):
    return 1.0 / (1.0 + jnp.exp(-x))


def _log_sigmoid(x):
    return jnp.minimum(x, 0.0) - jnp.log(1.0 + jnp.exp(-jnp.abs(x)))


def _mod_kernel(c_ref, w_ref, b_ref, o_ref):
    o_ref[...] = _dot3(c_ref[...], w_ref[...]) + b_ref[...]


def _modulation(c, w_ada, b_ada):
    B, D = c.shape
    n = w_ada.shape[1]
    tn = 768
    return pl.pallas_call(
        _mod_kernel,
        grid=(n // tn,),
        in_specs=[pl.BlockSpec((B, D), lambda j: (0, 0)),
                  pl.BlockSpec((D, tn), lambda j: (0, j)),
                  pl.BlockSpec((1, tn), lambda j: (0, j))],
        out_specs=pl.BlockSpec((B, tn), lambda j: (0, j)),
        out_shape=jax.ShapeDtypeStruct((B, n), F32),
        compiler_params=_cparams(("parallel",)),
        name="adaln_mod",
    )(c, w_ada, b_ada.reshape(1, n))


def _inproj_kernel(x_ref, sc_ref, sh_ref, g_ref, wq_ref, bq_ref, wh_ref, bh_ref, wgt_ref, bgt_ref,
                   qkvo_ref, hx_ref, gt_ref):
    x = x_ref[0]
    h = _rms(x, g_ref[...]) * (1.0 + sc_ref[0]) + sh_ref[0]
    hb = h.astype(BF16)
    qkvo_ref[0] = _dot(hb, wq_ref[...]) + bq_ref[...]
    hx_ref[0] = _dot(hb, wh_ref[...]) + bh_ref[...]
    gt_ref[0] = _dot3_nt(wgt_ref[...], h) + bgt_ref[...]


def _in_projection(x, sc1, sh1, g_mix, w_in, b_in, ts=512):
    B, S, D = x.shape
    nq = 4 * MLSTM_HEADS * HEAD_DIM
    nh = w_in.shape[1] - nq - N_GATE_COLS
    wq = w_in[:, :nq].astype(BF16)
    wh = w_in[:, nq + N_GATE_COLS:].astype(BF16)
    bq = b_in[:nq].reshape(1, nq)
    bh = b_in[nq + N_GATE_COLS:].reshape(1, nh)
    ng = MLSTM_HEADS * SUBLANES
    pad = lambda t: jnp.pad(t.reshape(4, MLSTM_HEADS, -1).transpose(1, 0, 2), ((0, 0), (0, 4), (0, 0))).reshape(ng, -1)
    wgt = pad(w_in[:, nq:nq + N_GATE_COLS].T)
    bgt = pad(b_in[nq:nq + N_GATE_COLS].reshape(N_GATE_COLS, 1))
    const = lambda b, i: (0, 0)
    return pl.pallas_call(
        _inproj_kernel,
        grid=(B, S // ts),
        in_specs=[pl.BlockSpec((1, ts, D), lambda b, i: (b, i, 0)),
                  pl.BlockSpec((1, 1, D), lambda b, i: (b, 0, 0)),
                  pl.BlockSpec((1, 1, D), lambda b, i: (b, 0, 0)),
                  pl.BlockSpec((1, D), const),
                  pl.BlockSpec((D, nq), const), pl.BlockSpec((1, nq), const),
                  pl.BlockSpec((D, nh), const), pl.BlockSpec((1, nh), const),
                  pl.BlockSpec((ng, D), const), pl.BlockSpec((ng, 1), const)],
        out_specs=[pl.BlockSpec((1, ts, nq), lambda b, i: (b, i, 0)),
                   pl.BlockSpec((1, ts, nh), lambda b, i: (b, i, 0)),
                   pl.BlockSpec((1, ng, ts), lambda b, i: (b, 0, i))],
        out_shape=[jax.ShapeDtypeStruct((B, S, nq), F32),
                   jax.ShapeDtypeStruct((B, S, nh), F32),
                   jax.ShapeDtypeStruct((B, ng, S), F32)],
        compiler_params=_cparams(("parallel", "parallel")),
        name="in_proj",
    )(x, sc1.reshape(B, 1, D), sh1.reshape(B, 1, D), g_mix.reshape(1, D), wq, bq, wh, bh, wgt, bgt)


def _short_conv_chunk(src_ref, lead, c, nc, w_ref, b_ref, rows):
    L = rows.shape[0]
    S = nc * L
    t0 = pl.multiple_of(c * L, L)
    cur = src_ref[lead, pl.ds(t0, L), :]
    p0 = pl.multiple_of(jnp.maximum(t0 - SUBLANES, 0), SUBLANES)
    n0 = pl.multiple_of(jnp.minimum(t0 + L, S - SUBLANES), SUBLANES)
    prev_row = src_ref[lead, pl.ds(p0, SUBLANES), :][SUBLANES - 1:SUBLANES, :]
    next_row = src_ref[lead, pl.ds(n0, SUBLANES), :][0:1, :]
    prev_row = jnp.where(c > 0, prev_row, 0.0)
    next_row = jnp.where(c < nc - 1, next_row, 0.0)
    up = jnp.where(rows == 0, prev_row, pltpu.roll(cur, 1, 0))
    dn = jnp.where(rows == L - 1, next_row, pltpu.roll(cur, L - 1, 0))
    return b_ref[...] + up * w_ref[0:1, :] + cur * w_ref[1:2, :] + dn * w_ref[2:3, :]


def _mlstm_kernel(q_ref, k_ref, v_ref, o_ref, gt_ref, wq_ref, wk_ref, bq_ref, bk_ref, ng_ref,
                  y_ref, qt_ref, kb_ref, vt_ref, c_ref, cl_ref, ld_ref, st_ref, pc_ref, cs_ref, kq_ref, *, nc):
    L = MLSTM_CHUNK
    rows = lax.broadcasted_iota(jnp.int32, (L, LANES), 0)
    cols = lax.broadcasted_iota(jnp.int32, (L, LANES), 1)
    pad = 2 * SUBLANES
    piece_row = lax.broadcasted_iota(jnp.int32, (SUBLANES, LANES), 0)
    k_scale = HEAD_DIM ** -0.5

    def conv_body(c, carry):
        t0 = pl.multiple_of(c * L, L)
        yq = _short_conv_chunk(q_ref, 0, c, nc, wq_ref, bq_ref, rows)
        qt_ref[c] = (yq * _sigmoid(yq)).T.astype(BF16)
        yk = _short_conv_chunk(k_ref, 0, c, nc, wk_ref, bk_ref, rows)
        kb_ref[pl.ds(t0, L), :] = (yk * _sigmoid(yk) * k_scale).astype(BF16)
        vt_ref[c] = v_ref[0, pl.ds(t0, L), :].T
        gates = gt_ref[0, 0, :, pl.ds(t0, L)]
        lf = _log_sigmoid(jnp.where(piece_row < 3, gates[1:2], gates[3:4]))
        hi = lf.astype(BF16).astype(F32)
        mid = (lf - hi).astype(BF16).astype(F32)
        kind = piece_row % 3
        pc_ref[pl.ds(pl.multiple_of(c * SUBLANES, SUBLANES), SUBLANES), :] = jnp.where(
            piece_row >= 6, 0.0, jnp.where(kind == 0, hi, jnp.where(kind == 1, mid, lf - hi - mid)))
        return carry

    lax.fori_loop(0, nc, conv_body, 0, unroll=2)

    pieces = pc_ref[...].astype(BF16)
    fwd_rows = lax.broadcasted_iota(jnp.int32, pc_ref.shape, 0) % SUBLANES < 3
    cs_ref[...] = jnp.where(fwd_rows, _dot(pieces, (rows <= cols).astype(BF16)),
                            _dot(pieces, (rows >= cols).astype(BF16)))

    ROW_B, ROW_MAX, ROW_G, ROW_A, ROW_N, ROW_M = range(6)

    def stat_tile(rows_by_index, base=None):
        tile = jnp.zeros((SUBLANES, LANES), F32) if base is None else base
        for r, value in rows_by_index.items():
            tile = jnp.where(piece_row == r, value, tile)
        return tile

    def local_pass(c, rev):
        t0 = pl.multiple_of(c * L, L)
        slot = 2 * c + (1 if rev else 0)
        kb = kb_ref[pl.ds(t0, L), :]
        gates = gt_ref[0, 0, :, pl.ds(t0, L)]
        i_row = gates[(2 if rev else 0):(3 if rev else 1)]
        cs = cs_ref[pl.ds(pl.multiple_of(c * SUBLANES, SUBLANES), SUBLANES), :]
        o = 3 if rev else 0
        b_row = cs[o:o + 1] + cs[o + 1:o + 2] + cs[o + 2:o + 3]
        keep = (rows >= cols) if rev else (rows <= cols)
        g = b_row[:, 0:1] if rev else b_row[:, L - 1:L]
        w_tile = jnp.broadcast_to(i_row - b_row, (L, LANES)).T
        log_d = jnp.where(keep, w_tile + b_row, NEG_BIG)
        ld_ref[slot] = log_d
        w_end = g - b_row + i_row
        a = jnp.max(w_end, axis=1, keepdims=True)
        e_end = jnp.exp(w_end - a)
        ve = jnp.concatenate([vt_ref[c] * e_end, jnp.broadcast_to(e_end, (pad, L))], axis=0).astype(BF16)
        cn = _dot(ve, kb)
        cl_ref[slot] = cn[:HEAD_DIM]
        st_ref[slot] = stat_tile({ROW_B: b_row, ROW_MAX: jnp.max(log_d, axis=0, keepdims=True), ROW_G: g,
                                  ROW_A: a, ROW_N: cn[HEAD_DIM:HEAD_DIM + 1]})

    def local_body(c, carry):
        kq_ref[c] = _dot(kb_ref[pl.ds(pl.multiple_of(c * L, L), L), :], qt_ref[c])
        local_pass(c, False)
        local_pass(c, True)
        return carry

    lax.fori_loop(0, nc, local_body, 0, unroll=8)

    def scan_step(c, rev, n, m):
        slot = 2 * c + (1 if rev else 0)
        side = 1 if rev else 0
        stats = st_ref[slot]
        g = stats[ROW_G:ROW_G + 1]
        a = stats[ROW_A:ROW_A + 1]
        m_new = jnp.maximum(g + m, a)
        s_prev = jnp.exp(g + m - m_new)
        s_loc = jnp.exp(a - m_new)
        c_in = c_ref[side]
        c_ref[side] = s_prev * c_in + s_loc * cl_ref[slot]
        cl_ref[slot] = c_in
        n_new = s_prev * n + s_loc * stats[ROW_N:ROW_N + 1]
        st_ref[slot] = stat_tile({ROW_N: n, ROW_M: m}, base=stats)
        return n_new, m_new

    c_ref[...] = jnp.zeros_like(c_ref)
    n0 = jnp.zeros((1, HEAD_DIM), F32)
    m0 = jnp.full((1, LANES), MLSTM_M_INIT, F32)

    def scan_body(j, carry):
        n_f, m_f, n_b, m_b = carry
        n_f, m_f = scan_step(j, False, n_f, m_f)
        n_b, m_b = scan_step(nc - 1 - j, True, n_b, m_b)
        return n_f, m_f, n_b, m_b

    lax.fori_loop(0, nc, scan_body, (n0, m0, n0, m0))

    def output_pass(c, rev):
        t0 = pl.multiple_of(c * L, L)
        slot = 2 * c + (1 if rev else 0)
        qt = qt_ref[c]
        stats = st_ref[slot]
        log_inter = stats[ROW_B:ROW_B + 1] + stats[ROW_M:ROW_M + 1]
        m_t = jnp.maximum(log_inter, stats[ROW_MAX:ROW_MAX + 1])
        d = jnp.exp(ld_ref[slot] - m_t)
        e_inter = jnp.exp(log_inter - m_t)
        s = (kq_ref[c] * d).astype(BF16)
        vn = _dot(jnp.concatenate([vt_ref[c].astype(BF16), jnp.ones((pad, L), BF16)], axis=0), s)
        n_in = stats[ROW_N:ROW_N + 1]
        cq = _dot(jnp.concatenate([cl_ref[slot], jnp.broadcast_to(n_in, (pad, LANES))], axis=0).astype(BF16), qt)
        den = vn[HEAD_DIM:HEAD_DIM + 1] + e_inter * cq[HEAD_DIM:HEAD_DIM + 1]
        scale = 1.0 / jnp.maximum(jnp.abs(den), jnp.exp(-m_t))
        return ((vn[:HEAD_DIM] + e_inter * cq[:HEAD_DIM]) * scale).T

    def output_body(c, carry):
        t0 = pl.multiple_of(c * L, L)
        hs = output_pass(c, False) + output_pass(c, True)
        y_ref[0, pl.ds(t0, L), :] = _sigmoid(o_ref[0, pl.ds(t0, L), :]) * _rms(hs, ng_ref[...])
        return carry

    lax.fori_loop(0, nc, output_body, 0, unroll=8)


def _mlstm(qkvo, gates_t, conv_w, conv_b, norm_g):
    B, S, _ = qkvo.shape
    H, Dh = MLSTM_HEADS, HEAD_DIM
    nc = S // MLSTM_CHUNK
    seq = lambda off: pl.BlockSpec((1, S, Dh), lambda b, h: (b, 0, off + h))
    vec = lambda r, off: pl.BlockSpec((r, Dh), lambda b, h: (0, off + h))
    return pl.pallas_call(
        functools.partial(_mlstm_kernel, nc=nc),
        grid=(B, H),
        in_specs=[seq(0), seq(H), seq(2 * H), seq(3 * H),
                  pl.BlockSpec((1, 1, SUBLANES, S), lambda b, h: (b, h, 0, 0)),
                  vec(3, 0), vec(3, H), vec(1, 0), vec(1, H), vec(1, 0)],
        out_specs=pl.BlockSpec((1, S, Dh), lambda b, h: (b, 0, h)),
        out_shape=jax.ShapeDtypeStruct((B, S, H * Dh), F32),
        scratch_shapes=[pltpu.VMEM((nc, Dh, MLSTM_CHUNK), BF16), pltpu.VMEM((S, Dh), BF16),
                        pltpu.VMEM((nc, Dh, MLSTM_CHUNK), F32), pltpu.VMEM((2, Dh, Dh), F32),
                        pltpu.VMEM((2 * nc, Dh, Dh), F32), pltpu.VMEM((2 * nc, MLSTM_CHUNK, MLSTM_CHUNK), F32),
                        pltpu.VMEM((2 * nc, SUBLANES, LANES), F32), pltpu.VMEM((nc * SUBLANES, LANES), F32),
                        pltpu.VMEM((nc * SUBLANES, LANES), F32), pltpu.VMEM((nc, MLSTM_CHUNK, MLSTM_CHUNK), F32)],
        compiler_params=_cparams(("parallel", "arbitrary")),
        name="mlstm",
    )(qkvo, qkvo, qkvo, qkvo, gates_t.reshape(B, H, SUBLANES, S), conv_w, conv_w, conv_b.reshape(1, -1), conv_b.reshape(1, -1),
      norm_g.reshape(1, -1))


FILTER_PASSES = 3
CONV_PASSES = 1
FFT_UNROLL = 8


def _hilo(m):
    m32 = jnp.asarray(m, F32)
    hi = m32.astype(BF16)
    return hi, (m32 - hi.astype(F32)).astype(BF16)


def _stack_complex(m):
    return np.block([[m.real, -m.imag], [m.imag, m.real]])


def _dft_constants(S):
    N = 2 * S
    N2 = FFT_N2
    N1 = N // N2
    k1 = np.arange(N1)
    n2 = np.arange(N2)
    f1 = np.exp(-2j * np.pi * np.outer(k1, np.arange(N1)) / N1)
    f2 = np.exp(-2j * np.pi * np.outer(n2, n2) / N2)
    tw = np.exp(-2j * np.pi * np.outer(k1, n2) / N)
    g1 = np.conj(f1).T[:N1 // 2] / N
    consts = dict(
        f1_real=_hilo(np.concatenate([f1.real, f1.imag], axis=0)),
        f1_half=_hilo(_stack_complex(f1[:, :N1 // 2])),
        f2=_hilo(_stack_complex(f2)),
        f2_inv=_hilo(_stack_complex(np.conj(f2))),
        g1=_hilo(_stack_complex(g1)),
        tw_re=jnp.asarray(np.broadcast_to(tw.real[:, :, None], (N1, N2, LANES)), F32),
        tw_im=jnp.asarray(np.broadcast_to(tw.imag[:, :, None], (N1, N2, LANES)), F32),
    )
    return N1, N2, consts


def _hy_hidden_kernel(w1t_ref, w1c_ref, w1s_ref, b1_ref, w2_ref, b2_ref, fr_ref, o_ref, *, S, T):
    j = pl.program_id(0) * T + lax.broadcasted_iota(jnp.int32, (T, 1), 0)
    p = jnp.where(j < S, j, 2 * S - j).astype(F32)
    t = p / (S - 1)
    w = (2.0 * math.pi) * p / S
    band = lax.broadcasted_iota(jnp.int32, (1, HYENA_BANDS), 1).astype(F32)
    bands = 1e-4 + band * ((HYENA_BANDS - 1 - 1e-4) / (HYENA_BANDS - 1))
    arg = bands * w
    pre = t * w1t_ref[...] + _dot3(jnp.cos(arg), w1c_ref[...]) + _dot3(-jnp.sin(arg), w1s_ref[...]) + b1_ref[...]
    hid = jnp.sin(fr_ref[...] * pre)
    o_ref[...] = jnp.sin(fr_ref[...] * (_dot3(hid, w2_ref[...]) + b2_ref[...]))


def _hy_hidden(S, w1, b1, w2, b2, freq):
    N = 2 * S
    T = 1024
    Hd = w2.shape[0]
    full = lambda a: pl.BlockSpec(a.shape, lambda i: (0,) * a.ndim)
    args = (w1[0:1], w1[1:1 + HYENA_BANDS], w1[1 + HYENA_BANDS:], b1.reshape(1, Hd), w2, b2.reshape(1, Hd),
            freq.reshape(1, Hd))
    return pl.pallas_call(
        functools.partial(_hy_hidden_kernel, S=S, T=T),
        grid=(N // T,),
        in_specs=[full(a) for a in args],
        out_specs=pl.BlockSpec((T, Hd), lambda i: (i, 0)),
        out_shape=jax.ShapeDtypeStruct((N, Hd), F32),
        compiler_params=_cparams(("parallel",)),
        name="hyena_hidden",
    )(*args)


def _fft_stage2(a_re, a_im, k1, twr_ref, twi_ref, f2_hi, f2_lo, passes):
    r0 = pl.multiple_of(k1 * FFT_PITCH, SUBLANES)
    ar = a_re[pl.ds(r0, FFT_N2), :]
    ai = a_im[pl.ds(r0, FFT_N2), :]
    twr = twr_ref[k1]
    twi = twi_ref[k1]
    t = jnp.concatenate([ar * twr - ai * twi, ar * twi + ai * twr], axis=0)
    return _dotc(f2_hi, f2_lo, t, passes)


def _hy_filter_kernel(hid_ref, w3f_ref, w3b_ref, df_ref, db_ref, bias_ref, f1h_ref, f1l_ref, f2h_ref, f2l_ref,
                      twr_ref, twi_ref, kr_ref, ki_ref, u_ref, a_re, a_im, *, S, N1):
    N2, P = FFT_N2, FFT_PITCH
    T = 512
    nb = T // N2

    def gen_body(i, carry):
        r0 = pl.multiple_of(i * T, T)
        hid = hid_ref[pl.ds(r0, T), :]
        j = r0 + lax.broadcasted_iota(jnp.int32, (T, 1), 0)
        t = jnp.where(j < S, j, 2 * S - j).astype(F32) / (S - 1)
        kf = _dot3(hid, w3f_ref[...]) * (jnp.exp(-t * jnp.abs(df_ref[...])) + HYENA_WINDOW_SHIFT)
        kb = _dot3(hid, w3b_ref[...]) * (jnp.exp(-t * jnp.abs(db_ref[...])) + HYENA_WINDOW_SHIFT)
        ker = (jnp.where(j < S, kf, 0.0) + jnp.where((j > S) | (j == 0), kb, 0.0)
               + jnp.where(j == 0, bias_ref[0], 0.0))
        for b in range(nb):
            u_ref[pl.ds(pl.multiple_of((i * nb + b) * P, SUBLANES), N2), :] = ker[b * N2:(b + 1) * N2]
        return carry

    lax.fori_loop(0, 2 * S // T, gen_body, 0)

    def stage1(n2, carry):
        x = u_ref[pl.ds(n2, N1, stride=P), :]
        a = _dotc(f1h_ref[...], f1l_ref[...], x, FILTER_PASSES)
        a_re[pl.ds(n2, N1, stride=P), :] = a[:N1]
        a_im[pl.ds(n2, N1, stride=P), :] = a[N1:]
        return carry

    lax.fori_loop(0, N2, stage1, 0, unroll=FFT_UNROLL)

    def stage2(k1, carry):
        x = _fft_stage2(a_re, a_im, k1, twr_ref, twi_ref, f2h_ref[...], f2l_ref[...], FILTER_PASSES)
        r0 = pl.multiple_of(k1 * N2, N2)
        kr_ref[0, pl.ds(r0, N2), :] = x[:N2]
        ki_ref[0, pl.ds(r0, N2), :] = x[N2:]
        return carry

    lax.fori_loop(0, N1, stage2, 0, unroll=2 * FFT_UNROLL)


def _hy_filter_spectrum(S, hid, w3, deltas, bias, N1, consts):
    N = 2 * S
    N2, P = FFT_N2, FFT_PITCH
    Hd = hid.shape[1]
    nblk = w3.shape[1] // (2 * HYENA_ORDER * LANES)
    f1h, f1l = consts["f1_real"]
    f2h, f2l = consts["f2"]
    full = lambda a: pl.BlockSpec(a.shape, lambda o, j: (0,) * a.ndim, pipeline_mode=pl.Buffered(1))
    colf = lambda o, j: (0, (2 * o) * nblk + j)
    colb = lambda o, j: (0, (2 * o + 1) * nblk + j)
    out = jax.ShapeDtypeStruct((HYENA_ORDER, N, nblk * LANES), F32)
    return pl.pallas_call(
        functools.partial(_hy_filter_kernel, S=S, N1=N1),
        grid=(HYENA_ORDER, nblk),
        in_specs=[full(hid),
                  pl.BlockSpec((Hd, LANES), colf), pl.BlockSpec((Hd, LANES), colb),
                  pl.BlockSpec((1, LANES), colf), pl.BlockSpec((1, LANES), colb),
                  pl.BlockSpec((1, 1, LANES), lambda o, j: (o, 0, j)),
                  full(f1h), full(f1l), full(f2h), full(f2l), full(consts["tw_re"]), full(consts["tw_im"])],
        out_specs=[pl.BlockSpec((1, N, LANES), lambda o, j: (o, 0, j))] * 2,
        out_shape=[out, out],
        scratch_shapes=[pltpu.VMEM((N1 * P, LANES), F32)] * 3,
        compiler_params=_cparams(("parallel", "parallel")),
        name="hyena_filter_spectrum",
    )(hid, w3, w3, deltas.reshape(1, -1), deltas.reshape(1, -1), bias.reshape(HYENA_ORDER, 1, -1),
      f1h, f1l, f2h, f2l, consts["tw_re"], consts["tw_im"])


def _hy_conv_kernel(u_ref, x_ref, kr_ref, ki_ref, wu_ref, bu_ref, wx_ref, bx_ref, ng_ref,
                    f1h_ref, f1l_ref, f2h_ref, f2l_ref, f2ih_ref, f2il_ref, g1h_ref, g1l_ref, twr_ref, twi_ref,
                    o_ref, u_re, u_im, a_re, a_im, *, N1, conv_u, last):
    N2, P = FFT_N2, FFT_PITCH
    nh = N1 // 2
    rows = lax.broadcasted_iota(jnp.int32, (N2, LANES), 0)

    def load_body(c, carry):
        r0 = pl.multiple_of(c * P, SUBLANES)
        for b, dst in ((0, u_re), (1, u_im)):
            if conv_u:
                dst[pl.ds(r0, N2), :] = _short_conv_chunk(u_ref, b, c, nh, wu_ref, bu_ref, rows)
            else:
                dst[pl.ds(r0, N2), :] = u_ref[b, pl.ds(pl.multiple_of(c * N2, N2), N2), :]
        return carry

    lax.fori_loop(0, nh, load_body, 0, unroll=4)

    def stage1(n2, carry):
        x = jnp.concatenate([u_re[pl.ds(n2, nh, stride=P), :], u_im[pl.ds(n2, nh, stride=P), :]], axis=0)
        a = _dotc(f1h_ref[...], f1l_ref[...], x, CONV_PASSES)
        a_re[pl.ds(n2, N1, stride=P), :] = a[:N1]
        a_im[pl.ds(n2, N1, stride=P), :] = a[N1:]
        return carry

    lax.fori_loop(0, N2, stage1, 0, unroll=FFT_UNROLL)

    def stage2(k1, carry):
        x = _fft_stage2(a_re, a_im, k1, twr_ref, twi_ref, f2h_ref[...], f2l_ref[...], CONV_PASSES)
        s0 = pl.multiple_of(k1 * N2, N2)
        kr = kr_ref[0, pl.ds(s0, N2), :]
        ki = ki_ref[0, pl.ds(s0, N2), :]
        xr, xi = x[:N2], x[N2:]
        y = jnp.concatenate([xr * kr - xi * ki, xr * ki + xi * kr], axis=0)
        b = _dotc(f2ih_ref[...], f2il_ref[...], y, CONV_PASSES)
        br, bi = b[:N2], b[N2:]
        twr = twr_ref[k1]
        twi = twi_ref[k1]
        r0 = pl.multiple_of(k1 * P, SUBLANES)
        a_re[pl.ds(r0, N2), :] = br * twr + bi * twi
        a_im[pl.ds(r0, N2), :] = bi * twr - br * twi
        return carry

    lax.fori_loop(0, N1, stage2, 0, unroll=2 * FFT_UNROLL)

    def inv2(n2, carry):
        b = jnp.concatenate([a_re[pl.ds(n2, N1, stride=P), :], a_im[pl.ds(n2, N1, stride=P), :]], axis=0)
        y = _dotc(g1h_ref[...], g1l_ref[...], b, CONV_PASSES)
        u_re[pl.ds(n2, nh, stride=P), :] = y[:nh]
        u_im[pl.ds(n2, nh, stride=P), :] = y[nh:]
        return carry

    lax.fori_loop(0, N2, inv2, 0, unroll=FFT_UNROLL)

    def out_body(c, carry):
        r0 = pl.multiple_of(c * P, SUBLANES)
        t0 = pl.multiple_of(c * N2, N2)
        for b, src in ((0, u_re), (1, u_im)):
            z = _short_conv_chunk(x_ref, b, c, nh, wx_ref, bx_ref, rows) * src[pl.ds(r0, N2), :]
            o_ref[b, pl.ds(t0, N2), :] = _rms(z, ng_ref[...]) if last else z
        return carry

    lax.fori_loop(0, nh, out_body, 0, unroll=4)


def _hy_conv(u, u_blk, x, x_blk, kr, ki, order, conv_w, conv_b, norm_g, N1, consts, conv_u, last):
    B, S, _ = u.shape
    N2, P = FFT_N2, FFT_PITCH
    N = 2 * S
    nblk = kr.shape[2] // LANES
    pair = 2
    single = pl.Buffered(1)
    full = lambda a: pl.BlockSpec(a.shape, lambda j, p: (0,) * a.ndim, pipeline_mode=single)
    seq = lambda off: pl.BlockSpec((pair, S, LANES), lambda j, p: (p, 0, off + j))
    vec = lambda r, off: pl.BlockSpec((r, LANES), lambda j, p: (0, off + j))
    spec = pl.BlockSpec((1, N, LANES), lambda j, p: (order, 0, j), pipeline_mode=single)
    mats = [*consts["f1_half"], *consts["f2"], *consts["f2_inv"], *consts["g1"], consts["tw_re"], consts["tw_im"]]
    return pl.pallas_call(
        functools.partial(_hy_conv_kernel, N1=N1, conv_u=conv_u, last=last),
        grid=(nblk, B // pair),
        in_specs=[seq(u_blk), seq(x_blk), spec, spec,
                  vec(3, u_blk if conv_u else 0), vec(1, u_blk if conv_u else 0), vec(3, x_blk), vec(1, x_blk),
                  pl.BlockSpec((1, LANES), lambda j, p: (0, j))] + [full(m) for m in mats],
        out_specs=pl.BlockSpec((pair, S, LANES), lambda j, p: (p, 0, j)),
        out_shape=jax.ShapeDtypeStruct((B, S, nblk * LANES), F32),
        scratch_shapes=[pltpu.VMEM((N1 // 2 * P, LANES), F32)] * 2 + [pltpu.VMEM((N1 * P, LANES), F32)] * 2,
        compiler_params=_cparams(("parallel", "parallel")),
        name=f"hyena_conv{order}",
    )(u, x, kr, ki, conv_w, conv_b.reshape(1, -1), conv_w, conv_b.reshape(1, -1), norm_g.reshape(1, -1), *mats)


def _hyena(hx, conv_w, conv_b, w1, b1, w2, b2, w3, freq, deltas, bias, norm_g):
    B, S, C3 = hx.shape
    nblk = C3 // (HYENA_ORDER + 1) // LANES
    N1, N2, consts = _dft_constants(S)
    hid = _hy_hidden(S, w1, b1, w2, b2, freq)
    kr, ki = _hy_filter_spectrum(S, hid, w3, deltas, bias, N1, consts)
    z1 = _hy_conv(hx, 2 * nblk, hx, 0, kr, ki, 0, conv_w, conv_b, norm_g, N1, consts, True, False)
    return _hy_conv(z1, 0, hx, nblk, kr, ki, 1, conv_w, conv_b, norm_g, N1, consts, False, True)


def _outproj_kernel(ym_ref, yh_ref, x_ref, gt_ref, sc_ref, sh_ref, g_ref, wm_ref, wh_ref, wr_ref,
                    x1_ref, hf_ref, aff_ref):
    mixed = _dot(ym_ref[0].astype(BF16), wm_ref[...]) + _dot(yh_ref[0].astype(BF16), wh_ref[...])
    x1 = x_ref[0] + gt_ref[0] * mixed
    x1_ref[0] = x1
    hf = _rms(x1, g_ref[...]) * (1.0 + sc_ref[0]) + sh_ref[0]
    hf_ref[0] = hf.astype(BF16)
    logits = _dot3_nt(wr_ref[...], hf)
    e = jnp.exp(logits - jnp.max(logits, axis=0, keepdims=True))
    aff_ref[0] = e / jnp.sum(e, axis=0, keepdims=True)


def _out_projection(y_m, y_h, x, gt1, sc2, sh2, g_ffn, w_out, w_router, ts=512):
    B, S, D = x.shape
    Wm = y_m.shape[2]
    E = w_router.shape[1]
    wm = w_out[:Wm].astype(BF16)
    wh = w_out[Wm:].astype(BF16)
    tile = lambda w: pl.BlockSpec((1, ts, w), lambda b, i: (b, i, 0))
    row = pl.BlockSpec((1, 1, D), lambda b, i: (b, 0, 0))
    full = lambda a: pl.BlockSpec(a.shape, lambda b, i: (0,) * a.ndim)
    wr = w_router.T
    g = g_ffn.reshape(1, D)
    return pl.pallas_call(
        _outproj_kernel,
        grid=(B, S // ts),
        in_specs=[tile(Wm), tile(y_h.shape[2]), tile(D), row, row, row, full(g), full(wm), full(wh), full(wr)],
        out_specs=[tile(D), tile(D), pl.BlockSpec((1, E, ts), lambda b, i: (b, 0, i))],
        out_shape=[jax.ShapeDtypeStruct((B, S, D), F32), jax.ShapeDtypeStruct((B, S, D), BF16),
                   jax.ShapeDtypeStruct((B, E, S), F32)],
        compiler_params=_cparams(("parallel", "parallel")),
        name="out_proj_router",
    )(y_m, y_h, x, gt1.reshape(B, 1, D), sc2.reshape(B, 1, D), sh2.reshape(B, 1, D), g, wm, wh, wr)


def _route_kernel(aff_ref, pos_ref, cnt_ref, *, S, cap):
    aff = aff_ref[0]
    E = aff.shape[0]
    tpos = lax.broadcasted_iota(jnp.int32, (E, S), 1)
    count = lambda mask: jnp.sum(jnp.where(mask, 1.0, 0.0), axis=1, keepdims=True)
    as_float = lambda word: lax.bitcast_convert_type(word, F32)

    def value_step(i, prefix):
        cand = prefix | jnp.left_shift(1, 30 - i)
        return jnp.where(count(aff >= as_float(cand)) >= cap, cand, prefix)

    thresh = as_float(lax.fori_loop(0, 31, value_step, jnp.zeros((E, 1), jnp.int32)))
    gt = aff > thresh
    eq = aff == thresh
    need = cap - count(gt)
    nbits = S.bit_length() - 1

    def index_step(i, x):
        cand = x | jnp.left_shift(1, nbits - 1 - i)
        return jnp.where(count(eq & (tpos < cand)) < need, cand, x)

    last_tie = lax.fori_loop(0, nbits, index_step, jnp.zeros((E, 1), jnp.int32))
    sel = jnp.where(gt | (eq & (tpos <= last_tie)), 1.0, 0.0)
    before = (lax.broadcasted_iota(jnp.int32, (LANES, LANES), 0)
              < lax.broadcasted_iota(jnp.int32, (LANES, LANES), 1)).astype(BF16)
    carry = jnp.zeros((E, 1), F32)
    lane = lax.broadcasted_iota(jnp.int32, (E, LANES), 1)
    starts = jnp.zeros((E, LANES), F32)
    for c in range(S // LANES):
        sc = sel[:, c * LANES:(c + 1) * LANES]
        rank = _dot(sc.astype(BF16), before) + carry
        pos_ref[0, :, c * LANES:(c + 1) * LANES] = jnp.where(sc > 0.0, rank, -1.0).astype(jnp.int32)
        starts = jnp.where(lane == c, carry, starts)
        carry = carry + jnp.sum(sc, axis=1, keepdims=True)
    cnt_ref[0] = jnp.where(lane >= S // LANES, carry, starts).astype(jnp.int32)


def _route(aff_t, cap):
    B, E, S = aff_t.shape
    assert S // LANES < LANES
    blk = pl.BlockSpec((1, E, S), lambda b: (b, 0, 0))
    return pl.pallas_call(
        functools.partial(_route_kernel, S=S, cap=cap),
        grid=(B,),
        in_specs=[blk],
        out_specs=[blk, pl.BlockSpec((1, E, LANES), lambda b: (b, 0, 0))],
        out_shape=[jax.ShapeDtypeStruct((B, E, S), jnp.int32), jax.ShapeDtypeStruct((B, E, LANES), jnp.int32)],
        compiler_params=_cparams(("parallel",)),
        name="route_topk",
    )(aff_t)


def _window_tables(cnt, S, cap):
    B, E, _ = cnt.shape
    nch = S // LANES
    bounds = cnt[:, :, 1:nch + 1]
    first = jnp.arange(cap // GATHER_SLOTS, dtype=jnp.int32) * GATHER_SLOTS
    chunk_of = lambda slot: jnp.sum((bounds[:, :, None, :] <= slot[None, None, :, None]).astype(jnp.int32), axis=-1)
    per = GATHER_ALIGN // LANES
    step = GATHER_TOKENS // GATHER_ALIGN
    lo_a = chunk_of(first) // per
    hi_a = chunk_of(first + GATHER_SLOTS - 1) // per
    g_hi = (hi_a - lo_a) // step + 1
    g_lo = jnp.minimum(lo_a, S // GATHER_ALIGN - step * g_hi)
    edges = cnt[:, :, 0:nch + 1:SCATTER_TOKENS // LANES]
    lo, hi = edges[:, :, :-1], edges[:, :, 1:]
    s_lo = lo // SCATTER_SLOTS
    s_n = jnp.where(hi > lo, (hi - 1) // SCATTER_SLOTS - s_lo + 1, 0)
    to_tiles = lambda t: jnp.swapaxes(t, 1, 2).reshape(-1)
    return g_lo.reshape(-1), g_hi.reshape(-1), to_tiles(s_lo), to_tiles(s_n)


META_ROWS = 16


GATHER_SLOTS = 128
GATHER_TOKENS = 512
GATHER_ALIGN = 256
SCATTER_SLOTS = 256
SCATTER_TOKENS = 512


def _gather_kernel(lo_ref, hi_ref, pos_ref, aff_ref, hf_ref, xs_ref, meta_ref, acc_ref, macc_ref, *, cap):
    R, tk = GATHER_SLOTS, GATHER_TOKENS
    nblk = cap // R
    base = (pl.program_id(0) * pl.num_programs(1) + pl.program_id(1)) * nblk
    mrow = lax.broadcasted_iota(jnp.int32, (META_ROWS, tk), 0)
    lane = lax.broadcasted_iota(jnp.int32, (1, tk), 1)
    for j in range(nblk):
        slot = j * R + lax.broadcasted_iota(jnp.int32, (R, tk), 0)
        acc_ref[...] = jnp.zeros_like(acc_ref)
        macc_ref[...] = jnp.zeros_like(macc_ref)

        first = lo_ref[base + j]

        def body(w, carry):
            t0 = pl.multiple_of(first * GATHER_ALIGN + w * tk, GATHER_ALIGN)
            onehot = jnp.where(slot == pos_ref[0, 0, :, pl.ds(t0, tk)], 1.0, 0.0).astype(BF16)
            acc_ref[...] += _dot(onehot, hf_ref[0, pl.ds(t0, tk), :])
            a = aff_ref[0, 0, :, pl.ds(t0, tk)]
            hi = a.astype(BF16).astype(F32)
            mid = (a - hi).astype(BF16).astype(F32)
            lo = (a - hi - mid).astype(BF16).astype(F32)
            t = t0 + lane
            pieces = (hi, mid, lo, (t // 64).astype(F32), (t % 64).astype(F32))
            meta = jnp.zeros((META_ROWS, tk), F32)
            for r, piece in enumerate(pieces):
                meta = jnp.where(mrow == r, piece, meta)
            macc_ref[...] += _dot_nt(meta.astype(BF16), onehot)
            return carry

        lax.fori_loop(0, hi_ref[base + j], body, 0)
        xs_ref[0, 0, j * R:(j + 1) * R, :] = acc_ref[...].astype(BF16)
        meta_ref[0, 0, :, j * R:(j + 1) * R] = macc_ref[...]


def _gather(g_lo, g_hi, pos, aff_t, hf, cap):
    B, E, S = pos.shape
    D = hf.shape[2]
    row = pl.BlockSpec((1, 1, 1, S), lambda b, e, *_: (b, e, 0, 0))
    return pl.pallas_call(
        functools.partial(_gather_kernel, cap=cap),
        grid_spec=pltpu.PrefetchScalarGridSpec(
            num_scalar_prefetch=2,
            grid=(B, E),
            in_specs=[row, row, pl.BlockSpec((1, S, D), lambda b, e, *_: (b, 0, 0))],
            out_specs=[pl.BlockSpec((1, 1, cap, D), lambda b, e, *_: (b, e, 0, 0)),
                       pl.BlockSpec((1, 1, META_ROWS, cap), lambda b, e, *_: (b, e, 0, 0))],
            scratch_shapes=[pltpu.VMEM((GATHER_SLOTS, D), F32), pltpu.VMEM((META_ROWS, GATHER_SLOTS), F32)]),
        out_shape=[jax.ShapeDtypeStruct((B, E, cap, D), BF16), jax.ShapeDtypeStruct((B, E, META_ROWS, cap), F32)],
        compiler_params=_cparams(("parallel", "arbitrary")),
        name="moe_gather",
    )(g_lo, g_hi, pos.reshape(B, E, 1, S), aff_t.reshape(B, E, 1, S), hf)


def _ffn_kernel(xs_ref, meta_ref, wg_ref, wu_ref, wd_ref, y_ref, wdb_ref, *, tf):
    x = xs_ref[0, 0]
    cap = x.shape[0]
    FF = wg_ref.shape[2]

    @pl.when(pl.program_id(1) == 0)
    def _():
        for f in range(FF // tf):
            wdb_ref[f * tf:(f + 1) * tf, :] = wd_ref[0, f * tf:(f + 1) * tf, :].astype(BF16)

    y = jnp.zeros((cap, wd_ref.shape[2]), F32)
    for f in range(FF // tf):
        hg = _dot(x, wg_ref[0, :, f * tf:(f + 1) * tf])
        hu = _dot(x, wu_ref[0, :, f * tf:(f + 1) * tf])
        hid = (hg * _sigmoid(hg) * hu).astype(BF16)
        y = y + _dot(hid, wdb_ref[f * tf:(f + 1) * tf, :])
    m = meta_ref[0, 0]
    g_row = m[0:1] + m[1:2] + m[2:3]
    eye = lax.broadcasted_iota(jnp.int32, (cap, cap), 0) == lax.broadcasted_iota(jnp.int32, (cap, cap), 1)
    g_col = jnp.sum(jnp.where(eye, g_row, 0.0), axis=1, keepdims=True)
    y_ref[0, 0] = (y * g_col).astype(BF16)


def _expert_ffn(xs, meta, w_gate, w_up, w_down, tf=512):
    B, E, cap, D = xs.shape
    FF = w_gate.shape[2]
    return pl.pallas_call(
        functools.partial(_ffn_kernel, tf=tf),
        grid=(E, B),
        in_specs=[pl.BlockSpec((1, 1, cap, D), lambda e, b: (b, e, 0, 0)),
                  pl.BlockSpec((1, 1, META_ROWS, cap), lambda e, b: (b, e, 0, 0)),
                  pl.BlockSpec((1, D, FF), lambda e, b: (e, 0, 0)),
                  pl.BlockSpec((1, D, FF), lambda e, b: (e, 0, 0)),
                  pl.BlockSpec((1, FF, D), lambda e, b: (e, 0, 0))],
        out_specs=pl.BlockSpec((1, 1, cap, D), lambda e, b: (b, e, 0, 0)),
        out_shape=jax.ShapeDtypeStruct((B, E, cap, D), BF16),
        scratch_shapes=[pltpu.VMEM((FF, D), BF16)],
        compiler_params=_cparams(("parallel", "arbitrary")),
        name="moe_ffn",
    )(xs, meta, w_gate.astype(BF16), w_up.astype(BF16), w_down)


def _scatter_kernel(lo_ref, n_ref, y_ref, meta_ref, x1_ref, gt_ref, g_ref, o_ref, acc_ref):
    E = y_ref.shape[1]
    tt, R = SCATTER_TOKENS, SCATTER_SLOTS
    tile = pl.program_id(0) * pl.num_programs(1) + pl.program_id(1)
    tok = pl.program_id(1) * tt + lax.broadcasted_iota(jnp.int32, (tt, R), 0)
    acc_ref[...] = jnp.zeros_like(acc_ref)

    def expert_body(e, carry):
        first = lo_ref[tile * E + e]

        def block_body(k, carry):
            r0 = pl.multiple_of((first + k) * R, R)
            m = meta_ref[0, e, :, pl.ds(r0, R)]
            idx = (m[3:4] * 64.0 + m[4:5]).astype(jnp.int32)
            onehot = jnp.where(tok == idx, 1.0, 0.0).astype(BF16)
            acc_ref[...] += _dot(onehot, y_ref[0, e, pl.ds(r0, R), :])
            return carry

        return lax.fori_loop(0, n_ref[tile * E + e], block_body, carry)

    lax.fori_loop(0, E, expert_body, 0)
    o_ref[0] = _rms(x1_ref[0] + gt_ref[0] * acc_ref[...], g_ref[...])


def _scatter_final(s_lo, s_n, y, meta, x1, gt2, g_final):
    B, E, cap, D = y.shape
    S = x1.shape[1]
    tt = SCATTER_TOKENS
    tile = pl.BlockSpec((1, tt, D), lambda b, i, *_: (b, i, 0))
    return pl.pallas_call(
        _scatter_kernel,
        grid_spec=pltpu.PrefetchScalarGridSpec(
            num_scalar_prefetch=2,
            grid=(B, S // tt),
            in_specs=[pl.BlockSpec((1, E, cap, D), lambda b, i, *_: (b, 0, 0, 0)),
                      pl.BlockSpec((1, E, META_ROWS, cap), lambda b, i, *_: (b, 0, 0, 0)),
                      tile, pl.BlockSpec((1, 1, D), lambda b, i, *_: (b, 0, 0)),
                      pl.BlockSpec((1, D), lambda b, i, *_: (0, 0))],
            out_specs=tile,
            scratch_shapes=[pltpu.VMEM((tt, D), F32)]),
        out_shape=jax.ShapeDtypeStruct((B, S, D), F32),
        compiler_params=_cparams(("parallel", "parallel")),
        name="moe_scatter_final",
    )(s_lo, s_n, y, meta, x1, gt2.reshape(B, 1, D), g_final.reshape(1, D))


def kernel(x, c, w_ada, b_ada, g_mix, w_in, b_in, conv_qk_w, conv_qk_b, mlstm_norm_g, conv_hy_w, conv_hy_b,
           hy_w1, hy_b1, hy_w2, hy_b2, hy_w3, hy_freq, hy_deltas, hy_bias, hyena_norm_g, w_out, g_ffn,
           w_router, w_gate, w_up, w_down, g_final):
    S = x.shape[1]
    cap = EC_CAPACITY_FACTOR * S // N_EXPERTS
    assert w_ada.shape[0] == 1, "single-layer block: the final RMSNorm is fused into the MoE scatter"
    l = 0
    mod = _modulation(c, w_ada[l], b_ada[l])
    sh1, sc1, gt1, sh2, sc2, gt2 = jnp.split(mod, 6, axis=-1)
    qkvo, hx, gates_t = _in_projection(x, sc1, sh1, g_mix[l], w_in[l], b_in[l])
    y_m = _mlstm(qkvo, gates_t, conv_qk_w[l], conv_qk_b[l], mlstm_norm_g[l])
    y_h = _hyena(hx, conv_hy_w[l], conv_hy_b[l], hy_w1[l], hy_b1[l], hy_w2[l], hy_b2[l], hy_w3[l],
                 hy_freq[l], hy_deltas[l], hy_bias[l], hyena_norm_g[l])
    x1, hf, aff_t = _out_projection(y_m, y_h, x, gt1, sc2, sh2, g_ffn[l], w_out[l], w_router[l])
    pos, cnt = _route(aff_t, cap)
    g_lo, g_hi, s_lo, s_n = _window_tables(cnt, S, cap)
    xs, meta = _gather(g_lo, g_hi, pos, aff_t, hf, cap)
    y = _expert_ffn(xs, meta, w_gate[l], w_up[l], w_down[l])
    return _scatter_final(s_lo, s_n, y, meta, x1, gt2, g_final)
```

```python
import functools
import math

import numpy as np
import jax
import jax.numpy as jnp
from jax import lax
from jax.experimental import pallas as pl
from jax.experimental.pallas import tpu as pltpu

F32 = jnp.float32
BF16 = jnp.bfloat16

MLSTM_HEADS = 4
HEAD_DIM = 128
MLSTM_CHUNK = 128
MLSTM_M_INIT = -1e30
HYENA_GROUP_DIM = 128
HYENA_ORDER = 2
HYENA_BANDS = 16
HYENA_WINDOW_SHIFT = 0.05
N_GATE_COLS = 4 * MLSTM_HEADS
N_EXPERTS = 16
EC_CAPACITY_FACTOR = 2
RMS_EPS = 1e-6

LANES = 128
SUBLANES = 8
FFT_N2 = 64
FFT_PITCH = 72
NEG_BIG = -1e30
VMEM_LIMIT = 56 * 1024 * 1024


def _cparams(sem, vmem=None, flags=None):
    return pltpu.CompilerParams(dimension_semantics=sem, vmem_limit_bytes=vmem or VMEM_LIMIT, flags=flags)


def _split(a):
    hi = a.astype(BF16)
    lo = (a - hi.astype(F32)).astype(BF16)
    return hi, lo


def _dot(a, b):
    return jnp.dot(a, b, preferred_element_type=F32)


def _dot_nt(a, b):
    return lax.dot_general(a, b, (((1,), (1,)), ((), ())), preferred_element_type=F32)


def _dot3(a, b):
    ah, al = _split(a)
    bh, bl = _split(b)
    return _dot(ah, bh) + _dot(ah, bl) + _dot(al, bh)


def _dot3_nt(a, b):
    ah, al = _split(a)
    bh, bl = _split(b)
    return _dot_nt(ah, bh) + _dot_nt(ah, bl) + _dot_nt(al, bh)


def _dotc(w_hi, w_lo, x, passes):
    xh, xl = _split(x)
    out = _dot(w_hi, xh)
    if passes == 3:
        out = out + _dot(w_hi, xl) + _dot(w_lo, xh)
    return out


def _rms(x, g):
    return x * lax.rsqrt(jnp.mean(x * x, axis=-1, keepdims=True) + RMS_EPS) * g


def _sigmoid(x):
    return 1.0 / (1.0 + jnp.exp(-x))


def _log_sigmoid(x):
    return jnp.minimum(x, 0.0) - jnp.log(1.0 + jnp.exp(-jnp.abs(x)))


def _mod_kernel(c_ref, w_ref, b_ref, o_ref):
    o_ref[...] = _dot3(c_ref[...], w_ref[...]) + b_ref[...]


def _modulation(c, w_ada, b_ada):
    B, D = c.shape
    n = w_ada.shape[1]
    tn = 768
    return pl.pallas_call(
        _mod_kernel,
        grid=(n // tn,),
        in_specs=[pl.BlockSpec((B, D), lambda j: (0, 0)),
                  pl.BlockSpec((D, tn), lambda j: (0, j)),
                  pl.BlockSpec((1, tn), lambda j: (0, j))],
        out_specs=pl.BlockSpec((B, tn), lambda j: (0, j)),
        out_shape=jax.ShapeDtypeStruct((B, n), F32),
        compiler_params=_cparams(("parallel",)),
        name="adaln_mod",
    )(c, w_ada, b_ada.reshape(1, n))


def _inproj_kernel(x_ref, sc_ref, sh_ref, g_ref, wq_ref, bq_ref, wh_ref, bh_ref, wgt_ref, bgt_ref,
                   qkvo_ref, hx_ref, gt_ref):
    x = x_ref[0]
    h = _rms(x, g_ref[...]) * (1.0 + sc_ref[0]) + sh_ref[0]
    hb = h.astype(BF16)
    qkvo_ref[0] = _dot(hb, wq_ref[...]) + bq_ref[...]
    hx_ref[0] = _dot(hb, wh_ref[...]) + bh_ref[...]
    gt_ref[0] = _dot3_nt(wgt_ref[...], h) + bgt_ref[...]


def _in_projection(x, sc1, sh1, g_mix, w_in, b_in, ts=512):
    B, S, D = x.shape
    nq = 4 * MLSTM_HEADS * HEAD_DIM
    nh = w_in.shape[1] - nq - N_GATE_COLS
    wq = w_in[:, :nq].astype(BF16)
    wh = w_in[:, nq + N_GATE_COLS:].astype(BF16)
    bq = b_in[:nq].reshape(1, nq)
    bh = b_in[nq + N_GATE_COLS:].reshape(1, nh)
    ng = MLSTM_HEADS * SUBLANES
    pad = lambda t: jnp.pad(t.reshape(4, MLSTM_HEADS, -1).transpose(1, 0, 2), ((0, 0), (0, 4), (0, 0))).reshape(ng, -1)
    wgt = pad(w_in[:, nq:nq + N_GATE_COLS].T)
    bgt = pad(b_in[nq:nq + N_GATE_COLS].reshape(N_GATE_COLS, 1))
    const = lambda b, i: (0, 0)
    return pl.pallas_call(
        _inproj_kernel,
        grid=(B, S // ts),
        in_specs=[pl.BlockSpec((1, ts, D), lambda b, i: (b, i, 0)),
                  pl.BlockSpec((1, 1, D), lambda b, i: (b, 0, 0)),
                  pl.BlockSpec((1, 1, D), lambda b, i: (b, 0, 0)),
                  pl.BlockSpec((1, D), const),
                  pl.BlockSpec((D, nq), const), pl.BlockSpec((1, nq), const),
                  pl.BlockSpec((D, nh), const), pl.BlockSpec((1, nh), const),
                  pl.BlockSpec((ng, D), const), pl.BlockSpec((ng, 1), const)],
        out_specs=[pl.BlockSpec((1, ts, nq), lambda b, i: (b, i, 0)),
                   pl.BlockSpec((1, ts, nh), lambda b, i: (b, i, 0)),
                   pl.BlockSpec((1, ng, ts), lambda b, i: (b, 0, i))],
        out_shape=[jax.ShapeDtypeStruct((B, S, nq), F32),
                   jax.ShapeDtypeStruct((B, S, nh), F32),
                   jax.ShapeDtypeStruct((B, ng, S), F32)],
        compiler_params=_cparams(("parallel", "parallel")),
        name="in_proj",
    )(x, sc1.reshape(B, 1, D), sh1.reshape(B, 1, D), g_mix.reshape(1, D), wq, bq, wh, bh, wgt, bgt)


def _short_conv_chunk(src_ref, lead, c, nc, w_ref, b_ref, rows):
    L = rows.shape[0]
    S = nc * L
    t0 = pl.multiple_of(c * L, L)
    cur = src_ref[lead, pl.ds(t0, L), :]
    p0 = pl.multiple_of(jnp.maximum(t0 - SUBLANES, 0), SUBLANES)
    n0 = pl.multiple_of(jnp.minimum(t0 + L, S - SUBLANES), SUBLANES)
    prev_row = src_ref[lead, pl.ds(p0, SUBLANES), :][SUBLANES - 1:SUBLANES, :]
    next_row = src_ref[lead, pl.ds(n0, SUBLANES), :][0:1, :]
    prev_row = jnp.where(c > 0, prev_row, 0.0)
    next_row = jnp.where(c < nc - 1, next_row, 0.0)
    up = jnp.where(rows == 0, prev_row, pltpu.roll(cur, 1, 0))
    dn = jnp.where(rows == L - 1, next_row, pltpu.roll(cur, L - 1, 0))
    return b_ref[...] + up * w_ref[0:1, :] + cur * w_ref[1:2, :] + dn * w_ref[2:3, :]


def _mlstm_kernel(q_ref, k_ref, v_ref, o_ref, gt_ref, wq_ref, wk_ref, bq_ref, bk_ref, ng_ref,
                  y_ref, qt_ref, kb_ref, vt_ref, c_ref, cl_ref, ld_ref, st_ref, pc_ref, cs_ref, kq_ref, *, nc):
    L = MLSTM_CHUNK
    rows = lax.broadcasted_iota(jnp.int32, (L, LANES), 0)
    cols = lax.broadcasted_iota(jnp.int32, (L, LANES), 1)
    pad = 2 * SUBLANES
    piece_row = lax.broadcasted_iota(jnp.int32, (SUBLANES, LANES), 0)
    k_scale = HEAD_DIM ** -0.5

    def conv_body(c, carry):
        t0 = pl.multiple_of(c * L, L)
        yq = _short_conv_chunk(q_ref, 0, c, nc, wq_ref, bq_ref, rows)
        qt_ref[c] = (yq * _sigmoid(yq)).T.astype(BF16)
        yk = _short_conv_chunk(k_ref, 0, c, nc, wk_ref, bk_ref, rows)
        kb_ref[pl.ds(t0, L), :] = (yk * _sigmoid(yk) * k_scale).astype(BF16)
        vt_ref[c] = v_ref[0, pl.ds(t0, L), :].T
        gates = gt_ref[0, 0, :, pl.ds(t0, L)]
        lf = _log_sigmoid(jnp.where(piece_row < 3, gates[1:2], gates[3:4]))
        hi = lf.astype(BF16).astype(F32)
        mid = (lf - hi).astype(BF16).astype(F32)
        kind = piece_row % 3
        pc_ref[pl.ds(pl.multiple_of(c * SUBLANES, SUBLANES), SUBLANES), :] = jnp.where(
            piece_row >= 6, 0.0, jnp.where(kind == 0, hi, jnp.where(kind == 1, mid, lf - hi - mid)))
        return carry

    lax.fori_loop(0, nc, conv_body, 0, unroll=2)

    pieces = pc_ref[...].astype(BF16)
    fwd_rows = lax.broadcasted_iota(jnp.int32, pc_ref.shape, 0) % SUBLANES < 3
    cs_ref[...] = jnp.where(fwd_rows, _dot(pieces, (rows <= cols).astype(BF16)),
                            _dot(pieces, (rows >= cols).astype(BF16)))

    ROW_B, ROW_MAX, ROW_G, ROW_A, ROW_N, ROW_M = range(6)

    def stat_tile(rows_by_index, base=None):
        tile = jnp.zeros((SUBLANES, LANES), F32) if base is None else base
        for r, value in rows_by_index.items():
            tile = jnp.where(piece_row == r, value, tile)
        return tile

    def local_pass(c, rev):
        t0 = pl.multiple_of(c * L, L)
        slot = 2 * c + (1 if rev else 0)
        kb = kb_ref[pl.ds(t0, L), :]
        gates = gt_ref[0, 0, :, pl.ds(t0, L)]
        i_row = gates[(2 if rev else 0):(3 if rev else 1)]
        cs = cs_ref[pl.ds(pl.multiple_of(c * SUBLANES, SUBLANES), SUBLANES), :]
        o = 3 if rev else 0
        b_row = cs[o:o + 1] + cs[o + 1:o + 2] + cs[o + 2:o + 3]
        keep = (rows >= cols) if rev else (rows <= cols)
        g = b_row[:, 0:1] if rev else b_row[:, L - 1:L]
        w_tile = jnp.broadcast_to(i_row - b_row, (L, LANES)).T
        log_d = jnp.where(keep, w_tile + b_row, NEG_BIG)
        ld_ref[slot] = log_d
        w_end = g - b_row + i_row
        a = jnp.max(w_end, axis=1, keepdims=True)
        e_end = jnp.exp(w_end - a)
        ve = jnp.concatenate([vt_ref[c] * e_end, jnp.broadcast_to(e_end, (pad, L))], axis=0).astype(BF16)
        cn = _dot(ve, kb)
        cl_ref[slot] = cn[:HEAD_DIM]
        st_ref[slot] = stat_tile({ROW_B: b_row, ROW_MAX: jnp.max(log_d, axis=0, keepdims=True), ROW_G: g,
                                  ROW_A: a, ROW_N: cn[HEAD_DIM:HEAD_DIM + 1]})

    def local_body(c, carry):
        kq_ref[c] = _dot(kb_ref[pl.ds(pl.multiple_of(c * L, L), L), :], qt_ref[c])
        local_pass(c, False)
        local_pass(c, True)
        return carry

    lax.fori_loop(0, nc, local_body, 0, unroll=8)

    def scan_step(c, rev, n, m):
        slot = 2 * c + (1 if rev else 0)
        side = 1 if rev else 0
        stats = st_ref[slot]
        g = stats[ROW_G:ROW_G + 1]
        a = stats[ROW_A:ROW_A + 1]
        m_new = jnp.maximum(g + m, a)
        s_prev = jnp.exp(g + m - m_new)
        s_loc = jnp.exp(a - m_new)
        c_in = c_ref[side]
        c_ref[side] = s_prev * c_in + s_loc * cl_ref[slot]
        cl_ref[slot] = c_in
        n_new = s_prev * n + s_loc * stats[ROW_N:ROW_N + 1]
        st_ref[slot] = stat_tile({ROW_N: n, ROW_M: m}, base=stats)
        return n_new, m_new

    c_ref[...] = jnp.zeros_like(c_ref)
    n0 = jnp.zeros((1, HEAD_DIM), F32)
    m0 = jnp.full((1, LANES), MLSTM_M_INIT, F32)

    def scan_body(j, carry):
        n_f, m_f, n_b, m_b = carry
        n_f, m_f = scan_step(j, False, n_f, m_f)
        n_b, m_b = scan_step(nc - 1 - j, True, n_b, m_b)
        return n_f, m_f, n_b, m_b

    lax.fori_loop(0, nc, scan_body, (n0, m0, n0, m0))

    def output_pass(c, rev):
        t0 = pl.multiple_of(c * L, L)
        slot = 2 * c + (1 if rev else 0)
        qt = qt_ref[c]
        stats = st_ref[slot]
        log_inter = stats[ROW_B:ROW_B + 1] + stats[ROW_M:ROW_M + 1]
        m_t = jnp.maximum(log_inter, stats[ROW_MAX:ROW_MAX + 1])
        d = jnp.exp(ld_ref[slot] - m_t)
        e_inter = jnp.exp(log_inter - m_t)
        s = (kq_ref[c] * d).astype(BF16)
        vn = _dot(jnp.concatenate([vt_ref[c].astype(BF16), jnp.ones((pad, L), BF16)], axis=0), s)
        n_in = stats[ROW_N:ROW_N + 1]
        cq = _dot(jnp.concatenate([cl_ref[slot], jnp.broadcast_to(n_in, (pad, LANES))], axis=0).astype(BF16), qt)
        den = vn[HEAD_DIM:HEAD_DIM + 1] + e_inter * cq[HEAD_DIM:HEAD_DIM + 1]
        scale = 1.0 / jnp.maximum(jnp.abs(den), jnp.exp(-m_t))
        return ((vn[:HEAD_DIM] + e_inter * cq[:HEAD_DIM]) * scale).T

    def output_body(c, carry):
        t0 = pl.multiple_of(c * L, L)
        hs = output_pass(c, False) + output_pass(c, True)
        y_ref[0, pl.ds(t0, L), :] = _sigmoid(o_ref[0, pl.ds(t0, L), :]) * _rms(hs, ng_ref[...])
        return carry

    lax.fori_loop(0, nc, output_body, 0, unroll=8)


def _mlstm(qkvo, gates_t, conv_w, conv_b, norm_g):
    B, S, _ = qkvo.shape
    H, Dh = MLSTM_HEADS, HEAD_DIM
    nc = S // MLSTM_CHUNK
    seq = lambda off: pl.BlockSpec((1, S, Dh), lambda b, h: (b, 0, off + h))
    vec = lambda r, off: pl.BlockSpec((r, Dh), lambda b, h: (0, off + h))
    return pl.pallas_call(
        functools.partial(_mlstm_kernel, nc=nc),
        grid=(B, H),
        in_specs=[seq(0), seq(H), seq(2 * H), seq(3 * H),
                  pl.BlockSpec((1, 1, SUBLANES, S), lambda b, h: (b, h, 0, 0)),
                  vec(3, 0), vec(3, H), vec(1, 0), vec(1, H), vec(1, 0)],
        out_specs=pl.BlockSpec((1, S, Dh), lambda b, h: (b, 0, h)),
        out_shape=jax.ShapeDtypeStruct((B, S, H * Dh), F32),
        scratch_shapes=[pltpu.VMEM((nc, Dh, MLSTM_CHUNK), BF16), pltpu.VMEM((S, Dh), BF16),
                        pltpu.VMEM((nc, Dh, MLSTM_CHUNK), F32), pltpu.VMEM((2, Dh, Dh), F32),
                        pltpu.VMEM((2 * nc, Dh, Dh), F32), pltpu.VMEM((2 * nc, MLSTM_CHUNK, MLSTM_CHUNK), F32),
                        pltpu.VMEM((2 * nc, SUBLANES, LANES), F32), pltpu.VMEM((nc * SUBLANES, LANES), F32),
                        pltpu.VMEM((nc * SUBLANES, LANES), F32), pltpu.VMEM((nc, MLSTM_CHUNK, MLSTM_CHUNK), F32)],
        compiler_params=_cparams(("parallel", "arbitrary")),
        name="mlstm",
    )(qkvo, qkvo, qkvo, qkvo, gates_t.reshape(B, H, SUBLANES, S), conv_w, conv_w, conv_b.reshape(1, -1), conv_b.reshape(1, -1),
      norm_g.reshape(1, -1))


FILTER_PASSES = 3
FFT_UNROLL = 16


def _hilo(m):
    m32 = jnp.asarray(m, F32)
    hi = m32.astype(BF16)
    return hi, (m32 - hi.astype(F32)).astype(BF16)


def _stack_complex(m):
    return np.block([[m.real, -m.imag], [m.imag, m.real]])


def _dft_constants(S):
    N = 2 * S
    N2 = FFT_N2
    N1 = N // N2
    k1 = np.arange(N1)
    n2 = np.arange(N2)
    f1 = np.exp(-2j * np.pi * np.outer(k1, np.arange(N1)) / N1)
    f2 = np.exp(-2j * np.pi * np.outer(n2, n2) / N2)
    tw = np.exp(-2j * np.pi * np.outer(k1, n2) / N)
    g1 = np.conj(f1).T[:N1 // 2] / N
    consts = dict(
        f1_real=_hilo(np.concatenate([f1.real, f1.imag], axis=0)),
        f1_half=_hilo(_stack_complex(f1[:, :N1 // 2])),
        f2=_hilo(_stack_complex(f2)),
        g1=_hilo(_stack_complex(g1)),
        f2_tw=_hilo(np.stack([_stack_complex(f2 * tw[k][None, :]) for k in k1]))[0],
        f2_tw_inv=_hilo(np.stack([_stack_complex(np.conj(tw[k])[:, None] * np.conj(f2)) for k in k1]))[0],
        tw_re=jnp.asarray(np.broadcast_to(tw.real[:, :, None], (N1, N2, LANES)), F32),
        tw_im=jnp.asarray(np.broadcast_to(tw.imag[:, :, None], (N1, N2, LANES)), F32),
    )
    return N1, N2, consts


def _hy_hidden_kernel(w1t_ref, w1c_ref, w1s_ref, b1_ref, w2_ref, b2_ref, fr_ref, o_ref, *, S, T):
    j = pl.program_id(0) * T + lax.broadcasted_iota(jnp.int32, (T, 1), 0)
    p = jnp.where(j < S, j, 2 * S - j).astype(F32)
    t = p / (S - 1)
    w = (2.0 * math.pi) * p / S
    band = lax.broadcasted_iota(jnp.int32, (1, HYENA_BANDS), 1).astype(F32)
    bands = 1e-4 + band * ((HYENA_BANDS - 1 - 1e-4) / (HYENA_BANDS - 1))
    arg = bands * w
    pre = t * w1t_ref[...] + _dot3(jnp.cos(arg), w1c_ref[...]) + _dot3(-jnp.sin(arg), w1s_ref[...]) + b1_ref[...]
    hid = jnp.sin(fr_ref[...] * pre)
    o_ref[...] = jnp.sin(fr_ref[...] * (_dot3(hid, w2_ref[...]) + b2_ref[...]))


def _hy_hidden(S, w1, b1, w2, b2, freq):
    N = 2 * S
    T = 1024
    Hd = w2.shape[0]
    full = lambda a: pl.BlockSpec(a.shape, lambda i: (0,) * a.ndim)
    args = (w1[0:1], w1[1:1 + HYENA_BANDS], w1[1 + HYENA_BANDS:], b1.reshape(1, Hd), w2, b2.reshape(1, Hd),
            freq.reshape(1, Hd))
    return pl.pallas_call(
        functools.partial(_hy_hidden_kernel, S=S, T=T),
        grid=(N // T,),
        in_specs=[full(a) for a in args],
        out_specs=pl.BlockSpec((T, Hd), lambda i: (i, 0)),
        out_shape=jax.ShapeDtypeStruct((N, Hd), F32),
        compiler_params=_cparams(("parallel",)),
        name="hyena_hidden",
    )(*args)


def _fft_stage2(a_re, a_im, k1, twr_ref, twi_ref, f2_hi, f2_lo, passes):
    r0 = pl.multiple_of(k1 * FFT_PITCH, SUBLANES)
    ar = a_re[pl.ds(r0, FFT_N2), :]
    ai = a_im[pl.ds(r0, FFT_N2), :]
    twr = twr_ref[k1]
    twi = twi_ref[k1]
    t = jnp.concatenate([ar * twr - ai * twi, ar * twi + ai * twr], axis=0)
    return _dotc(f2_hi, f2_lo, t, passes)


def _hy_filter_kernel(hid_ref, w3f_ref, w3b_ref, df_ref, db_ref, bias_ref, f1h_ref, f1l_ref, f2h_ref, f2l_ref,
                      twr_ref, twi_ref, kr_ref, ki_ref, u_ref, a_re, a_im, *, S, N1):
    N2, P = FFT_N2, FFT_PITCH
    T = 512
    nb = T // N2

    def gen_body(i, carry):
        r0 = pl.multiple_of(i * T, T)
        hid = hid_ref[pl.ds(r0, T), :]
        j = r0 + lax.broadcasted_iota(jnp.int32, (T, 1), 0)
        t = jnp.where(j < S, j, 2 * S - j).astype(F32) / (S - 1)
        kf = _dot3(hid, w3f_ref[...]) * (jnp.exp(-t * jnp.abs(df_ref[...])) + HYENA_WINDOW_SHIFT)
        kb = _dot3(hid, w3b_ref[...]) * (jnp.exp(-t * jnp.abs(db_ref[...])) + HYENA_WINDOW_SHIFT)
        ker = (jnp.where(j < S, kf, 0.0) + jnp.where((j > S) | (j == 0), kb, 0.0)
               + jnp.where(j == 0, bias_ref[0], 0.0))
        for b in range(nb):
            u_ref[pl.ds(pl.multiple_of((i * nb + b) * P, SUBLANES), N2), :] = ker[b * N2:(b + 1) * N2]
        return carry

    lax.fori_loop(0, 2 * S // T, gen_body, 0)

    def stage1(n2, carry):
        x = u_ref[pl.ds(n2, N1, stride=P), :]
        a = _dotc(f1h_ref[...], f1l_ref[...], x, FILTER_PASSES)
        a_re[pl.ds(n2, N1, stride=P), :] = a[:N1]
        a_im[pl.ds(n2, N1, stride=P), :] = a[N1:]
        return carry

    lax.fori_loop(0, N2, stage1, 0, unroll=FFT_UNROLL)

    def stage2(k1, carry):
        x = _fft_stage2(a_re, a_im, k1, twr_ref, twi_ref, f2h_ref[...], f2l_ref[...], FILTER_PASSES)
        r0 = pl.multiple_of(k1 * N2, N2)
        kr_ref[0, pl.ds(r0, N2), :] = x[:N2]
        ki_ref[0, pl.ds(r0, N2), :] = x[N2:]
        return carry

    lax.fori_loop(0, N1, stage2, 0, unroll=2 * FFT_UNROLL)


def _hy_filter_spectrum(S, hid, w3, deltas, bias, N1, consts):
    N = 2 * S
    N2, P = FFT_N2, FFT_PITCH
    Hd = hid.shape[1]
    nblk = w3.shape[1] // (2 * HYENA_ORDER * LANES)
    f1h, f1l = consts["f1_real"]
    f2h, f2l = consts["f2"]
    full = lambda a: pl.BlockSpec(a.shape, lambda o, j: (0,) * a.ndim, pipeline_mode=pl.Buffered(1))
    colf = lambda o, j: (0, (2 * o) * nblk + j)
    colb = lambda o, j: (0, (2 * o + 1) * nblk + j)
    out = jax.ShapeDtypeStruct((HYENA_ORDER, N, nblk * LANES), F32)
    return pl.pallas_call(
        functools.partial(_hy_filter_kernel, S=S, N1=N1),
        grid=(HYENA_ORDER, nblk),
        in_specs=[full(hid),
                  pl.BlockSpec((Hd, LANES), colf), pl.BlockSpec((Hd, LANES), colb),
                  pl.BlockSpec((1, LANES), colf), pl.BlockSpec((1, LANES), colb),
                  pl.BlockSpec((1, 1, LANES), lambda o, j: (o, 0, j)),
                  full(f1h), full(f1l), full(f2h), full(f2l), full(consts["tw_re"]), full(consts["tw_im"])],
        out_specs=[pl.BlockSpec((1, N, LANES), lambda o, j: (o, 0, j))] * 2,
        out_shape=[out, out],
        scratch_shapes=[pltpu.VMEM((N1 * P, LANES), F32)] * 3,
        compiler_params=_cparams(("parallel", "parallel")),
        name="hyena_filter_spectrum",
    )(hid, w3, w3, deltas.reshape(1, -1), deltas.reshape(1, -1), bias.reshape(HYENA_ORDER, 1, -1),
      f1h, f1l, f2h, f2l, consts["tw_re"], consts["tw_im"])


def _hy_conv_kernel(u_ref, x_ref, kr_ref, ki_ref, wu_ref, bu_ref, wx_ref, bx_ref, ng_ref,
                    f1_ref, g1_ref, mf_ref, mi_ref, o_ref, u_re, u_im, a_re, a_im, *, N1, conv_u, last):
    N2, P = FFT_N2, FFT_PITCH
    nh = N1 // 2
    rows = lax.broadcasted_iota(jnp.int32, (N2, LANES), 0)

    def load_body(c, carry):
        r0 = pl.multiple_of(c * P, SUBLANES)
        for b, dst in ((0, u_re), (1, u_im)):
            if conv_u:
                dst[pl.ds(r0, N2), :] = _short_conv_chunk(u_ref, b, c, nh, wu_ref, bu_ref, rows)
            else:
                dst[pl.ds(r0, N2), :] = u_ref[b, pl.ds(pl.multiple_of(c * N2, N2), N2), :]
        return carry

    lax.fori_loop(0, nh, load_body, 0, unroll=4)

    def stage1(n2, carry):
        x = jnp.concatenate([u_re[pl.ds(n2, nh, stride=P), :], u_im[pl.ds(n2, nh, stride=P), :]], axis=0)
        a = _dot(f1_ref[...], x.astype(BF16))
        a_re[pl.ds(n2, N1, stride=P), :] = a[:N1]
        a_im[pl.ds(n2, N1, stride=P), :] = a[N1:]
        return carry

    lax.fori_loop(0, N2, stage1, 0, unroll=FFT_UNROLL)

    def stage2(k1, carry):
        r0 = pl.multiple_of(k1 * P, SUBLANES)
        a = jnp.concatenate([a_re[pl.ds(r0, N2), :], a_im[pl.ds(r0, N2), :]], axis=0)
        x = _dot(mf_ref[k1], a.astype(BF16))
        s0 = pl.multiple_of(k1 * N2, N2)
        kr = kr_ref[0, pl.ds(s0, N2), :]
        ki = ki_ref[0, pl.ds(s0, N2), :]
        xr, xi = x[:N2], x[N2:]
        y = jnp.concatenate([xr * kr - xi * ki, xr * ki + xi * kr], axis=0)
        b = _dot(mi_ref[k1], y.astype(BF16))
        a_re[pl.ds(r0, N2), :] = b[:N2]
        a_im[pl.ds(r0, N2), :] = b[N2:]
        return carry

    lax.fori_loop(0, N1, stage2, 0, unroll=2 * FFT_UNROLL)

    def inv2(n2, carry):
        b = jnp.concatenate([a_re[pl.ds(n2, N1, stride=P), :], a_im[pl.ds(n2, N1, stride=P), :]], axis=0)
        y = _dot(g1_ref[...], b.astype(BF16))
        u_re[pl.ds(n2, nh, stride=P), :] = y[:nh]
        u_im[pl.ds(n2, nh, stride=P), :] = y[nh:]
        return carry

    lax.fori_loop(0, N2, inv2, 0, unroll=FFT_UNROLL)

    def out_body(c, carry):
        r0 = pl.multiple_of(c * P, SUBLANES)
        t0 = pl.multiple_of(c * N2, N2)
        for b, src in ((0, u_re), (1, u_im)):
            z = _short_conv_chunk(x_ref, b, c, nh, wx_ref, bx_ref, rows) * src[pl.ds(r0, N2), :]
            o_ref[b, pl.ds(t0, N2), :] = _rms(z, ng_ref[...]) if last else z
        return carry

    lax.fori_loop(0, nh, out_body, 0, unroll=4)


def _hy_conv(u, u_blk, x, x_blk, kr, ki, order, conv_w, conv_b, norm_g, N1, consts, conv_u, last):
    B, S, _ = u.shape
    N2, P = FFT_N2, FFT_PITCH
    N = 2 * S
    nblk = kr.shape[2] // LANES
    pair = 2
    single = pl.Buffered(1)
    full = lambda a: pl.BlockSpec(a.shape, lambda j, p: (0,) * a.ndim, pipeline_mode=single)
    seq = lambda off: pl.BlockSpec((pair, S, LANES), lambda j, p: (p, 0, off + j))
    vec = lambda r, off: pl.BlockSpec((r, LANES), lambda j, p: (0, off + j))
    spec = pl.BlockSpec((1, N, LANES), lambda j, p: (order, 0, j), pipeline_mode=single)
    mats = [consts["f1_half"][0], consts["g1"][0], consts["f2_tw"], consts["f2_tw_inv"]]
    return pl.pallas_call(
        functools.partial(_hy_conv_kernel, N1=N1, conv_u=conv_u, last=last),
        grid=(nblk, B // pair),
        in_specs=[seq(u_blk), seq(x_blk), spec, spec,
                  vec(3, u_blk if conv_u else 0), vec(1, u_blk if conv_u else 0), vec(3, x_blk), vec(1, x_blk),
                  pl.BlockSpec((1, LANES), lambda j, p: (0, j))] + [full(m) for m in mats],
        out_specs=pl.BlockSpec((pair, S, LANES), lambda j, p: (p, 0, j)),
        out_shape=jax.ShapeDtypeStruct((B, S, nblk * LANES), F32),
        scratch_shapes=[pltpu.VMEM((N1 // 2 * P, LANES), F32)] * 2 + [pltpu.VMEM((N1 * P, LANES), F32)] * 2,
        compiler_params=_cparams(("parallel", "parallel")),
        name=f"hyena_conv{order}",
    )(u, x, kr, ki, conv_w, conv_b.reshape(1, -1), conv_w, conv_b.reshape(1, -1), norm_g.reshape(1, -1), *mats)


def _hyena(hx, conv_w, conv_b, w1, b1, w2, b2, w3, freq, deltas, bias, norm_g):
    B, S, C3 = hx.shape
    nblk = C3 // (HYENA_ORDER + 1) // LANES
    N1, N2, consts = _dft_constants(S)
    hid = _hy_hidden(S, w1, b1, w2, b2, freq)
    kr, ki = _hy_filter_spectrum(S, hid, w3, deltas, bias, N1, consts)
    z1 = _hy_conv(hx, 2 * nblk, hx, 0, kr, ki, 0, conv_w, conv_b, norm_g, N1, consts, True, False)
    return _hy_conv(z1, 0, hx, nblk, kr, ki, 1, conv_w, conv_b, norm_g, N1, consts, False, True)


def _outproj_kernel(ym_ref, yh_ref, x_ref, gt_ref, sc_ref, sh_ref, g_ref, wm_ref, wh_ref, wr_ref,
                    x1_ref, hf_ref, aff_ref):
    mixed = _dot(ym_ref[0].astype(BF16), wm_ref[...]) + _dot(yh_ref[0].astype(BF16), wh_ref[...])
    x1 = x_ref[0] + gt_ref[0] * mixed
    x1_ref[0] = x1
    hf = _rms(x1, g_ref[...]) * (1.0 + sc_ref[0]) + sh_ref[0]
    hf_ref[0] = hf.astype(BF16)
    logits = _dot3_nt(wr_ref[...], hf)
    e = jnp.exp(logits - jnp.max(logits, axis=0, keepdims=True))
    aff_ref[0] = e / jnp.sum(e, axis=0, keepdims=True)


def _out_projection(y_m, y_h, x, gt1, sc2, sh2, g_ffn, w_out, w_router, ts=512):
    B, S, D = x.shape
    Wm = y_m.shape[2]
    E = w_router.shape[1]
    wm = w_out[:Wm].astype(BF16)
    wh = w_out[Wm:].astype(BF16)
    tile = lambda w: pl.BlockSpec((1, ts, w), lambda b, i: (b, i, 0))
    row = pl.BlockSpec((1, 1, D), lambda b, i: (b, 0, 0))
    full = lambda a: pl.BlockSpec(a.shape, lambda b, i: (0,) * a.ndim)
    wr = w_router.T
    g = g_ffn.reshape(1, D)
    return pl.pallas_call(
        _outproj_kernel,
        grid=(B, S // ts),
        in_specs=[tile(Wm), tile(y_h.shape[2]), tile(D), row, row, row, full(g), full(wm), full(wh), full(wr)],
        out_specs=[tile(D), tile(D), pl.BlockSpec((1, E, ts), lambda b, i: (b, 0, i))],
        out_shape=[jax.ShapeDtypeStruct((B, S, D), F32), jax.ShapeDtypeStruct((B, S, D), BF16),
                   jax.ShapeDtypeStruct((B, E, S), F32)],
        compiler_params=_cparams(("parallel", "parallel")),
        name="out_proj_router",
    )(y_m, y_h, x, gt1.reshape(B, 1, D), sc2.reshape(B, 1, D), sh2.reshape(B, 1, D), g, wm, wh, wr)


def _route_kernel(aff_ref, pos_ref, cnt_ref, *, S, cap):
    aff = aff_ref[0]
    E = aff.shape[0]
    tpos = lax.broadcasted_iota(jnp.int32, (E, S), 1)
    count = lambda mask: jnp.sum(jnp.where(mask, 1.0, 0.0), axis=1, keepdims=True)
    as_float = lambda word: lax.bitcast_convert_type(word, F32)

    def value_step(i, prefix):
        cand = prefix | jnp.left_shift(1, 30 - i)
        return jnp.where(count(aff >= as_float(cand)) >= cap, cand, prefix)

    thresh = as_float(lax.fori_loop(0, 31, value_step, jnp.zeros((E, 1), jnp.int32)))
    gt = aff > thresh
    eq = aff == thresh
    need = cap - count(gt)
    nbits = S.bit_length() - 1

    def index_step(i, x):
        cand = x | jnp.left_shift(1, nbits - 1 - i)
        return jnp.where(count(eq & (tpos < cand)) < need, cand, x)

    last_tie = lax.fori_loop(0, nbits, index_step, jnp.zeros((E, 1), jnp.int32))
    sel = jnp.where(gt | (eq & (tpos <= last_tie)), 1.0, 0.0)
    before = (lax.broadcasted_iota(jnp.int32, (LANES, LANES), 0)
              < lax.broadcasted_iota(jnp.int32, (LANES, LANES), 1)).astype(BF16)
    carry = jnp.zeros((E, 1), F32)
    lane = lax.broadcasted_iota(jnp.int32, (E, LANES), 1)
    starts = jnp.zeros((E, LANES), F32)
    for c in range(S // LANES):
        sc = sel[:, c * LANES:(c + 1) * LANES]
        rank = _dot(sc.astype(BF16), before) + carry
        pos_ref[0, :, c * LANES:(c + 1) * LANES] = jnp.where(sc > 0.0, rank, -1.0).astype(jnp.int32)
        starts = jnp.where(lane == c, carry, starts)
        carry = carry + jnp.sum(sc, axis=1, keepdims=True)
    cnt_ref[0] = jnp.where(lane >= S // LANES, carry, starts).astype(jnp.int32)


def _route(aff_t, cap):
    B, E, S = aff_t.shape
    assert S // LANES < LANES
    blk = pl.BlockSpec((1, E, S), lambda b: (b, 0, 0))
    return pl.pallas_call(
        functools.partial(_route_kernel, S=S, cap=cap),
        grid=(B,),
        in_specs=[blk],
        out_specs=[blk, pl.BlockSpec((1, E, LANES), lambda b: (b, 0, 0))],
        out_shape=[jax.ShapeDtypeStruct((B, E, S), jnp.int32), jax.ShapeDtypeStruct((B, E, LANES), jnp.int32)],
        compiler_params=_cparams(("parallel",)),
        name="route_topk",
    )(aff_t)


def _window_tables(cnt, S, cap):
    B, E, _ = cnt.shape
    nch = S // LANES
    bounds = cnt[:, :, 1:nch + 1]
    first = jnp.arange(cap // GATHER_SLOTS, dtype=jnp.int32) * GATHER_SLOTS
    chunk_of = lambda slot: jnp.sum((bounds[:, :, None, :] <= slot[None, None, :, None]).astype(jnp.int32), axis=-1)
    per = GATHER_ALIGN // LANES
    step = GATHER_TOKENS // GATHER_ALIGN
    lo_a = chunk_of(first) // per
    hi_a = chunk_of(first + GATHER_SLOTS - 1) // per
    g_hi = (hi_a - lo_a) // step + 1
    g_lo = jnp.minimum(lo_a, S // GATHER_ALIGN - step * g_hi)
    edges = cnt[:, :, 0:nch + 1:SCATTER_TOKENS // LANES]
    lo, hi = edges[:, :, :-1], edges[:, :, 1:]
    s_lo = jnp.swapaxes(lo // SCATTER_SLOTS, 1, 2)
    s_n = jnp.swapaxes(jnp.where(hi > lo, (hi - 1) // SCATTER_SLOTS - lo // SCATTER_SLOTS + 1, 0), 1, 2)
    ends = jnp.cumsum(s_n, axis=-1)
    j = jnp.arange(SCATTER_LIST, dtype=jnp.int32)
    e_j = jnp.minimum(jnp.sum((ends[:, :, None, :] <= j[None, None, :, None]).astype(jnp.int32), axis=-1), E - 1)
    pick = lambda t: jnp.take_along_axis(t, e_j, axis=-1)
    valid = j < ends[:, :, -1:]
    blk_j = jnp.where(valid, pick(s_lo) + j - pick(ends - s_n), 0)
    e_j = jnp.where(valid, e_j, 0)
    return (g_lo.reshape(-1), g_hi.reshape(-1), e_j.reshape(-1).astype(jnp.int32),
            blk_j.reshape(-1).astype(jnp.int32), ends[:, :, -1].reshape(-1).astype(jnp.int32))


META_ROWS = 16


GATHER_SLOTS = 128
GATHER_TOKENS = 512
GATHER_ALIGN = 256
SCATTER_SLOTS = 256
SCATTER_TOKENS = 512
SCATTER_LIST = N_EXPERTS * ((SCATTER_TOKENS - 1) // SCATTER_SLOTS + 2)


def _gather_kernel(lo_ref, hi_ref, pos_ref, aff_ref, hf_ref, xs_ref, meta_ref, acc_ref, macc_ref, *, cap):
    R, tk = GATHER_SLOTS, GATHER_TOKENS
    nblk = cap // R
    base = (pl.program_id(0) * pl.num_programs(1) + pl.program_id(1)) * nblk
    mrow = lax.broadcasted_iota(jnp.int32, (META_ROWS, tk), 0)
    lane = lax.broadcasted_iota(jnp.int32, (1, tk), 1)
    for j in range(nblk):
        slot = j * R + lax.broadcasted_iota(jnp.int32, (R, tk), 0)
        acc_ref[...] = jnp.zeros_like(acc_ref)
        macc_ref[...] = jnp.zeros_like(macc_ref)

        first = lo_ref[base + j]

        def body(w, carry):
            t0 = pl.multiple_of(first * GATHER_ALIGN + w * tk, GATHER_ALIGN)
            onehot = jnp.where(slot == pos_ref[0, 0, :, pl.ds(t0, tk)], 1.0, 0.0).astype(BF16)
            acc_ref[...] += _dot(onehot, hf_ref[0, pl.ds(t0, tk), :])
            a = aff_ref[0, 0, :, pl.ds(t0, tk)]
            hi = a.astype(BF16).astype(F32)
            mid = (a - hi).astype(BF16).astype(F32)
            lo = (a - hi - mid).astype(BF16).astype(F32)
            t = t0 + lane
            pieces = (hi, mid, lo, (t // 64).astype(F32), (t % 64).astype(F32))
            meta = jnp.zeros((META_ROWS, tk), F32)
            for r, piece in enumerate(pieces):
                meta = jnp.where(mrow == r, piece, meta)
            macc_ref[...] += _dot_nt(meta.astype(BF16), onehot)
            return carry

        lax.fori_loop(0, hi_ref[base + j], body, 0)
        xs_ref[0, 0, j * R:(j + 1) * R, :] = acc_ref[...].astype(BF16)
        meta_ref[0, 0, :, j * R:(j + 1) * R] = macc_ref[...]


def _gather(g_lo, g_hi, pos, aff_t, hf, cap):
    B, E, S = pos.shape
    D = hf.shape[2]
    row = pl.BlockSpec((1, 1, 1, S), lambda b, e, *_: (b, e, 0, 0))
    return pl.pallas_call(
        functools.partial(_gather_kernel, cap=cap),
        grid_spec=pltpu.PrefetchScalarGridSpec(
            num_scalar_prefetch=2,
            grid=(B, E),
            in_specs=[row, row, pl.BlockSpec((1, S, D), lambda b, e, *_: (b, 0, 0))],
            out_specs=[pl.BlockSpec((1, 1, cap, D), lambda b, e, *_: (b, e, 0, 0)),
                       pl.BlockSpec((1, 1, META_ROWS, cap), lambda b, e, *_: (b, e, 0, 0))],
            scratch_shapes=[pltpu.VMEM((GATHER_SLOTS, D), F32), pltpu.VMEM((META_ROWS, GATHER_SLOTS), F32)]),
        out_shape=[jax.ShapeDtypeStruct((B, E, cap, D), BF16), jax.ShapeDtypeStruct((B, E, META_ROWS, cap), F32)],
        compiler_params=_cparams(("parallel", "arbitrary")),
        name="moe_gather",
    )(g_lo, g_hi, pos.reshape(B, E, 1, S), aff_t.reshape(B, E, 1, S), hf)


def _ffn_kernel(xs_ref, meta_ref, wg_ref, wu_ref, wd_ref, y_ref, wdb_ref, *, tf):
    x = xs_ref[0, 0]
    cap = x.shape[0]
    FF = wg_ref.shape[2]

    @pl.when(pl.program_id(1) == 0)
    def _():
        for f in range(FF // tf):
            wdb_ref[f * tf:(f + 1) * tf, :] = wd_ref[0, f * tf:(f + 1) * tf, :].astype(BF16)

    y = jnp.zeros((cap, wd_ref.shape[2]), F32)
    for f in range(FF // tf):
        hg = _dot(x, wg_ref[0, :, f * tf:(f + 1) * tf])
        hu = _dot(x, wu_ref[0, :, f * tf:(f + 1) * tf])
        hid = (hg * _sigmoid(hg) * hu).astype(BF16)
        y = y + _dot(hid, wdb_ref[f * tf:(f + 1) * tf, :])
    m = meta_ref[0, 0]
    g_row = m[0:1] + m[1:2] + m[2:3]
    eye = lax.broadcasted_iota(jnp.int32, (cap, cap), 0) == lax.broadcasted_iota(jnp.int32, (cap, cap), 1)
    g_col = jnp.sum(jnp.where(eye, g_row, 0.0), axis=1, keepdims=True)
    y_ref[0, 0] = (y * g_col).astype(BF16)


def _expert_ffn(xs, meta, w_gate, w_up, w_down, tf=512):
    B, E, cap, D = xs.shape
    FF = w_gate.shape[2]
    return pl.pallas_call(
        functools.partial(_ffn_kernel, tf=tf),
        grid=(E, B),
        in_specs=[pl.BlockSpec((1, 1, cap, D), lambda e, b: (b, e, 0, 0)),
                  pl.BlockSpec((1, 1, META_ROWS, cap), lambda e, b: (b, e, 0, 0)),
                  pl.BlockSpec((1, D, FF), lambda e, b: (e, 0, 0)),
                  pl.BlockSpec((1, D, FF), lambda e, b: (e, 0, 0)),
                  pl.BlockSpec((1, FF, D), lambda e, b: (e, 0, 0))],
        out_specs=pl.BlockSpec((1, 1, cap, D), lambda e, b: (b, e, 0, 0)),
        out_shape=jax.ShapeDtypeStruct((B, E, cap, D), BF16),
        scratch_shapes=[pltpu.VMEM((FF, D), BF16)],
        compiler_params=_cparams(("parallel", "arbitrary")),
        name="moe_ffn",
    )(xs, meta, w_gate.astype(BF16), w_up.astype(BF16), w_down)


def _scatter_kernel(e_ref, blk_ref, cnt_ref, y_ref, meta_ref, x1_ref, gt_ref, g_ref, o_ref, acc_ref):
    tt, R = SCATTER_TOKENS, SCATTER_SLOTS
    tile = pl.program_id(0) * pl.num_programs(1) + pl.program_id(1)
    tok = pl.program_id(1) * tt + lax.broadcasted_iota(jnp.int32, (tt, R), 0)
    acc_ref[...] = jnp.zeros_like(acc_ref)
    count = cnt_ref[tile]

    def contribution(j):
        e = e_ref[tile * SCATTER_LIST + j]
        r0 = pl.multiple_of(blk_ref[tile * SCATTER_LIST + j] * R, R)
        m = meta_ref[0, e, :, pl.ds(r0, R)]
        idx = jnp.where(j < count, (m[3:4] * 64.0 + m[4:5]).astype(jnp.int32), -1)
        onehot = jnp.where(tok == idx, 1.0, 0.0).astype(BF16)
        return _dot(onehot, y_ref[0, e, pl.ds(r0, R), :])

    def pair_body(i, carry):
        acc_ref[...] += contribution(2 * i) + contribution(2 * i + 1)
        return carry

    lax.fori_loop(0, (count + 1) // 2, pair_body, 0)
    o_ref[0] = _rms(x1_ref[0] + gt_ref[0] * acc_ref[...], g_ref[...])


def _scatter_final(s_e, s_blk, s_cnt, y, meta, x1, gt2, g_final):
    B, E, cap, D = y.shape
    S = x1.shape[1]
    tt = SCATTER_TOKENS
    tile = pl.BlockSpec((1, tt, D), lambda b, i, *_: (b, i, 0))
    return pl.pallas_call(
        _scatter_kernel,
        grid_spec=pltpu.PrefetchScalarGridSpec(
            num_scalar_prefetch=3,
            grid=(B, S // tt),
            in_specs=[pl.BlockSpec((1, E, cap, D), lambda b, i, *_: (b, 0, 0, 0)),
                      pl.BlockSpec((1, E, META_ROWS, cap), lambda b, i, *_: (b, 0, 0, 0)),
                      tile, pl.BlockSpec((1, 1, D), lambda b, i, *_: (b, 0, 0)),
                      pl.BlockSpec((1, D), lambda b, i, *_: (0, 0))],
            out_specs=tile,
            scratch_shapes=[pltpu.VMEM((tt, D), F32)]),
        out_shape=jax.ShapeDtypeStruct((B, S, D), F32),
        compiler_params=_cparams(("parallel", "parallel")),
        name="moe_scatter_final",
    )(s_e, s_blk, s_cnt, y, meta, x1, gt2.reshape(B, 1, D), g_final.reshape(1, D))


def kernel(x, c, w_ada, b_ada, g_mix, w_in, b_in, conv_qk_w, conv_qk_b, mlstm_norm_g, conv_hy_w, conv_hy_b,
           hy_w1, hy_b1, hy_w2, hy_b2, hy_w3, hy_freq, hy_deltas, hy_bias, hyena_norm_g, w_out, g_ffn,
           w_router, w_gate, w_up, w_down, g_final):
    S = x.shape[1]
    cap = EC_CAPACITY_FACTOR * S // N_EXPERTS
    assert w_ada.shape[0] == 1, "single-layer block: the final RMSNorm is fused into the MoE scatter"
    l = 0
    mod = _modulation(c, w_ada[l], b_ada[l])
    sh1, sc1, gt1, sh2, sc2, gt2 = jnp.split(mod, 6, axis=-1)
    qkvo, hx, gates_t = _in_projection(x, sc1, sh1, g_mix[l], w_in[l], b_in[l])
    y_m = _mlstm(qkvo, gates_t, conv_qk_w[l], conv_qk_b[l], mlstm_norm_g[l])
    y_h = _hyena(hx, conv_hy_w[l], conv_hy_b[l], hy_w1[l], hy_b1[l], hy_w2[l], hy_b2[l], hy_w3[l],
                 hy_freq[l], hy_deltas[l], hy_bias[l], hyena_norm_g[l])
    x1, hf, aff_t = _out_projection(y_m, y_h, x, gt1, sc2, sh2, g_ffn[l], w_out[l], w_router[l])
    pos, cnt = _route(aff_t, cap)
    g_lo, g_hi, s_e, s_blk, s_cnt = _window_tables(cnt, S, cap)
    xs, meta = _gather(g_lo, g_hi, pos, aff_t, hf, cap)
    y = _expert_ffn(xs, meta, w_gate[l], w_up[l], w_down[l])
    return _scatter_final(s_e, s_blk, s_cnt, y, meta, x1, gt2, g_final)
```

```python
import functools
import math

import numpy as np
import jax
import jax.numpy as jnp
from jax import lax
from jax.experimental import pallas as pl
from jax.experimental.pallas import tpu as pltpu

F32 = jnp.float32
BF16 = jnp.bfloat16

MLSTM_HEADS = 4
HEAD_DIM = 128
MLSTM_CHUNK = 128
MLSTM_M_INIT = -1e30
HYENA_GROUP_DIM = 128
HYENA_ORDER = 2
HYENA_BANDS = 16
HYENA_WINDOW_SHIFT = 0.05
N_GATE_COLS = 4 * MLSTM_HEADS
N_EXPERTS = 16
EC_CAPACITY_FACTOR = 2
RMS_EPS = 1e-6

LANES = 128
SUBLANES = 8
FFT_N2 = 64
FFT_PITCH = 72
NEG_BIG = -1e30
VMEM_LIMIT = 56 * 1024 * 1024


def _cparams(sem, vmem=None, flags=None):
    return pltpu.CompilerParams(dimension_semantics=sem, vmem_limit_bytes=vmem or VMEM_LIMIT, flags=flags)


def _split(a):
    hi = a.astype(BF16)
    lo = (a - hi.astype(F32)).astype(BF16)
    return hi, lo


def _dot(a, b):
    return jnp.dot(a, b, preferred_element_type=F32)


def _dot_nt(a, b):
    return lax.dot_general(a, b, (((1,), (1,)), ((), ())), preferred_element_type=F32)


def _dot3(a, b):
    ah, al = _split(a)
    bh, bl = _split(b)
    return _dot(ah, bh) + _dot(ah, bl) + _dot(al, bh)


def _dot3_nt(a, b):
    ah, al = _split(a)
    bh, bl = _split(b)
    return _dot_nt(ah, bh) + _dot_nt(ah, bl) + _dot_nt(al, bh)


def _dotc(w_hi, w_lo, x, passes):
    xh, xl = _split(x)
    out = _dot(w_hi, xh)
    if passes == 3:
        out = out + _dot(w_hi, xl) + _dot(w_lo, xh)
    return out


def _rms(x, g):
    return x * lax.rsqrt(jnp.mean(x * x, axis=-1, keepdims=True) + RMS_EPS) * g


def _sigmoid(x):
    return 1.0 / (1.0 + jnp.exp(-x))


def _log_sigmoid(x):
    return jnp.minimum(x, 0.0) - jnp.log(1.0 + jnp.exp(-jnp.abs(x)))


def _mod_kernel(c_ref, w_ref, b_ref, o_ref):
    o_ref[...] = _dot3(c_ref[...], w_ref[...]) + b_ref[...]


def _modulation(c, w_ada, b_ada):
    B, D = c.shape
    n = w_ada.shape[1]
    tn = 768
    return pl.pallas_call(
        _mod_kernel,
        grid=(n // tn,),
        in_specs=[pl.BlockSpec((B, D), lambda j: (0, 0)),
                  pl.BlockSpec((D, tn), lambda j: (0, j)),
                  pl.BlockSpec((1, tn), lambda j: (0, j))],
        out_specs=pl.BlockSpec((B, tn), lambda j: (0, j)),
        out_shape=jax.ShapeDtypeStruct((B, n), F32),
        compiler_params=_cparams(("parallel",)),
        name="adaln_mod",
    )(c, w_ada, b_ada.reshape(1, n))


def _inproj_kernel(x_ref, sc_ref, sh_ref, g_ref, wq_ref, bq_ref, wh_ref, bh_ref, wgt_ref, bgt_ref,
                   qkvo_ref, hx_ref, gt_ref):
    x = x_ref[0]
    h = _rms(x, g_ref[...]) * (1.0 + sc_ref[0]) + sh_ref[0]
    hb = h.astype(BF16)
    qkvo_ref[0] = _dot(hb, wq_ref[...]) + bq_ref[...]
    hx_ref[0] = _dot(hb, wh_ref[...]) + bh_ref[...]
    gt_ref[0] = _dot3_nt(wgt_ref[...], h) + bgt_ref[...]


def _in_projection(x, sc1, sh1, g_mix, w_in, b_in, ts=512):
    B, S, D = x.shape
    nq = 4 * MLSTM_HEADS * HEAD_DIM
    nh = w_in.shape[1] - nq - N_GATE_COLS
    wq = w_in[:, :nq].astype(BF16)
    wh = w_in[:, nq + N_GATE_COLS:].astype(BF16)
    bq = b_in[:nq].reshape(1, nq)
    bh = b_in[nq + N_GATE_COLS:].reshape(1, nh)
    ng = MLSTM_HEADS * SUBLANES
    pad = lambda t: jnp.pad(t.reshape(4, MLSTM_HEADS, -1).transpose(1, 0, 2), ((0, 0), (0, 4), (0, 0))).reshape(ng, -1)
    wgt = pad(w_in[:, nq:nq + N_GATE_COLS].T)
    bgt = pad(b_in[nq:nq + N_GATE_COLS].reshape(N_GATE_COLS, 1))
    const = lambda b, i: (0, 0)
    return pl.pallas_call(
        _inproj_kernel,
        grid=(B, S // ts),
        in_specs=[pl.BlockSpec((1, ts, D), lambda b, i: (b, i, 0)),
                  pl.BlockSpec((1, 1, D), lambda b, i: (b, 0, 0)),
                  pl.BlockSpec((1, 1, D), lambda b, i: (b, 0, 0)),
                  pl.BlockSpec((1, D), const),
                  pl.BlockSpec((D, nq), const), pl.BlockSpec((1, nq), const),
                  pl.BlockSpec((D, nh), const), pl.BlockSpec((1, nh), const),
                  pl.BlockSpec((ng, D), const), pl.BlockSpec((ng, 1), const)],
        out_specs=[pl.BlockSpec((1, ts, nq), lambda b, i: (b, i, 0)),
                   pl.BlockSpec((1, ts, nh), lambda b, i: (b, i, 0)),
                   pl.BlockSpec((1, ng, ts), lambda b, i: (b, 0, i))],
        out_shape=[jax.ShapeDtypeStruct((B, S, nq), F32),
                   jax.ShapeDtypeStruct((B, S, nh), F32),
                   jax.ShapeDtypeStruct((B, ng, S), F32)],
        compiler_params=_cparams(("parallel", "parallel")),
        name="in_proj",
    )(x, sc1.reshape(B, 1, D), sh1.reshape(B, 1, D), g_mix.reshape(1, D), wq, bq, wh, bh, wgt, bgt)


def _short_conv_chunk(src_ref, lead, c, nc, w_ref, b_ref, rows):
    L = rows.shape[0]
    S = nc * L
    t0 = pl.multiple_of(c * L, L)
    cur = src_ref[lead, pl.ds(t0, L), :]
    p0 = pl.multiple_of(jnp.maximum(t0 - SUBLANES, 0), SUBLANES)
    n0 = pl.multiple_of(jnp.minimum(t0 + L, S - SUBLANES), SUBLANES)
    prev_row = src_ref[lead, pl.ds(p0, SUBLANES), :][SUBLANES - 1:SUBLANES, :]
    next_row = src_ref[lead, pl.ds(n0, SUBLANES), :][0:1, :]
    prev_row = jnp.where(c > 0, prev_row, 0.0)
    next_row = jnp.where(c < nc - 1, next_row, 0.0)
    up = jnp.where(rows == 0, prev_row, pltpu.roll(cur, 1, 0))
    dn = jnp.where(rows == L - 1, next_row, pltpu.roll(cur, L - 1, 0))
    return b_ref[...] + up * w_ref[0:1, :] + cur * w_ref[1:2, :] + dn * w_ref[2:3, :]


def _mlstm_kernel(q_ref, k_ref, v_ref, o_ref, gt_ref, wq_ref, wk_ref, bq_ref, bk_ref, ng_ref,
                  y_ref, qt_ref, kb_ref, vt_ref, c_ref, cl_ref, ld_ref, st_ref, pc_ref, cs_ref, kq_ref, *, nc):
    L = MLSTM_CHUNK
    rows = lax.broadcasted_iota(jnp.int32, (L, LANES), 0)
    cols = lax.broadcasted_iota(jnp.int32, (L, LANES), 1)
    pad = 2 * SUBLANES
    piece_row = lax.broadcasted_iota(jnp.int32, (SUBLANES, LANES), 0)
    k_scale = HEAD_DIM ** -0.5

    def conv_body(c, carry):
        t0 = pl.multiple_of(c * L, L)
        yq = _short_conv_chunk(q_ref, 0, c, nc, wq_ref, bq_ref, rows)
        qt_ref[c] = (yq * _sigmoid(yq)).T.astype(BF16)
        yk = _short_conv_chunk(k_ref, 0, c, nc, wk_ref, bk_ref, rows)
        kb_ref[pl.ds(t0, L), :] = (yk * _sigmoid(yk) * k_scale).astype(BF16)
        vt_ref[c] = v_ref[0, pl.ds(t0, L), :].T
        gates = gt_ref[0, 0, :, pl.ds(t0, L)]
        lf = _log_sigmoid(jnp.where(piece_row < 3, gates[1:2], gates[3:4]))
        hi = lf.astype(BF16).astype(F32)
        mid = (lf - hi).astype(BF16).astype(F32)
        kind = piece_row % 3
        pc_ref[pl.ds(pl.multiple_of(c * SUBLANES, SUBLANES), SUBLANES), :] = jnp.where(
            piece_row >= 6, 0.0, jnp.where(kind == 0, hi, jnp.where(kind == 1, mid, lf - hi - mid)))
        return carry

    lax.fori_loop(0, nc, conv_body, 0, unroll=2)

    pieces = pc_ref[...].astype(BF16)
    fwd_rows = lax.broadcasted_iota(jnp.int32, pc_ref.shape, 0) % SUBLANES < 3
    cs_ref[...] = jnp.where(fwd_rows, _dot(pieces, (rows <= cols).astype(BF16)),
                            _dot(pieces, (rows >= cols).astype(BF16)))

    ROW_B, ROW_MAX, ROW_G, ROW_A, ROW_N, ROW_M = range(6)

    def stat_tile(rows_by_index, base=None):
        tile = jnp.zeros((SUBLANES, LANES), F32) if base is None else base
        for r, value in rows_by_index.items():
            tile = jnp.where(piece_row == r, value, tile)
        return tile

    def local_pass(c, rev):
        t0 = pl.multiple_of(c * L, L)
        slot = 2 * c + (1 if rev else 0)
        kb = kb_ref[pl.ds(t0, L), :]
        gates = gt_ref[0, 0, :, pl.ds(t0, L)]
        i_row = gates[(2 if rev else 0):(3 if rev else 1)]
        cs = cs_ref[pl.ds(pl.multiple_of(c * SUBLANES, SUBLANES), SUBLANES), :]
        o = 3 if rev else 0
        b_row = cs[o:o + 1] + cs[o + 1:o + 2] + cs[o + 2:o + 3]
        keep = (rows >= cols) if rev else (rows <= cols)
        g = b_row[:, 0:1] if rev else b_row[:, L - 1:L]
        w_tile = jnp.broadcast_to(i_row - b_row, (L, LANES)).T
        log_d = jnp.where(keep, w_tile + b_row, NEG_BIG)
        ld_ref[slot] = log_d
        w_end = g - b_row + i_row
        a = jnp.max(w_end, axis=1, keepdims=True)
        e_end = jnp.exp(w_end - a)
        ve = jnp.concatenate([vt_ref[c] * e_end, jnp.broadcast_to(e_end, (pad, L))], axis=0).astype(BF16)
        cn = _dot(ve, kb)
        cl_ref[slot] = cn[:HEAD_DIM]
        st_ref[slot] = stat_tile({ROW_B: b_row, ROW_MAX: jnp.max(log_d, axis=0, keepdims=True), ROW_G: g,
                                  ROW_A: a, ROW_N: cn[HEAD_DIM:HEAD_DIM + 1]})

    def local_body(c, carry):
        kq_ref[c] = _dot(kb_ref[pl.ds(pl.multiple_of(c * L, L), L), :], qt_ref[c])
        local_pass(c, False)
        local_pass(c, True)
        return carry

    lax.fori_loop(0, nc, local_body, 0, unroll=8)

    def scan_step(c, rev, n, m):
        slot = 2 * c + (1 if rev else 0)
        side = 1 if rev else 0
        stats = st_ref[slot]
        g = stats[ROW_G:ROW_G + 1]
        a = stats[ROW_A:ROW_A + 1]
        m_new = jnp.maximum(g + m, a)
        s_prev = jnp.exp(g + m - m_new)
        s_loc = jnp.exp(a - m_new)
        c_in = c_ref[side]
        c_ref[side] = s_prev * c_in + s_loc * cl_ref[slot]
        cl_ref[slot] = c_in
        n_new = s_prev * n + s_loc * stats[ROW_N:ROW_N + 1]
        st_ref[slot] = stat_tile({ROW_N: n, ROW_M: m}, base=stats)
        return n_new, m_new

    c_ref[...] = jnp.zeros_like(c_ref)
    n0 = jnp.zeros((1, HEAD_DIM), F32)
    m0 = jnp.full((1, LANES), MLSTM_M_INIT, F32)

    def scan_body(j, carry):
        n_f, m_f, n_b, m_b = carry
        n_f, m_f = scan_step(j, False, n_f, m_f)
        n_b, m_b = scan_step(nc - 1 - j, True, n_b, m_b)
        return n_f, m_f, n_b, m_b

    lax.fori_loop(0, nc, scan_body, (n0, m0, n0, m0))

    def output_pass(c, rev):
        t0 = pl.multiple_of(c * L, L)
        slot = 2 * c + (1 if rev else 0)
        qt = qt_ref[c]
        stats = st_ref[slot]
        log_inter = stats[ROW_B:ROW_B + 1] + stats[ROW_M:ROW_M + 1]
        m_t = jnp.maximum(log_inter, stats[ROW_MAX:ROW_MAX + 1])
        d = jnp.exp(ld_ref[slot] - m_t)
        e_inter = jnp.exp(log_inter - m_t)
        s = (kq_ref[c] * d).astype(BF16)
        vn = _dot(jnp.concatenate([vt_ref[c].astype(BF16), jnp.ones((pad, L), BF16)], axis=0), s)
        n_in = stats[ROW_N:ROW_N + 1]
        cq = _dot(jnp.concatenate([cl_ref[slot], jnp.broadcast_to(n_in, (pad, LANES))], axis=0).astype(BF16), qt)
        den = vn[HEAD_DIM:HEAD_DIM + 1] + e_inter * cq[HEAD_DIM:HEAD_DIM + 1]
        scale = 1.0 / jnp.maximum(jnp.abs(den), jnp.exp(-m_t))
        return ((vn[:HEAD_DIM] + e_inter * cq[:HEAD_DIM]) * scale).T

    def output_body(c, carry):
        t0 = pl.multiple_of(c * L, L)
        hs = output_pass(c, False) + output_pass(c, True)
        y_ref[0, pl.ds(t0, L), :] = _sigmoid(o_ref[0, pl.ds(t0, L), :]) * _rms(hs, ng_ref[...])
        return carry

    lax.fori_loop(0, nc, output_body, 0, unroll=8)


def _mlstm(qkvo, gates_t, conv_w, conv_b, norm_g):
    B, S, _ = qkvo.shape
    H, Dh = MLSTM_HEADS, HEAD_DIM
    nc = S // MLSTM_CHUNK
    seq = lambda off: pl.BlockSpec((1, S, Dh), lambda b, h: (b, 0, off + h))
    vec = lambda r, off: pl.BlockSpec((r, Dh), lambda b, h: (0, off + h))
    return pl.pallas_call(
        functools.partial(_mlstm_kernel, nc=nc),
        grid=(B, H),
        in_specs=[seq(0), seq(H), seq(2 * H), seq(3 * H),
                  pl.BlockSpec((1, 1, SUBLANES, S), lambda b, h: (b, h, 0, 0)),
                  vec(3, 0), vec(3, H), vec(1, 0), vec(1, H), vec(1, 0)],
        out_specs=pl.BlockSpec((1, S, Dh), lambda b, h: (b, 0, h)),
        out_shape=jax.ShapeDtypeStruct((B, S, H * Dh), F32),
        scratch_shapes=[pltpu.VMEM((nc, Dh, MLSTM_CHUNK), BF16), pltpu.VMEM((S, Dh), BF16),
                        pltpu.VMEM((nc, Dh, MLSTM_CHUNK), F32), pltpu.VMEM((2, Dh, Dh), F32),
                        pltpu.VMEM((2 * nc, Dh, Dh), F32), pltpu.VMEM((2 * nc, MLSTM_CHUNK, MLSTM_CHUNK), F32),
                        pltpu.VMEM((2 * nc, SUBLANES, LANES), F32), pltpu.VMEM((nc * SUBLANES, LANES), F32),
                        pltpu.VMEM((nc * SUBLANES, LANES), F32), pltpu.VMEM((nc, MLSTM_CHUNK, MLSTM_CHUNK), F32)],
        compiler_params=_cparams(("parallel", "arbitrary")),
        name="mlstm",
    )(qkvo, qkvo, qkvo, qkvo, gates_t.reshape(B, H, SUBLANES, S), conv_w, conv_w, conv_b.reshape(1, -1), conv_b.reshape(1, -1),
      norm_g.reshape(1, -1))


FILTER_PASSES = 3
FFT_UNROLL = 16


def _hilo(m):
    m32 = jnp.asarray(m, F32)
    hi = m32.astype(BF16)
    return hi, (m32 - hi.astype(F32)).astype(BF16)


def _stack_complex(m):
    return np.block([[m.real, -m.imag], [m.imag, m.real]])


def _dft_constants(S):
    N = 2 * S
    N2 = FFT_N2
    N1 = N // N2
    k1 = np.arange(N1)
    n2 = np.arange(N2)
    f1 = np.exp(-2j * np.pi * np.outer(k1, np.arange(N1)) / N1)
    f2 = np.exp(-2j * np.pi * np.outer(n2, n2) / N2)
    tw = np.exp(-2j * np.pi * np.outer(k1, n2) / N)
    g1 = np.conj(f1).T[:N1 // 2] / N
    consts = dict(
        f1_real=_hilo(np.concatenate([f1.real, f1.imag], axis=0)),
        f1_half=_hilo(_stack_complex(f1[:, :N1 // 2])),
        f2=_hilo(_stack_complex(f2)),
        g1=_hilo(_stack_complex(g1)),
        f2_tw=_hilo(np.stack([_stack_complex(f2 * tw[k][None, :]) for k in k1]))[0],
        f2_tw_inv=_hilo(np.stack([_stack_complex(np.conj(tw[k])[:, None] * np.conj(f2)) for k in k1]))[0],
        tw_re=jnp.asarray(np.broadcast_to(tw.real[:, :, None], (N1, N2, LANES)), F32),
        tw_im=jnp.asarray(np.broadcast_to(tw.imag[:, :, None], (N1, N2, LANES)), F32),
    )
    return N1, N2, consts


def _hy_hidden_kernel(w1t_ref, w1c_ref, w1s_ref, b1_ref, w2_ref, b2_ref, fr_ref, o_ref, *, S, T):
    j = pl.program_id(0) * T + lax.broadcasted_iota(jnp.int32, (T, 1), 0)
    p = jnp.where(j < S, j, 2 * S - j).astype(F32)
    t = p / (S - 1)
    w = (2.0 * math.pi) * p / S
    band = lax.broadcasted_iota(jnp.int32, (1, HYENA_BANDS), 1).astype(F32)
    bands = 1e-4 + band * ((HYENA_BANDS - 1 - 1e-4) / (HYENA_BANDS - 1))
    arg = bands * w
    pre = t * w1t_ref[...] + _dot3(jnp.cos(arg), w1c_ref[...]) + _dot3(-jnp.sin(arg), w1s_ref[...]) + b1_ref[...]
    hid = jnp.sin(fr_ref[...] * pre)
    o_ref[...] = jnp.sin(fr_ref[...] * (_dot3(hid, w2_ref[...]) + b2_ref[...]))


def _hy_hidden(S, w1, b1, w2, b2, freq):
    N = 2 * S
    T = 1024
    Hd = w2.shape[0]
    full = lambda a: pl.BlockSpec(a.shape, lambda i: (0,) * a.ndim)
    args = (w1[0:1], w1[1:1 + HYENA_BANDS], w1[1 + HYENA_BANDS:], b1.reshape(1, Hd), w2, b2.reshape(1, Hd),
            freq.reshape(1, Hd))
    return pl.pallas_call(
        functools.partial(_hy_hidden_kernel, S=S, T=T),
        grid=(N // T,),
        in_specs=[full(a) for a in args],
        out_specs=pl.BlockSpec((T, Hd), lambda i: (i, 0)),
        out_shape=jax.ShapeDtypeStruct((N, Hd), F32),
        compiler_params=_cparams(("parallel",)),
        name="hyena_hidden",
    )(*args)


def _fft_stage2(a_re, a_im, k1, twr_ref, twi_ref, f2_hi, f2_lo, passes):
    r0 = pl.multiple_of(k1 * FFT_PITCH, SUBLANES)
    ar = a_re[pl.ds(r0, FFT_N2), :]
    ai = a_im[pl.ds(r0, FFT_N2), :]
    twr = twr_ref[k1]
    twi = twi_ref[k1]
    t = jnp.concatenate([ar * twr - ai * twi, ar * twi + ai * twr], axis=0)
    return _dotc(f2_hi, f2_lo, t, passes)


def _hy_filter_kernel(hid_ref, w3f_ref, w3b_ref, df_ref, db_ref, bias_ref, f1h_ref, f1l_ref, f2h_ref, f2l_ref,
                      twr_ref, twi_ref, kr_ref, ki_ref, u_ref, a_re, a_im, *, S, N1):
    N2, P = FFT_N2, FFT_PITCH
    T = 512
    nb = T // N2

    def gen_body(i, carry):
        r0 = pl.multiple_of(i * T, T)
        hid = hid_ref[pl.ds(r0, T), :]
        j = r0 + lax.broadcasted_iota(jnp.int32, (T, 1), 0)
        t = jnp.where(j < S, j, 2 * S - j).astype(F32) / (S - 1)
        kf = _dot3(hid, w3f_ref[...]) * (jnp.exp(-t * jnp.abs(df_ref[...])) + HYENA_WINDOW_SHIFT)
        kb = _dot3(hid, w3b_ref[...]) * (jnp.exp(-t * jnp.abs(db_ref[...])) + HYENA_WINDOW_SHIFT)
        ker = (jnp.where(j < S, kf, 0.0) + jnp.where((j > S) | (j == 0), kb, 0.0)
               + jnp.where(j == 0, bias_ref[0], 0.0))
        for b in range(nb):
            u_ref[pl.ds(pl.multiple_of((i * nb + b) * P, SUBLANES), N2), :] = ker[b * N2:(b + 1) * N2]
        return carry

    lax.fori_loop(0, 2 * S // T, gen_body, 0)

    def stage1(n2, carry):
        x = u_ref[pl.ds(n2, N1, stride=P), :]
        a = _dotc(f1h_ref[...], f1l_ref[...], x, FILTER_PASSES)
        a_re[pl.ds(n2, N1, stride=P), :] = a[:N1]
        a_im[pl.ds(n2, N1, stride=P), :] = a[N1:]
        return carry

    lax.fori_loop(0, N2, stage1, 0, unroll=FFT_UNROLL)

    def stage2(k1, carry):
        x = _fft_stage2(a_re, a_im, k1, twr_ref, twi_ref, f2h_ref[...], f2l_ref[...], FILTER_PASSES)
        r0 = pl.multiple_of(k1 * N2, N2)
        kr_ref[0, pl.ds(r0, N2), :] = x[:N2]
        ki_ref[0, pl.ds(r0, N2), :] = x[N2:]
        return carry

    lax.fori_loop(0, N1, stage2, 0, unroll=2 * FFT_UNROLL)


def _hy_filter_spectrum(S, hid, w3, deltas, bias, N1, consts):
    N = 2 * S
    N2, P = FFT_N2, FFT_PITCH
    Hd = hid.shape[1]
    nblk = w3.shape[1] // (2 * HYENA_ORDER * LANES)
    f1h, f1l = consts["f1_real"]
    f2h, f2l = consts["f2"]
    full = lambda a: pl.BlockSpec(a.shape, lambda o, j: (0,) * a.ndim, pipeline_mode=pl.Buffered(1))
    colf = lambda o, j: (0, (2 * o) * nblk + j)
    colb = lambda o, j: (0, (2 * o + 1) * nblk + j)
    out = jax.ShapeDtypeStruct((HYENA_ORDER, N, nblk * LANES), F32)
    return pl.pallas_call(
        functools.partial(_hy_filter_kernel, S=S, N1=N1),
        grid=(HYENA_ORDER, nblk),
        in_specs=[full(hid),
                  pl.BlockSpec((Hd, LANES), colf), pl.BlockSpec((Hd, LANES), colb),
                  pl.BlockSpec((1, LANES), colf), pl.BlockSpec((1, LANES), colb),
                  pl.BlockSpec((1, 1, LANES), lambda o, j: (o, 0, j)),
                  full(f1h), full(f1l), full(f2h), full(f2l), full(consts["tw_re"]), full(consts["tw_im"])],
        out_specs=[pl.BlockSpec((1, N, LANES), lambda o, j: (o, 0, j))] * 2,
        out_shape=[out, out],
        scratch_shapes=[pltpu.VMEM((N1 * P, LANES), F32)] * 3,
        compiler_params=_cparams(("parallel", "parallel")),
        name="hyena_filter_spectrum",
    )(hid, w3, w3, deltas.reshape(1, -1), deltas.reshape(1, -1), bias.reshape(HYENA_ORDER, 1, -1),
      f1h, f1l, f2h, f2l, consts["tw_re"], consts["tw_im"])


def _hy_conv_kernel(u_ref, x_ref, kr_ref, ki_ref, wu_ref, bu_ref, wx_ref, bx_ref, ng_ref,
                    f1_ref, g1_ref, mf_ref, mi_ref, o_ref, u_re, u_im, a_re, a_im, *, N1, conv_u, last):
    N2, P = FFT_N2, FFT_PITCH
    nh = N1 // 2
    rows = lax.broadcasted_iota(jnp.int32, (N2, LANES), 0)

    def load_body(c, carry):
        r0 = pl.multiple_of(c * P, SUBLANES)
        for b, dst in ((0, u_re), (1, u_im)):
            if conv_u:
                dst[pl.ds(r0, N2), :] = _short_conv_chunk(u_ref, b, c, nh, wu_ref, bu_ref, rows)
            else:
                dst[pl.ds(r0, N2), :] = u_ref[b, pl.ds(pl.multiple_of(c * N2, N2), N2), :]
        return carry

    lax.fori_loop(0, nh, load_body, 0, unroll=4)

    def stage1(n2, carry):
        x = jnp.concatenate([u_re[pl.ds(n2, nh, stride=P), :], u_im[pl.ds(n2, nh, stride=P), :]], axis=0)
        a = _dot(f1_ref[...], x.astype(BF16))
        a_re[pl.ds(n2, N1, stride=P), :] = a[:N1]
        a_im[pl.ds(n2, N1, stride=P), :] = a[N1:]
        return carry

    lax.fori_loop(0, N2, stage1, 0, unroll=FFT_UNROLL)

    def stage2(k1, carry):
        r0 = pl.multiple_of(k1 * P, SUBLANES)
        a = jnp.concatenate([a_re[pl.ds(r0, N2), :], a_im[pl.ds(r0, N2), :]], axis=0)
        x = _dot(mf_ref[k1], a.astype(BF16))
        s0 = pl.multiple_of(k1 * N2, N2)
        kr = kr_ref[0, pl.ds(s0, N2), :]
        ki = ki_ref[0, pl.ds(s0, N2), :]
        xr, xi = x[:N2], x[N2:]
        y = jnp.concatenate([xr * kr - xi * ki, xr * ki + xi * kr], axis=0)
        b = _dot(mi_ref[k1], y.astype(BF16))
        a_re[pl.ds(r0, N2), :] = b[:N2]
        a_im[pl.ds(r0, N2), :] = b[N2:]
        return carry

    lax.fori_loop(0, N1, stage2, 0, unroll=2 * FFT_UNROLL)

    def inv2(n2, carry):
        b = jnp.concatenate([a_re[pl.ds(n2, N1, stride=P), :], a_im[pl.ds(n2, N1, stride=P), :]], axis=0)
        y = _dot(g1_ref[...], b.astype(BF16))
        u_re[pl.ds(n2, nh, stride=P), :] = y[:nh]
        u_im[pl.ds(n2, nh, stride=P), :] = y[nh:]
        return carry

    lax.fori_loop(0, N2, inv2, 0, unroll=FFT_UNROLL)

    def out_body(c, carry):
        r0 = pl.multiple_of(c * P, SUBLANES)
        t0 = pl.multiple_of(c * N2, N2)
        for b, src in ((0, u_re), (1, u_im)):
            z = _short_conv_chunk(x_ref, b, c, nh, wx_ref, bx_ref, rows) * src[pl.ds(r0, N2), :]
            o_ref[b, pl.ds(t0, N2), :] = _rms(z, ng_ref[...]) if last else z
        return carry

    lax.fori_loop(0, nh, out_body, 0, unroll=4)


def _hy_conv(u, u_blk, x, x_blk, kr, ki, order, conv_w, conv_b, norm_g, N1, consts, conv_u, last):
    B, S, _ = u.shape
    N2, P = FFT_N2, FFT_PITCH
    N = 2 * S
    nblk = kr.shape[2] // LANES
    pair = 2
    single = pl.Buffered(1)
    full = lambda a: pl.BlockSpec(a.shape, lambda j, p: (0,) * a.ndim, pipeline_mode=single)
    seq = lambda off: pl.BlockSpec((pair, S, LANES), lambda j, p: (p, 0, off + j))
    vec = lambda r, off: pl.BlockSpec((r, LANES), lambda j, p: (0, off + j))
    spec = pl.BlockSpec((1, N, LANES), lambda j, p: (order, 0, j), pipeline_mode=single)
    mats = [consts["f1_half"][0], consts["g1"][0], consts["f2_tw"], consts["f2_tw_inv"]]
    return pl.pallas_call(
        functools.partial(_hy_conv_kernel, N1=N1, conv_u=conv_u, last=last),
        grid=(nblk, B // pair),
        in_specs=[seq(u_blk), seq(x_blk), spec, spec,
                  vec(3, u_blk if conv_u else 0), vec(1, u_blk if conv_u else 0), vec(3, x_blk), vec(1, x_blk),
                  pl.BlockSpec((1, LANES), lambda j, p: (0, j))] + [full(m) for m in mats],
        out_specs=pl.BlockSpec((pair, S, LANES), lambda j, p: (p, 0, j)),
        out_shape=jax.ShapeDtypeStruct((B, S, nblk * LANES), F32),
        scratch_shapes=[pltpu.VMEM((N1 // 2 * P, LANES), F32)] * 2 + [pltpu.VMEM((N1 * P, LANES), F32)] * 2,
        compiler_params=_cparams(("parallel", "parallel")),
        name=f"hyena_conv{order}",
    )(u, x, kr, ki, conv_w, conv_b.reshape(1, -1), conv_w, conv_b.reshape(1, -1), norm_g.reshape(1, -1), *mats)


def _hyena(hx, conv_w, conv_b, w1, b1, w2, b2, w3, freq, deltas, bias, norm_g):
    B, S, C3 = hx.shape
    nblk = C3 // (HYENA_ORDER + 1) // LANES
    N1, N2, consts = _dft_constants(S)
    hid = _hy_hidden(S, w1, b1, w2, b2, freq)
    kr, ki = _hy_filter_spectrum(S, hid, w3, deltas, bias, N1, consts)
    z1 = _hy_conv(hx, 2 * nblk, hx, 0, kr, ki, 0, conv_w, conv_b, norm_g, N1, consts, True, False)
    return _hy_conv(z1, 0, hx, nblk, kr, ki, 1, conv_w, conv_b, norm_g, N1, consts, False, True)


def _outproj_kernel(ym_ref, yh_ref, x_ref, gt_ref, sc_ref, sh_ref, g_ref, wm_ref, wh_ref, wr_ref,
                    x1_ref, hf_ref, aff_ref):
    mixed = _dot(ym_ref[0].astype(BF16), wm_ref[...]) + _dot(yh_ref[0].astype(BF16), wh_ref[...])
    x1 = x_ref[0] + gt_ref[0] * mixed
    x1_ref[0] = x1
    hf = _rms(x1, g_ref[...]) * (1.0 + sc_ref[0]) + sh_ref[0]
    hf_ref[0] = hf.astype(BF16)
    logits = _dot3_nt(wr_ref[...], hf)
    e = jnp.exp(logits - jnp.max(logits, axis=0, keepdims=True))
    aff_ref[0] = e / jnp.sum(e, axis=0, keepdims=True)


def _out_projection(y_m, y_h, x, gt1, sc2, sh2, g_ffn, w_out, w_router, ts=512):
    B, S, D = x.shape
    Wm = y_m.shape[2]
    E = w_router.shape[1]
    wm = w_out[:Wm].astype(BF16)
    wh = w_out[Wm:].astype(BF16)
    tile = lambda w: pl.BlockSpec((1, ts, w), lambda b, i: (b, i, 0))
    row = pl.BlockSpec((1, 1, D), lambda b, i: (b, 0, 0))
    full = lambda a: pl.BlockSpec(a.shape, lambda b, i: (0,) * a.ndim)
    wr = w_router.T
    g = g_ffn.reshape(1, D)
    return pl.pallas_call(
        _outproj_kernel,
        grid=(B, S // ts),
        in_specs=[tile(Wm), tile(y_h.shape[2]), tile(D), row, row, row, full(g), full(wm), full(wh), full(wr)],
        out_specs=[tile(D), tile(D), pl.BlockSpec((1, E, ts), lambda b, i: (b, 0, i))],
        out_shape=[jax.ShapeDtypeStruct((B, S, D), F32), jax.ShapeDtypeStruct((B, S, D), BF16),
                   jax.ShapeDtypeStruct((B, E, S), F32)],
        compiler_params=_cparams(("parallel", "parallel")),
        name="out_proj_router",
    )(y_m, y_h, x, gt1.reshape(B, 1, D), sc2.reshape(B, 1, D), sh2.reshape(B, 1, D), g, wm, wh, wr)


def _route_kernel(aff_ref, pos_ref, cnt_ref, *, S, cap):
    aff = aff_ref[0]
    E = aff.shape[0]
    tpos = lax.broadcasted_iota(jnp.int32, (E, S), 1)
    count = lambda mask: jnp.sum(jnp.where(mask, 1.0, 0.0), axis=1, keepdims=True)
    as_float = lambda word: lax.bitcast_convert_type(word, F32)

    def value_step(i, prefix):
        cand = prefix | jnp.left_shift(1, 30 - i)
        return jnp.where(count(aff >= as_float(cand)) >= cap, cand, prefix)

    thresh = as_float(lax.fori_loop(0, 31, value_step, jnp.zeros((E, 1), jnp.int32)))
    gt = aff > thresh
    eq = aff == thresh
    need = cap - count(gt)
    nbits = S.bit_length() - 1

    def index_step(i, x):
        cand = x | jnp.left_shift(1, nbits - 1 - i)
        return jnp.where(count(eq & (tpos < cand)) < need, cand, x)

    last_tie = lax.fori_loop(0, nbits, index_step, jnp.zeros((E, 1), jnp.int32))
    sel = jnp.where(gt | (eq & (tpos <= last_tie)), 1.0, 0.0)
    before = (lax.broadcasted_iota(jnp.int32, (LANES, LANES), 0)
              < lax.broadcasted_iota(jnp.int32, (LANES, LANES), 1)).astype(BF16)
    carry = jnp.zeros((E, 1), F32)
    lane = lax.broadcasted_iota(jnp.int32, (E, LANES), 1)
    starts = jnp.zeros((E, LANES), F32)
    for c in range(S // LANES):
        sc = sel[:, c * LANES:(c + 1) * LANES]
        rank = _dot(sc.astype(BF16), before) + carry
        pos_ref[0, :, c * LANES:(c + 1) * LANES] = jnp.where(sc > 0.0, rank, -1.0).astype(jnp.int32)
        starts = jnp.where(lane == c, carry, starts)
        carry = carry + jnp.sum(sc, axis=1, keepdims=True)
    cnt_ref[0] = jnp.where(lane >= S // LANES, carry, starts).astype(jnp.int32)


def _route(aff_t, cap):
    B, E, S = aff_t.shape
    assert S // LANES < LANES
    blk = pl.BlockSpec((1, E, S), lambda b: (b, 0, 0))
    return pl.pallas_call(
        functools.partial(_route_kernel, S=S, cap=cap),
        grid=(B,),
        in_specs=[blk],
        out_specs=[blk, pl.BlockSpec((1, E, LANES), lambda b: (b, 0, 0))],
        out_shape=[jax.ShapeDtypeStruct((B, E, S), jnp.int32), jax.ShapeDtypeStruct((B, E, LANES), jnp.int32)],
        compiler_params=_cparams(("parallel",)),
        name="route_topk",
    )(aff_t)


def _window_tables(cnt, S, cap):
    B, E, _ = cnt.shape
    nch = S // LANES
    bounds = cnt[:, :, 1:nch + 1]
    first = jnp.arange(cap // GATHER_SLOTS, dtype=jnp.int32) * GATHER_SLOTS
    chunk_of = lambda slot: jnp.sum((bounds[:, :, None, :] <= slot[None, None, :, None]).astype(jnp.int32), axis=-1)
    per = GATHER_ALIGN // LANES
    step = GATHER_TOKENS // GATHER_ALIGN
    lo_a = chunk_of(first) // per
    hi_a = chunk_of(first + GATHER_SLOTS - 1) // per
    g_n = (hi_a - lo_a) // step + 1
    g_lo = jnp.minimum(lo_a, S // GATHER_ALIGN - step * g_n)
    edges = cnt[:, :, 0:nch + 1:SCATTER_TOKENS // LANES]
    lo, hi = edges[:, :, :-1], edges[:, :, 1:]
    s_lo = jnp.swapaxes(lo // SCATTER_SLOTS, 1, 2)
    s_n = jnp.swapaxes(jnp.where(hi > lo, (hi - 1) // SCATTER_SLOTS - lo // SCATTER_SLOTS + 1, 0), 1, 2)

    def flat_runs(starts, counts, stride, length):
        groups = counts.shape[-1]
        ends = jnp.cumsum(counts, axis=-1)
        j = jnp.arange(length, dtype=jnp.int32)
        grp = jnp.minimum(jnp.sum((ends[..., None, :] <= j[:, None]).astype(jnp.int32), axis=-1), groups - 1)
        chosen = grp[..., None] == jnp.arange(groups, dtype=jnp.int32)
        pick = lambda t: jnp.sum(jnp.where(chosen, t[..., None, :], 0), axis=-1)
        valid = j < ends[..., -1:]
        val = jnp.where(valid, pick(starts) + (j - pick(ends - counts)) * stride, 0)
        flat = lambda t: t.reshape(-1).astype(jnp.int32)
        return flat(jnp.where(valid, grp, 0)), flat(val), flat(ends[..., -1])

    return flat_runs(g_lo, g_n, step, GATHER_LIST) + flat_runs(s_lo, s_n, 1, SCATTER_LIST)


META_ROWS = 16


GATHER_SLOTS = 128
GATHER_TOKENS = 512
GATHER_ALIGN = 256
GATHER_LIST = 12
SCATTER_SLOTS = 256
SCATTER_TOKENS = 512
SCATTER_LIST = N_EXPERTS * ((SCATTER_TOKENS - 1) // SCATTER_SLOTS + 2)


def _gather_kernel(blk_ref, win_ref, cnt_ref, pos_ref, aff_ref, hf_ref, xs_ref, meta_ref, acc_ref, macc_ref):
    R, tk = GATHER_SLOTS, GATHER_TOKENS
    base = (pl.program_id(0) * pl.num_programs(1) + pl.program_id(1)) * GATHER_LIST
    count = cnt_ref[pl.program_id(0) * pl.num_programs(1) + pl.program_id(1)]
    mrow = lax.broadcasted_iota(jnp.int32, (META_ROWS, tk), 0)
    lane = lax.broadcasted_iota(jnp.int32, (1, tk), 1)
    srow = lax.broadcasted_iota(jnp.int32, (R, tk), 0)
    acc_ref[...] = jnp.zeros_like(acc_ref)
    macc_ref[...] = jnp.zeros_like(macc_ref)

    def contribution(j):
        r0 = pl.multiple_of(blk_ref[base + j] * R, R)
        t0 = pl.multiple_of(win_ref[base + j] * GATHER_ALIGN, GATHER_ALIGN)
        first_slot = jnp.where(j < count, r0, -(R + 1))
        onehot = jnp.where(first_slot + srow == pos_ref[0, 0, :, pl.ds(t0, tk)], 1.0, 0.0).astype(BF16)
        a = aff_ref[0, 0, :, pl.ds(t0, tk)]
        hi = a.astype(BF16).astype(F32)
        mid = (a - hi).astype(BF16).astype(F32)
        lo = (a - hi - mid).astype(BF16).astype(F32)
        t = t0 + lane
        pieces = (hi, mid, lo, (t // 64).astype(F32), (t % 64).astype(F32))
        meta = jnp.zeros((META_ROWS, tk), F32)
        for r, piece in enumerate(pieces):
            meta = jnp.where(mrow == r, piece, meta)
        return r0, _dot(onehot, hf_ref[0, pl.ds(t0, tk), :]), _dot_nt(meta.astype(BF16), onehot)

    def pair_body(i, carry):
        for r0, rows, meta in (contribution(2 * i), contribution(2 * i + 1)):
            acc_ref[pl.ds(r0, R), :] += rows
            macc_ref[:, pl.ds(r0, R)] += meta
        return carry

    lax.fori_loop(0, (count + 1) // 2, pair_body, 0)
    xs_ref[0, 0] = acc_ref[...].astype(BF16)
    meta_ref[0, 0] = macc_ref[...]


def _gather(g_blk, g_win, g_cnt, pos, aff_t, hf, cap):
    B, E, S = pos.shape
    D = hf.shape[2]
    most = cap // GATHER_SLOTS + (S // GATHER_ALIGN - 1) // (GATHER_TOKENS // GATHER_ALIGN)
    assert most <= GATHER_LIST and GATHER_LIST % 2 == 0
    row = pl.BlockSpec((1, 1, 1, S), lambda b, e, *_: (b, e, 0, 0))
    return pl.pallas_call(
        _gather_kernel,
        grid_spec=pltpu.PrefetchScalarGridSpec(
            num_scalar_prefetch=3,
            grid=(B, E),
            in_specs=[row, row, pl.BlockSpec((1, S, D), lambda b, e, *_: (b, 0, 0))],
            out_specs=[pl.BlockSpec((1, 1, cap, D), lambda b, e, *_: (b, e, 0, 0)),
                       pl.BlockSpec((1, 1, META_ROWS, cap), lambda b, e, *_: (b, e, 0, 0))],
            scratch_shapes=[pltpu.VMEM((cap, D), F32), pltpu.VMEM((META_ROWS, cap), F32)]),
        out_shape=[jax.ShapeDtypeStruct((B, E, cap, D), BF16), jax.ShapeDtypeStruct((B, E, META_ROWS, cap), F32)],
        compiler_params=_cparams(("parallel", "arbitrary")),
        name="moe_gather",
    )(g_blk, g_win, g_cnt, pos.reshape(B, E, 1, S), aff_t.reshape(B, E, 1, S), hf)


def _ffn_kernel(xs_ref, meta_ref, wg_ref, wu_ref, wd_ref, y_ref, wdb_ref, *, tf):
    x = xs_ref[0, 0]
    cap = x.shape[0]
    FF = wg_ref.shape[2]

    @pl.when(pl.program_id(1) == 0)
    def _():
        for f in range(FF // tf):
            wdb_ref[f * tf:(f + 1) * tf, :] = wd_ref[0, f * tf:(f + 1) * tf, :].astype(BF16)

    y = jnp.zeros((cap, wd_ref.shape[2]), F32)
    for f in range(FF // tf):
        hg = _dot(x, wg_ref[0, :, f * tf:(f + 1) * tf])
        hu = _dot(x, wu_ref[0, :, f * tf:(f + 1) * tf])
        hid = (hg * _sigmoid(hg) * hu).astype(BF16)
        y = y + _dot(hid, wdb_ref[f * tf:(f + 1) * tf, :])
    m = meta_ref[0, 0]
    g_row = m[0:1] + m[1:2] + m[2:3]
    eye = lax.broadcasted_iota(jnp.int32, (cap, cap), 0) == lax.broadcasted_iota(jnp.int32, (cap, cap), 1)
    g_col = jnp.sum(jnp.where(eye, g_row, 0.0), axis=1, keepdims=True)
    y_ref[0, 0] = (y * g_col).astype(BF16)


def _expert_ffn(xs, meta, w_gate, w_up, w_down, tf=512):
    B, E, cap, D = xs.shape
    FF = w_gate.shape[2]
    return pl.pallas_call(
        functools.partial(_ffn_kernel, tf=tf),
        grid=(E, B),
        in_specs=[pl.BlockSpec((1, 1, cap, D), lambda e, b: (b, e, 0, 0)),
                  pl.BlockSpec((1, 1, META_ROWS, cap), lambda e, b: (b, e, 0, 0)),
                  pl.BlockSpec((1, D, FF), lambda e, b: (e, 0, 0)),
                  pl.BlockSpec((1, D, FF), lambda e, b: (e, 0, 0)),
                  pl.BlockSpec((1, FF, D), lambda e, b: (e, 0, 0))],
        out_specs=pl.BlockSpec((1, 1, cap, D), lambda e, b: (b, e, 0, 0)),
        out_shape=jax.ShapeDtypeStruct((B, E, cap, D), BF16),
        scratch_shapes=[pltpu.VMEM((FF, D), BF16)],
        compiler_params=_cparams(("parallel", "arbitrary")),
        name="moe_ffn",
    )(xs, meta, w_gate.astype(BF16), w_up.astype(BF16), w_down)


def _scatter_kernel(e_ref, blk_ref, cnt_ref, y_ref, meta_ref, x1_ref, gt_ref, g_ref, o_ref, acc_ref):
    tt, R = SCATTER_TOKENS, SCATTER_SLOTS
    tile = pl.program_id(0) * pl.num_programs(1) + pl.program_id(1)
    tok = pl.program_id(1) * tt + lax.broadcasted_iota(jnp.int32, (tt, R), 0)
    acc_ref[...] = jnp.zeros_like(acc_ref)
    count = cnt_ref[tile]

    def contribution(j):
        e = e_ref[tile * SCATTER_LIST + j]
        r0 = pl.multiple_of(blk_ref[tile * SCATTER_LIST + j] * R, R)
        m = meta_ref[0, e, :, pl.ds(r0, R)]
        idx = jnp.where(j < count, (m[3:4] * 64.0 + m[4:5]).astype(jnp.int32), -1)
        onehot = jnp.where(tok == idx, 1.0, 0.0).astype(BF16)
        return _dot(onehot, y_ref[0, e, pl.ds(r0, R), :])

    def pair_body(i, carry):
        acc_ref[...] += contribution(2 * i) + contribution(2 * i + 1)
        return carry

    lax.fori_loop(0, (count + 1) // 2, pair_body, 0)
    o_ref[0] = _rms(x1_ref[0] + gt_ref[0] * acc_ref[...], g_ref[...])


def _scatter_final(s_e, s_blk, s_cnt, y, meta, x1, gt2, g_final):
    B, E, cap, D = y.shape
    S = x1.shape[1]
    tt = SCATTER_TOKENS
    tile = pl.BlockSpec((1, tt, D), lambda b, i, *_: (b, i, 0))
    return pl.pallas_call(
        _scatter_kernel,
        grid_spec=pltpu.PrefetchScalarGridSpec(
            num_scalar_prefetch=3,
            grid=(B, S // tt),
            in_specs=[pl.BlockSpec((1, E, cap, D), lambda b, i, *_: (b, 0, 0, 0)),
                      pl.BlockSpec((1, E, META_ROWS, cap), lambda b, i, *_: (b, 0, 0, 0)),
                      tile, pl.BlockSpec((1, 1, D), lambda b, i, *_: (b, 0, 0)),
                      pl.BlockSpec((1, D), lambda b, i, *_: (0, 0))],
            out_specs=tile,
            scratch_shapes=[pltpu.VMEM((tt, D), F32)]),
        out_shape=jax.ShapeDtypeStruct((B, S, D), F32),
        compiler_params=_cparams(("parallel", "parallel")),
        name="moe_scatter_final",
    )(s_e, s_blk, s_cnt, y, meta, x1, gt2.reshape(B, 1, D), g_final.reshape(1, D))


def kernel(x, c, w_ada, b_ada, g_mix, w_in, b_in, conv_qk_w, conv_qk_b, mlstm_norm_g, conv_hy_w, conv_hy_b,
           hy_w1, hy_b1, hy_w2, hy_b2, hy_w3, hy_freq, hy_deltas, hy_bias, hyena_norm_g, w_out, g_ffn,
           w_router, w_gate, w_up, w_down, g_final):
    S = x.shape[1]
    cap = EC_CAPACITY_FACTOR * S // N_EXPERTS
    assert w_ada.shape[0] == 1, "single-layer block: the final RMSNorm is fused into the MoE scatter"
    l = 0
    mod = _modulation(c, w_ada[l], b_ada[l])
    sh1, sc1, gt1, sh2, sc2, gt2 = jnp.split(mod, 6, axis=-1)
    qkvo, hx, gates_t = _in_projection(x, sc1, sh1, g_mix[l], w_in[l], b_in[l])
    y_m = _mlstm(qkvo, gates_t, conv_qk_w[l], conv_qk_b[l], mlstm_norm_g[l])
    y_h = _hyena(hx, conv_hy_w[l], conv_hy_b[l], hy_w1[l], hy_b1[l], hy_w2[l], hy_b2[l], hy_w3[l],
                 hy_freq[l], hy_deltas[l], hy_bias[l], hyena_norm_g[l])
    x1, hf, aff_t = _out_projection(y_m, y_h, x, gt1, sc2, sh2, g_ffn[l], w_out[l], w_router[l])
    pos, cnt = _route(aff_t, cap)
    g_blk, g_win, g_cnt, s_e, s_blk, s_cnt = _window_tables(cnt, S, cap)
    xs, meta = _gather(g_blk, g_win, g_cnt, pos, aff_t, hf, cap)
    y = _expert_ffn(xs, meta, w_gate[l], w_up[l], w_down[l])
    return _scatter_final(s_e, s_blk, s_cnt, y, meta, x1, gt2, g_final)
```

```python
import functools
import math

import numpy as np
import jax
import jax.numpy as jnp
from jax import lax
from jax.experimental import pallas as pl
from jax.experimental.pallas import tpu as pltpu

F32 = jnp.float32
BF16 = jnp.bfloat16

MLSTM_HEADS = 4
HEAD_DIM = 128
MLSTM_CHUNK = 128
MLSTM_M_INIT = -1e30
HYENA_GROUP_DIM = 128
HYENA_ORDER = 2
HYENA_BANDS = 16
HYENA_WINDOW_SHIFT = 0.05
N_GATE_COLS = 4 * MLSTM_HEADS
N_EXPERTS = 16
EC_CAPACITY_FACTOR = 2
RMS_EPS = 1e-6

LANES = 128
SUBLANES = 8
FFT_N2 = 64
FFT_PITCH = 72
NEG_BIG = -1e30
VMEM_LIMIT = 56 * 1024 * 1024


def _cparams(sem, vmem=None, flags=None):
    return pltpu.CompilerParams(dimension_semantics=sem, vmem_limit_bytes=vmem or VMEM_LIMIT, flags=flags)


def _split(a):
    hi = a.astype(BF16)
    lo = (a - hi.astype(F32)).astype(BF16)
    return hi, lo


def _dot(a, b):
    return jnp.dot(a, b, preferred_element_type=F32)


def _dot_nt(a, b):
    return lax.dot_general(a, b, (((1,), (1,)), ((), ())), preferred_element_type=F32)


def _dot3(a, b):
    ah, al = _split(a)
    bh, bl = _split(b)
    return _dot(ah, bh) + _dot(ah, bl) + _dot(al, bh)


def _dot3_nt(a, b):
    ah, al = _split(a)
    bh, bl = _split(b)
    return _dot_nt(ah, bh) + _dot_nt(ah, bl) + _dot_nt(al, bh)


def _dotc(w_hi, w_lo, x, passes):
    xh, xl = _split(x)
    out = _dot(w_hi, xh)
    if passes == 3:
        out = out + _dot(w_hi, xl) + _dot(w_lo, xh)
    return out


def _rms(x, g):
    return x * lax.rsqrt(jnp.mean(x * x, axis=-1, keepdims=True) + RMS_EPS) * g


def _sigmoid(x):
    return 1.0 / (1.0 + jnp.exp(-x))


def _log_sigmoid(x):
    return jnp.minimum(x, 0.0) - jnp.log(1.0 + jnp.exp(-jnp.abs(x)))


def _mod_kernel(c_ref, w_ref, b_ref, o_ref):
    o_ref[...] = _dot3(c_ref[...], w_ref[...]) + b_ref[...]


def _modulation(c, w_ada, b_ada):
    B, D = c.shape
    n = w_ada.shape[1]
    tn = 768
    return pl.pallas_call(
        _mod_kernel,
        grid=(n // tn,),
        in_specs=[pl.BlockSpec((B, D), lambda j: (0, 0)),
                  pl.BlockSpec((D, tn), lambda j: (0, j)),
                  pl.BlockSpec((1, tn), lambda j: (0, j))],
        out_specs=pl.BlockSpec((B, tn), lambda j: (0, j)),
        out_shape=jax.ShapeDtypeStruct((B, n), F32),
        compiler_params=_cparams(("parallel",)),
        name="adaln_mod",
    )(c, w_ada, b_ada.reshape(1, n))


def _inproj_kernel(x_ref, sc_ref, sh_ref, g_ref, wq_ref, bq_ref, wh_ref, bh_ref, wgt_ref, bgt_ref,
                   qkvo_ref, hx_ref, gt_ref):
    x = x_ref[0]
    h = _rms(x, g_ref[...]) * (1.0 + sc_ref[0]) + sh_ref[0]
    hb = h.astype(BF16)
    qkvo_ref[0] = _dot(hb, wq_ref[...]) + bq_ref[...]
    hx_ref[0] = _dot(hb, wh_ref[...]) + bh_ref[...]
    gt_ref[0] = _dot3_nt(wgt_ref[...], h) + bgt_ref[...]


def _in_projection(x, sc1, sh1, g_mix, w_in, b_in, ts=512):
    B, S, D = x.shape
    nq = 4 * MLSTM_HEADS * HEAD_DIM
    nh = w_in.shape[1] - nq - N_GATE_COLS
    wq = w_in[:, :nq].astype(BF16)
    wh = w_in[:, nq + N_GATE_COLS:].astype(BF16)
    bq = b_in[:nq].reshape(1, nq)
    bh = b_in[nq + N_GATE_COLS:].reshape(1, nh)
    ng = MLSTM_HEADS * SUBLANES
    pad = lambda t: jnp.pad(t.reshape(4, MLSTM_HEADS, -1).transpose(1, 0, 2), ((0, 0), (0, 4), (0, 0))).reshape(ng, -1)
    wgt = pad(w_in[:, nq:nq + N_GATE_COLS].T)
    bgt = pad(b_in[nq:nq + N_GATE_COLS].reshape(N_GATE_COLS, 1))
    const = lambda b, i: (0, 0)
    return pl.pallas_call(
        _inproj_kernel,
        grid=(B, S // ts),
        in_specs=[pl.BlockSpec((1, ts, D), lambda b, i: (b, i, 0)),
                  pl.BlockSpec((1, 1, D), lambda b, i: (b, 0, 0)),
                  pl.BlockSpec((1, 1, D), lambda b, i: (b, 0, 0)),
                  pl.BlockSpec((1, D), const),
                  pl.BlockSpec((D, nq), const), pl.BlockSpec((1, nq), const),
                  pl.BlockSpec((D, nh), const), pl.BlockSpec((1, nh), const),
                  pl.BlockSpec((ng, D), const), pl.BlockSpec((ng, 1), const)],
        out_specs=[pl.BlockSpec((1, ts, nq), lambda b, i: (b, i, 0)),
                   pl.BlockSpec((1, ts, nh), lambda b, i: (b, i, 0)),
                   pl.BlockSpec((1, ng, ts), lambda b, i: (b, 0, i))],
        out_shape=[jax.ShapeDtypeStruct((B, S, nq), F32),
                   jax.ShapeDtypeStruct((B, S, nh), F32),
                   jax.ShapeDtypeStruct((B, ng, S), F32)],
        compiler_params=_cparams(("parallel", "parallel")),
        name="in_proj",
    )(x, sc1.reshape(B, 1, D), sh1.reshape(B, 1, D), g_mix.reshape(1, D), wq, bq, wh, bh, wgt, bgt)


def _short_conv_chunk(src_ref, lead, c, nc, w_ref, b_ref, rows):
    L = rows.shape[0]
    S = nc * L
    t0 = pl.multiple_of(c * L, L)
    cur = src_ref[lead, pl.ds(t0, L), :]
    p0 = pl.multiple_of(jnp.maximum(t0 - SUBLANES, 0), SUBLANES)
    n0 = pl.multiple_of(jnp.minimum(t0 + L, S - SUBLANES), SUBLANES)
    prev_row = src_ref[lead, pl.ds(p0, SUBLANES), :][SUBLANES - 1:SUBLANES, :]
    next_row = src_ref[lead, pl.ds(n0, SUBLANES), :][0:1, :]
    prev_row = jnp.where(c > 0, prev_row, 0.0)
    next_row = jnp.where(c < nc - 1, next_row, 0.0)
    up = jnp.where(rows == 0, prev_row, pltpu.roll(cur, 1, 0))
    dn = jnp.where(rows == L - 1, next_row, pltpu.roll(cur, L - 1, 0))
    return b_ref[...] + up * w_ref[0:1, :] + cur * w_ref[1:2, :] + dn * w_ref[2:3, :]


def _mlstm_kernel(q_ref, k_ref, v_ref, o_ref, gt_ref, wq_ref, wk_ref, bq_ref, bk_ref, ng_ref,
                  y_ref, qt_ref, kb_ref, vt_ref, c_ref, cl_ref, ld_ref, st_ref, pc_ref, cs_ref, kq_ref, *, nc):
    L = MLSTM_CHUNK
    rows = lax.broadcasted_iota(jnp.int32, (L, LANES), 0)
    cols = lax.broadcasted_iota(jnp.int32, (L, LANES), 1)
    pad = 2 * SUBLANES
    piece_row = lax.broadcasted_iota(jnp.int32, (SUBLANES, LANES), 0)
    k_scale = HEAD_DIM ** -0.5

    def conv_body(c, carry):
        t0 = pl.multiple_of(c * L, L)
        yq = _short_conv_chunk(q_ref, 0, c, nc, wq_ref, bq_ref, rows)
        qt_ref[c] = (yq * _sigmoid(yq)).T.astype(BF16)
        yk = _short_conv_chunk(k_ref, 0, c, nc, wk_ref, bk_ref, rows)
        kb_ref[pl.ds(t0, L), :] = (yk * _sigmoid(yk) * k_scale).astype(BF16)
        vt_ref[c] = v_ref[0, pl.ds(t0, L), :].T
        gates = gt_ref[0, 0, :, pl.ds(t0, L)]
        lf = _log_sigmoid(jnp.where(piece_row < 3, gates[1:2], gates[3:4]))
        hi = lf.astype(BF16).astype(F32)
        mid = (lf - hi).astype(BF16).astype(F32)
        kind = piece_row % 3
        pc_ref[pl.ds(pl.multiple_of(c * SUBLANES, SUBLANES), SUBLANES), :] = jnp.where(
            piece_row >= 6, 0.0, jnp.where(kind == 0, hi, jnp.where(kind == 1, mid, lf - hi - mid)))
        return carry

    lax.fori_loop(0, nc, conv_body, 0, unroll=2)

    pieces = pc_ref[...].astype(BF16)
    fwd_rows = lax.broadcasted_iota(jnp.int32, pc_ref.shape, 0) % SUBLANES < 3
    cs_ref[...] = jnp.where(fwd_rows, _dot(pieces, (rows <= cols).astype(BF16)),
                            _dot(pieces, (rows >= cols).astype(BF16)))

    ROW_B, ROW_MAX, ROW_G, ROW_A, ROW_N, ROW_M = range(6)

    def stat_tile(rows_by_index, base=None):
        tile = jnp.zeros((SUBLANES, LANES), F32) if base is None else base
        for r, value in rows_by_index.items():
            tile = jnp.where(piece_row == r, value, tile)
        return tile

    def local_pass(c, rev):
        t0 = pl.multiple_of(c * L, L)
        slot = 2 * c + (1 if rev else 0)
        kb = kb_ref[pl.ds(t0, L), :]
        gates = gt_ref[0, 0, :, pl.ds(t0, L)]
        i_row = gates[(2 if rev else 0):(3 if rev else 1)]
        cs = cs_ref[pl.ds(pl.multiple_of(c * SUBLANES, SUBLANES), SUBLANES), :]
        o = 3 if rev else 0
        b_row = cs[o:o + 1] + cs[o + 1:o + 2] + cs[o + 2:o + 3]
        keep = (rows >= cols) if rev else (rows <= cols)
        g = b_row[:, 0:1] if rev else b_row[:, L - 1:L]
        w_tile = jnp.broadcast_to(i_row - b_row, (L, LANES)).T
        log_d = jnp.where(keep, w_tile + b_row, NEG_BIG)
        ld_ref[slot] = log_d
        w_end = g - b_row + i_row
        a = jnp.max(w_end, axis=1, keepdims=True)
        e_end = jnp.exp(w_end - a)
        ve = jnp.concatenate([vt_ref[c] * e_end, jnp.broadcast_to(e_end, (pad, L))], axis=0).astype(BF16)
        cn = _dot(ve, kb)
        cl_ref[slot] = cn[:HEAD_DIM]
        st_ref[slot] = stat_tile({ROW_B: b_row, ROW_MAX: jnp.max(log_d, axis=0, keepdims=True), ROW_G: g,
                                  ROW_A: a, ROW_N: cn[HEAD_DIM:HEAD_DIM + 1]})

    def local_body(c, carry):
        kq_ref[c] = _dot(kb_ref[pl.ds(pl.multiple_of(c * L, L), L), :], qt_ref[c])
        local_pass(c, False)
        local_pass(c, True)
        return carry

    lax.fori_loop(0, nc, local_body, 0, unroll=8)

    def scan_step(c, rev, n, m):
        slot = 2 * c + (1 if rev else 0)
        side = 1 if rev else 0
        stats = st_ref[slot]
        g = stats[ROW_G:ROW_G + 1]
        a = stats[ROW_A:ROW_A + 1]
        m_new = jnp.maximum(g + m, a)
        s_prev = jnp.exp(g + m - m_new)
        s_loc = jnp.exp(a - m_new)
        c_in = c_ref[side]
        c_ref[side] = s_prev * c_in + s_loc * cl_ref[slot]
        cl_ref[slot] = c_in
        n_new = s_prev * n + s_loc * stats[ROW_N:ROW_N + 1]
        st_ref[slot] = stat_tile({ROW_N: n, ROW_M: m}, base=stats)
        return n_new, m_new

    c_ref[...] = jnp.zeros_like(c_ref)
    n0 = jnp.zeros((1, HEAD_DIM), F32)
    m0 = jnp.full((1, LANES), MLSTM_M_INIT, F32)

    def scan_body(j, carry):
        n_f, m_f, n_b, m_b = carry
        n_f, m_f = scan_step(j, False, n_f, m_f)
        n_b, m_b = scan_step(nc - 1 - j, True, n_b, m_b)
        return n_f, m_f, n_b, m_b

    lax.fori_loop(0, nc, scan_body, (n0, m0, n0, m0))

    def output_pass(c, rev):
        t0 = pl.multiple_of(c * L, L)
        slot = 2 * c + (1 if rev else 0)
        qt = qt_ref[c]
        stats = st_ref[slot]
        log_inter = stats[ROW_B:ROW_B + 1] + stats[ROW_M:ROW_M + 1]
        m_t = jnp.maximum(log_inter, stats[ROW_MAX:ROW_MAX + 1])
        d = jnp.exp(ld_ref[slot] - m_t)
        e_inter = jnp.exp(log_inter - m_t)
        s = (kq_ref[c] * d).astype(BF16)
        vn = _dot(jnp.concatenate([vt_ref[c].astype(BF16), jnp.ones((pad, L), BF16)], axis=0), s)
        n_in = stats[ROW_N:ROW_N + 1]
        cq = _dot(jnp.concatenate([cl_ref[slot], jnp.broadcast_to(n_in, (pad, LANES))], axis=0).astype(BF16), qt)
        den = vn[HEAD_DIM:HEAD_DIM + 1] + e_inter * cq[HEAD_DIM:HEAD_DIM + 1]
        scale = 1.0 / jnp.maximum(jnp.abs(den), jnp.exp(-m_t))
        return ((vn[:HEAD_DIM] + e_inter * cq[:HEAD_DIM]) * scale).T

    def output_body(c, carry):
        t0 = pl.multiple_of(c * L, L)
        hs = output_pass(c, False) + output_pass(c, True)
        y_ref[0, pl.ds(t0, L), :] = _sigmoid(o_ref[0, pl.ds(t0, L), :]) * _rms(hs, ng_ref[...])
        return carry

    lax.fori_loop(0, nc, output_body, 0, unroll=8)


def _mlstm(qkvo, gates_t, conv_w, conv_b, norm_g):
    B, S, _ = qkvo.shape
    H, Dh = MLSTM_HEADS, HEAD_DIM
    nc = S // MLSTM_CHUNK
    seq = lambda off: pl.BlockSpec((1, S, Dh), lambda b, h: (b, 0, off + h))
    vec = lambda r, off: pl.BlockSpec((r, Dh), lambda b, h: (0, off + h))
    return pl.pallas_call(
        functools.partial(_mlstm_kernel, nc=nc),
        grid=(B, H),
        in_specs=[seq(0), seq(H), seq(2 * H), seq(3 * H),
                  pl.BlockSpec((1, 1, SUBLANES, S), lambda b, h: (b, h, 0, 0)),
                  vec(3, 0), vec(3, H), vec(1, 0), vec(1, H), vec(1, 0)],
        out_specs=pl.BlockSpec((1, S, Dh), lambda b, h: (b, 0, h)),
        out_shape=jax.ShapeDtypeStruct((B, S, H * Dh), F32),
        scratch_shapes=[pltpu.VMEM((nc, Dh, MLSTM_CHUNK), BF16), pltpu.VMEM((S, Dh), BF16),
                        pltpu.VMEM((nc, Dh, MLSTM_CHUNK), F32), pltpu.VMEM((2, Dh, Dh), F32),
                        pltpu.VMEM((2 * nc, Dh, Dh), F32), pltpu.VMEM((2 * nc, MLSTM_CHUNK, MLSTM_CHUNK), F32),
                        pltpu.VMEM((2 * nc, SUBLANES, LANES), F32), pltpu.VMEM((nc * SUBLANES, LANES), F32),
                        pltpu.VMEM((nc * SUBLANES, LANES), F32), pltpu.VMEM((nc, MLSTM_CHUNK, MLSTM_CHUNK), F32)],
        compiler_params=_cparams(("parallel", "arbitrary")),
        name="mlstm",
    )(qkvo, qkvo, qkvo, qkvo, gates_t.reshape(B, H, SUBLANES, S), conv_w, conv_w, conv_b.reshape(1, -1), conv_b.reshape(1, -1),
      norm_g.reshape(1, -1))


FILTER_PASSES = 3
FFT_UNROLL = 16


def _hilo(m):
    m32 = jnp.asarray(m, F32)
    hi = m32.astype(BF16)
    return hi, (m32 - hi.astype(F32)).astype(BF16)


def _stack_complex(m):
    return np.block([[m.real, -m.imag], [m.imag, m.real]])


def _dft_constants(S):
    N = 2 * S
    N2 = FFT_N2
    N1 = N // N2
    k1 = np.arange(N1)
    n2 = np.arange(N2)
    f1 = np.exp(-2j * np.pi * np.outer(k1, np.arange(N1)) / N1)
    f2 = np.exp(-2j * np.pi * np.outer(n2, n2) / N2)
    tw = np.exp(-2j * np.pi * np.outer(k1, n2) / N)
    g1 = np.conj(f1).T[:N1 // 2] / N
    consts = dict(
        f1_real=_hilo(np.concatenate([f1.real, f1.imag], axis=0)),
        f1_half=_hilo(_stack_complex(f1[:, :N1 // 2])),
        f2=_hilo(_stack_complex(f2)),
        g1=_hilo(_stack_complex(g1)),
        f2_tw=_hilo(np.stack([_stack_complex(f2 * tw[k][None, :]) for k in k1]))[0],
        f2_tw_inv=_hilo(np.stack([_stack_complex(np.conj(tw[k])[:, None] * np.conj(f2)) for k in k1]))[0],
        tw_re=jnp.asarray(np.broadcast_to(tw.real[:, :, None], (N1, N2, LANES)), F32),
        tw_im=jnp.asarray(np.broadcast_to(tw.imag[:, :, None], (N1, N2, LANES)), F32),
    )
    return N1, N2, consts


def _hy_hidden_kernel(w1t_ref, w1c_ref, w1s_ref, b1_ref, w2_ref, b2_ref, fr_ref, o_ref, *, S, T):
    j = pl.program_id(0) * T + lax.broadcasted_iota(jnp.int32, (T, 1), 0)
    p = jnp.where(j < S, j, 2 * S - j).astype(F32)
    t = p / (S - 1)
    w = (2.0 * math.pi) * p / S
    band = lax.broadcasted_iota(jnp.int32, (1, HYENA_BANDS), 1).astype(F32)
    bands = 1e-4 + band * ((HYENA_BANDS - 1 - 1e-4) / (HYENA_BANDS - 1))
    arg = bands * w
    pre = t * w1t_ref[...] + _dot3(jnp.cos(arg), w1c_ref[...]) + _dot3(-jnp.sin(arg), w1s_ref[...]) + b1_ref[...]
    hid = jnp.sin(fr_ref[...] * pre)
    o_ref[...] = jnp.sin(fr_ref[...] * (_dot3(hid, w2_ref[...]) + b2_ref[...]))


def _hy_hidden(S, w1, b1, w2, b2, freq):
    N = 2 * S
    T = 1024
    Hd = w2.shape[0]
    full = lambda a: pl.BlockSpec(a.shape, lambda i: (0,) * a.ndim)
    args = (w1[0:1], w1[1:1 + HYENA_BANDS], w1[1 + HYENA_BANDS:], b1.reshape(1, Hd), w2, b2.reshape(1, Hd),
            freq.reshape(1, Hd))
    return pl.pallas_call(
        functools.partial(_hy_hidden_kernel, S=S, T=T),
        grid=(N // T,),
        in_specs=[full(a) for a in args],
        out_specs=pl.BlockSpec((T, Hd), lambda i: (i, 0)),
        out_shape=jax.ShapeDtypeStruct((N, Hd), F32),
        compiler_params=_cparams(("parallel",)),
        name="hyena_hidden",
    )(*args)


def _fft_stage2(a_re, a_im, k1, twr_ref, twi_ref, f2_hi, f2_lo, passes):
    r0 = pl.multiple_of(k1 * FFT_PITCH, SUBLANES)
    ar = a_re[pl.ds(r0, FFT_N2), :]
    ai = a_im[pl.ds(r0, FFT_N2), :]
    twr = twr_ref[k1]
    twi = twi_ref[k1]
    t = jnp.concatenate([ar * twr - ai * twi, ar * twi + ai * twr], axis=0)
    return _dotc(f2_hi, f2_lo, t, passes)


def _hy_filter_kernel(hid_ref, w3f_ref, w3b_ref, df_ref, db_ref, bias_ref, f1h_ref, f1l_ref, f2h_ref, f2l_ref,
                      twr_ref, twi_ref, kr_ref, ki_ref, u_ref, a_re, a_im, *, S, N1):
    N2, P = FFT_N2, FFT_PITCH
    T = 512
    nb = T // N2

    def gen_body(i, carry):
        r0 = pl.multiple_of(i * T, T)
        hid = hid_ref[pl.ds(r0, T), :]
        j = r0 + lax.broadcasted_iota(jnp.int32, (T, 1), 0)
        t = jnp.where(j < S, j, 2 * S - j).astype(F32) / (S - 1)
        kf = _dot3(hid, w3f_ref[...]) * (jnp.exp(-t * jnp.abs(df_ref[...])) + HYENA_WINDOW_SHIFT)
        kb = _dot3(hid, w3b_ref[...]) * (jnp.exp(-t * jnp.abs(db_ref[...])) + HYENA_WINDOW_SHIFT)
        ker = (jnp.where(j < S, kf, 0.0) + jnp.where((j > S) | (j == 0), kb, 0.0)
               + jnp.where(j == 0, bias_ref[0], 0.0))
        for b in range(nb):
            u_ref[pl.ds(pl.multiple_of((i * nb + b) * P, SUBLANES), N2), :] = ker[b * N2:(b + 1) * N2]
        return carry

    lax.fori_loop(0, 2 * S // T, gen_body, 0)

    def stage1(n2, carry):
        x = u_ref[pl.ds(n2, N1, stride=P), :]
        a = _dotc(f1h_ref[...], f1l_ref[...], x, FILTER_PASSES)
        a_re[pl.ds(n2, N1, stride=P), :] = a[:N1]
        a_im[pl.ds(n2, N1, stride=P), :] = a[N1:]
        return carry

    lax.fori_loop(0, N2, stage1, 0, unroll=FFT_UNROLL)

    def stage2(k1, carry):
        x = _fft_stage2(a_re, a_im, k1, twr_ref, twi_ref, f2h_ref[...], f2l_ref[...], FILTER_PASSES)
        r0 = pl.multiple_of(k1 * N2, N2)
        kr_ref[0, pl.ds(r0, N2), :] = x[:N2]
        ki_ref[0, pl.ds(r0, N2), :] = x[N2:]
        return carry

    lax.fori_loop(0, N1, stage2, 0, unroll=2 * FFT_UNROLL)


def _hy_filter_spectrum(S, hid, w3, deltas, bias, N1, consts):
    N = 2 * S
    N2, P = FFT_N2, FFT_PITCH
    Hd = hid.shape[1]
    nblk = w3.shape[1] // (2 * HYENA_ORDER * LANES)
    f1h, f1l = consts["f1_real"]
    f2h, f2l = consts["f2"]
    full = lambda a: pl.BlockSpec(a.shape, lambda o, j: (0,) * a.ndim, pipeline_mode=pl.Buffered(1))
    colf = lambda o, j: (0, (2 * o) * nblk + j)
    colb = lambda o, j: (0, (2 * o + 1) * nblk + j)
    out = jax.ShapeDtypeStruct((HYENA_ORDER, N, nblk * LANES), F32)
    return pl.pallas_call(
        functools.partial(_hy_filter_kernel, S=S, N1=N1),
        grid=(HYENA_ORDER, nblk),
        in_specs=[full(hid),
                  pl.BlockSpec((Hd, LANES), colf), pl.BlockSpec((Hd, LANES), colb),
                  pl.BlockSpec((1, LANES), colf), pl.BlockSpec((1, LANES), colb),
                  pl.BlockSpec((1, 1, LANES), lambda o, j: (o, 0, j)),
                  full(f1h), full(f1l), full(f2h), full(f2l), full(consts["tw_re"]), full(consts["tw_im"])],
        out_specs=[pl.BlockSpec((1, N, LANES), lambda o, j: (o, 0, j))] * 2,
        out_shape=[out, out],
        scratch_shapes=[pltpu.VMEM((N1 * P, LANES), F32)] * 3,
        compiler_params=_cparams(("parallel", "parallel")),
        name="hyena_filter_spectrum",
    )(hid, w3, w3, deltas.reshape(1, -1), deltas.reshape(1, -1), bias.reshape(HYENA_ORDER, 1, -1),
      f1h, f1l, f2h, f2l, consts["tw_re"], consts["tw_im"])


def _hy_conv_kernel(u_ref, x_ref, kr_ref, ki_ref, wu_ref, bu_ref, wx_ref, bx_ref, ng_ref,
                    f1_ref, g1_ref, mf_ref, mi_ref, o_ref, u_re, u_im, a_re, a_im, *, N1, conv_u, last):
    N2, P = FFT_N2, FFT_PITCH
    nh = N1 // 2
    rows = lax.broadcasted_iota(jnp.int32, (N2, LANES), 0)

    def load_body(c, carry):
        r0 = pl.multiple_of(c * P, SUBLANES)
        for b, dst in ((0, u_re), (1, u_im)):
            if conv_u:
                dst[pl.ds(r0, N2), :] = _short_conv_chunk(u_ref, b, c, nh, wu_ref, bu_ref, rows)
            else:
                dst[pl.ds(r0, N2), :] = u_ref[b, pl.ds(pl.multiple_of(c * N2, N2), N2), :]
        return carry

    lax.fori_loop(0, nh, load_body, 0, unroll=4)

    def stage1(n2, carry):
        x = jnp.concatenate([u_re[pl.ds(n2, nh, stride=P), :], u_im[pl.ds(n2, nh, stride=P), :]], axis=0)
        a = _dot(f1_ref[...], x.astype(BF16))
        a_re[pl.ds(n2, N1, stride=P), :] = a[:N1]
        a_im[pl.ds(n2, N1, stride=P), :] = a[N1:]
        return carry

    lax.fori_loop(0, N2, stage1, 0, unroll=FFT_UNROLL)

    def stage2(k1, carry):
        r0 = pl.multiple_of(k1 * P, SUBLANES)
        a = jnp.concatenate([a_re[pl.ds(r0, N2), :], a_im[pl.ds(r0, N2), :]], axis=0)
        x = _dot(mf_ref[k1], a.astype(BF16))
        s0 = pl.multiple_of(k1 * N2, N2)
        kr = kr_ref[0, pl.ds(s0, N2), :]
        ki = ki_ref[0, pl.ds(s0, N2), :]
        xr, xi = x[:N2], x[N2:]
        y = jnp.concatenate([xr * kr - xi * ki, xr * ki + xi * kr], axis=0)
        b = _dot(mi_ref[k1], y.astype(BF16))
        a_re[pl.ds(r0, N2), :] = b[:N2]
        a_im[pl.ds(r0, N2), :] = b[N2:]
        return carry

    lax.fori_loop(0, N1, stage2, 0, unroll=2 * FFT_UNROLL)

    def inv2(n2, carry):
        b = jnp.concatenate([a_re[pl.ds(n2, N1, stride=P), :], a_im[pl.ds(n2, N1, stride=P), :]], axis=0)
        y = _dot(g1_ref[...], b.astype(BF16))
        u_re[pl.ds(n2, nh, stride=P), :] = y[:nh]
        u_im[pl.ds(n2, nh, stride=P), :] = y[nh:]
        return carry

    lax.fori_loop(0, N2, inv2, 0, unroll=FFT_UNROLL)

    def out_body(c, carry):
        r0 = pl.multiple_of(c * P, SUBLANES)
        t0 = pl.multiple_of(c * N2, N2)
        for b, src in ((0, u_re), (1, u_im)):
            z = _short_conv_chunk(x_ref, b, c, nh, wx_ref, bx_ref, rows) * src[pl.ds(r0, N2), :]
            o_ref[b, pl.ds(t0, N2), :] = _rms(z, ng_ref[...]) if last else z
        return carry

    lax.fori_loop(0, nh, out_body, 0, unroll=4)


def _hy_conv(u, u_blk, x, x_blk, kr, ki, order, conv_w, conv_b, norm_g, N1, consts, conv_u, last):
    B, S, _ = u.shape
    N2, P = FFT_N2, FFT_PITCH
    N = 2 * S
    nblk = kr.shape[2] // LANES
    pair = 2
    single = pl.Buffered(1)
    full = lambda a: pl.BlockSpec(a.shape, lambda j, p: (0,) * a.ndim, pipeline_mode=single)
    seq = lambda off: pl.BlockSpec((pair, S, LANES), lambda j, p: (p, 0, off + j))
    vec = lambda r, off: pl.BlockSpec((r, LANES), lambda j, p: (0, off + j))
    spec = pl.BlockSpec((1, N, LANES), lambda j, p: (order, 0, j), pipeline_mode=single)
    mats = [consts["f1_half"][0], consts["g1"][0], consts["f2_tw"], consts["f2_tw_inv"]]
    return pl.pallas_call(
        functools.partial(_hy_conv_kernel, N1=N1, conv_u=conv_u, last=last),
        grid=(nblk, B // pair),
        in_specs=[seq(u_blk), seq(x_blk), spec, spec,
                  vec(3, u_blk if conv_u else 0), vec(1, u_blk if conv_u else 0), vec(3, x_blk), vec(1, x_blk),
                  pl.BlockSpec((1, LANES), lambda j, p: (0, j))] + [full(m) for m in mats],
        out_specs=pl.BlockSpec((pair, S, LANES), lambda j, p: (p, 0, j)),
        out_shape=jax.ShapeDtypeStruct((B, S, nblk * LANES), F32),
        scratch_shapes=[pltpu.VMEM((N1 // 2 * P, LANES), F32)] * 2 + [pltpu.VMEM((N1 * P, LANES), F32)] * 2,
        compiler_params=_cparams(("parallel", "parallel")),
        name=f"hyena_conv{order}",
    )(u, x, kr, ki, conv_w, conv_b.reshape(1, -1), conv_w, conv_b.reshape(1, -1), norm_g.reshape(1, -1), *mats)


def _hyena(hx, conv_w, conv_b, w1, b1, w2, b2, w3, freq, deltas, bias, norm_g):
    B, S, C3 = hx.shape
    nblk = C3 // (HYENA_ORDER + 1) // LANES
    N1, N2, consts = _dft_constants(S)
    hid = _hy_hidden(S, w1, b1, w2, b2, freq)
    kr, ki = _hy_filter_spectrum(S, hid, w3, deltas, bias, N1, consts)
    z1 = _hy_conv(hx, 2 * nblk, hx, 0, kr, ki, 0, conv_w, conv_b, norm_g, N1, consts, True, False)
    return _hy_conv(z1, 0, hx, nblk, kr, ki, 1, conv_w, conv_b, norm_g, N1, consts, False, True)


def _outproj_kernel(ym_ref, yh_ref, x_ref, gt_ref, sc_ref, sh_ref, g_ref, wm_ref, wh_ref, wr_ref,
                    x1_ref, hf_ref, aff_ref):
    mixed = _dot(ym_ref[0].astype(BF16), wm_ref[...]) + _dot(yh_ref[0].astype(BF16), wh_ref[...])
    x1 = x_ref[0] + gt_ref[0] * mixed
    x1_ref[0] = x1
    hf = _rms(x1, g_ref[...]) * (1.0 + sc_ref[0]) + sh_ref[0]
    hf_ref[0] = hf.astype(BF16)
    logits = _dot3_nt(wr_ref[...], hf)
    e = jnp.exp(logits - jnp.max(logits, axis=0, keepdims=True))
    aff_ref[0] = e / jnp.sum(e, axis=0, keepdims=True)


def _out_projection(y_m, y_h, x, gt1, sc2, sh2, g_ffn, w_out, w_router, ts=512):
    B, S, D = x.shape
    Wm = y_m.shape[2]
    E = w_router.shape[1]
    wm = w_out[:Wm].astype(BF16)
    wh = w_out[Wm:].astype(BF16)
    tile = lambda w: pl.BlockSpec((1, ts, w), lambda b, i: (b, i, 0))
    row = pl.BlockSpec((1, 1, D), lambda b, i: (b, 0, 0))
    full = lambda a: pl.BlockSpec(a.shape, lambda b, i: (0,) * a.ndim)
    wr = w_router.T
    g = g_ffn.reshape(1, D)
    return pl.pallas_call(
        _outproj_kernel,
        grid=(B, S // ts),
        in_specs=[tile(Wm), tile(y_h.shape[2]), tile(D), row, row, row, full(g), full(wm), full(wh), full(wr)],
        out_specs=[tile(D), tile(D), pl.BlockSpec((1, E, ts), lambda b, i: (b, 0, i))],
        out_shape=[jax.ShapeDtypeStruct((B, S, D), F32), jax.ShapeDtypeStruct((B, S, D), BF16),
                   jax.ShapeDtypeStruct((B, E, S), F32)],
        compiler_params=_cparams(("parallel", "parallel")),
        name="out_proj_router",
    )(y_m, y_h, x, gt1.reshape(B, 1, D), sc2.reshape(B, 1, D), sh2.reshape(B, 1, D), g, wm, wh, wr)


def _route_kernel(aff_ref, pos_ref, cnt_ref, *, S, cap):
    aff = aff_ref[0]
    E = aff.shape[0]
    tpos = lax.broadcasted_iota(jnp.int32, (E, S), 1)
    count = lambda mask: jnp.sum(jnp.where(mask, 1.0, 0.0), axis=1, keepdims=True)
    as_float = lambda word: lax.bitcast_convert_type(word, F32)

    def value_step(i, prefix):
        cand = prefix | jnp.left_shift(1, 30 - i)
        return jnp.where(count(aff >= as_float(cand)) >= cap, cand, prefix)

    thresh = as_float(lax.fori_loop(0, 31, value_step, jnp.zeros((E, 1), jnp.int32)))
    gt = aff > thresh
    eq = aff == thresh
    need = cap - count(gt)
    nbits = S.bit_length() - 1

    def index_step(i, x):
        cand = x | jnp.left_shift(1, nbits - 1 - i)
        return jnp.where(count(eq & (tpos < cand)) < need, cand, x)

    last_tie = lax.fori_loop(0, nbits, index_step, jnp.zeros((E, 1), jnp.int32))
    sel = jnp.where(gt | (eq & (tpos <= last_tie)), 1.0, 0.0)
    before = (lax.broadcasted_iota(jnp.int32, (LANES, LANES), 0)
              < lax.broadcasted_iota(jnp.int32, (LANES, LANES), 1)).astype(BF16)
    carry = jnp.zeros((E, 1), F32)
    lane = lax.broadcasted_iota(jnp.int32, (E, LANES), 1)
    starts = jnp.zeros((E, LANES), F32)
    for c in range(S // LANES):
        sc = sel[:, c * LANES:(c + 1) * LANES]
        rank = _dot(sc.astype(BF16), before) + carry
        pos_ref[0, :, c * LANES:(c + 1) * LANES] = jnp.where(sc > 0.0, rank, -1.0).astype(jnp.int32)
        starts = jnp.where(lane == c, carry, starts)
        carry = carry + jnp.sum(sc, axis=1, keepdims=True)
    cnt_ref[0] = jnp.where(lane >= S // LANES, carry, starts).astype(jnp.int32)


def _route(aff_t, cap):
    B, E, S = aff_t.shape
    assert S // LANES < LANES
    R = B * E
    blk = pl.BlockSpec((1, R, S), lambda i: (0, 0, 0))
    pos, cnt = pl.pallas_call(
        functools.partial(_route_kernel, S=S, cap=cap),
        grid=(1,),
        in_specs=[blk],
        out_specs=[blk, pl.BlockSpec((1, R, LANES), lambda i: (0, 0, 0))],
        out_shape=[jax.ShapeDtypeStruct((1, R, S), jnp.int32), jax.ShapeDtypeStruct((1, R, LANES), jnp.int32)],
        compiler_params=_cparams(("arbitrary",)),
        name="route_topk",
    )(aff_t.reshape(1, R, S))
    return pos.reshape(B, E, S), cnt.reshape(B, E, LANES)


def _window_tables(cnt, S, cap):
    B, E, _ = cnt.shape
    nch = S // LANES
    bounds = cnt[:, :, 1:nch + 1]
    first = jnp.arange(cap // GATHER_SLOTS, dtype=jnp.int32) * GATHER_SLOTS
    chunk_of = lambda slot: jnp.sum((bounds[:, :, None, :] <= slot[None, None, :, None]).astype(jnp.int32), axis=-1)
    per = GATHER_ALIGN // LANES
    step = GATHER_TOKENS // GATHER_ALIGN
    lo_a = chunk_of(first) // per
    hi_a = chunk_of(first + GATHER_SLOTS - 1) // per
    g_n = (hi_a - lo_a) // step + 1
    g_lo = jnp.minimum(lo_a, S // GATHER_ALIGN - step * g_n)
    edges = cnt[:, :, 0:nch + 1:SCATTER_TOKENS // LANES]
    lo, hi = edges[:, :, :-1], edges[:, :, 1:]
    s_lo = jnp.swapaxes(lo // SCATTER_SLOTS, 1, 2)
    s_n = jnp.swapaxes(jnp.where(hi > lo, (hi - 1) // SCATTER_SLOTS - lo // SCATTER_SLOTS + 1, 0), 1, 2)

    def flat_runs(starts, counts, stride, length):
        groups = counts.shape[-1]
        ends = jnp.cumsum(counts, axis=-1)
        j = jnp.arange(length, dtype=jnp.int32)
        grp = jnp.minimum(jnp.sum((ends[..., None, :] <= j[:, None]).astype(jnp.int32), axis=-1), groups - 1)
        chosen = grp[..., None] == jnp.arange(groups, dtype=jnp.int32)
        pick = lambda t: jnp.sum(jnp.where(chosen, t[..., None, :], 0), axis=-1)
        valid = j < ends[..., -1:]
        val = jnp.where(valid, pick(starts) + (j - pick(ends - counts)) * stride, 0)
        flat = lambda t: t.reshape(-1).astype(jnp.int32)
        return flat(jnp.where(valid, grp, 0)), flat(val), flat(ends[..., -1])

    return flat_runs(g_lo, g_n, step, GATHER_LIST) + flat_runs(s_lo, s_n, 1, SCATTER_LIST)


META_ROWS = 16


GATHER_SLOTS = 128
GATHER_TOKENS = 512
GATHER_ALIGN = 256
GATHER_LIST = 12
SCATTER_SLOTS = 256
SCATTER_TOKENS = 512
SCATTER_LIST = N_EXPERTS * ((SCATTER_TOKENS - 1) // SCATTER_SLOTS + 2)


def _gather_kernel(blk_ref, win_ref, cnt_ref, pos_ref, aff_ref, hf_ref, xs_ref, meta_ref, acc_ref, macc_ref):
    R, tk = GATHER_SLOTS, GATHER_TOKENS
    base = (pl.program_id(0) * pl.num_programs(1) + pl.program_id(1)) * GATHER_LIST
    count = cnt_ref[pl.program_id(0) * pl.num_programs(1) + pl.program_id(1)]
    mrow = lax.broadcasted_iota(jnp.int32, (META_ROWS, tk), 0)
    lane = lax.broadcasted_iota(jnp.int32, (1, tk), 1)
    srow = lax.broadcasted_iota(jnp.int32, (R, tk), 0)
    acc_ref[...] = jnp.zeros_like(acc_ref)
    macc_ref[...] = jnp.zeros_like(macc_ref)

    def contribution(j):
        r0 = pl.multiple_of(blk_ref[base + j] * R, R)
        t0 = pl.multiple_of(win_ref[base + j] * GATHER_ALIGN, GATHER_ALIGN)
        first_slot = jnp.where(j < count, r0, -(R + 1))
        onehot = jnp.where(first_slot + srow == pos_ref[0, 0, :, pl.ds(t0, tk)], 1.0, 0.0).astype(BF16)
        a = aff_ref[0, 0, :, pl.ds(t0, tk)]
        hi = a.astype(BF16).astype(F32)
        mid = (a - hi).astype(BF16).astype(F32)
        lo = (a - hi - mid).astype(BF16).astype(F32)
        t = t0 + lane
        pieces = (hi, mid, lo, (t // 64).astype(F32), (t % 64).astype(F32))
        meta = jnp.zeros((META_ROWS, tk), F32)
        for r, piece in enumerate(pieces):
            meta = jnp.where(mrow == r, piece, meta)
        return r0, _dot(onehot, hf_ref[0, pl.ds(t0, tk), :]), _dot_nt(meta.astype(BF16), onehot)

    def pair_body(i, carry):
        for r0, rows, meta in (contribution(2 * i), contribution(2 * i + 1)):
            acc_ref[pl.ds(r0, R), :] += rows
            macc_ref[:, pl.ds(r0, R)] += meta
        return carry

    lax.fori_loop(0, (count + 1) // 2, pair_body, 0)
    xs_ref[0, 0] = acc_ref[...].astype(BF16)
    meta_ref[0, 0] = macc_ref[...]


def _gather(g_blk, g_win, g_cnt, pos, aff_t, hf, cap):
    B, E, S = pos.shape
    D = hf.shape[2]
    most = cap // GATHER_SLOTS + (S // GATHER_ALIGN - 1) // (GATHER_TOKENS // GATHER_ALIGN)
    assert most <= GATHER_LIST and GATHER_LIST % 2 == 0
    row = pl.BlockSpec((1, 1, 1, S), lambda b, e, *_: (b, e, 0, 0))
    return pl.pallas_call(
        _gather_kernel,
        grid_spec=pltpu.PrefetchScalarGridSpec(
            num_scalar_prefetch=3,
            grid=(B, E),
            in_specs=[row, row, pl.BlockSpec((1, S, D), lambda b, e, *_: (b, 0, 0))],
            out_specs=[pl.BlockSpec((1, 1, cap, D), lambda b, e, *_: (b, e, 0, 0)),
                       pl.BlockSpec((1, 1, META_ROWS, cap), lambda b, e, *_: (b, e, 0, 0))],
            scratch_shapes=[pltpu.VMEM((cap, D), F32), pltpu.VMEM((META_ROWS, cap), F32)]),
        out_shape=[jax.ShapeDtypeStruct((B, E, cap, D), BF16), jax.ShapeDtypeStruct((B, E, META_ROWS, cap), F32)],
        compiler_params=_cparams(("parallel", "arbitrary")),
        name="moe_gather",
    )(g_blk, g_win, g_cnt, pos.reshape(B, E, 1, S), aff_t.reshape(B, E, 1, S), hf)


def _ffn_kernel(xs_ref, meta_ref, wg_ref, wu_ref, wd_ref, y_ref, wdb_ref, *, tf):
    x = xs_ref[0, 0]
    cap = x.shape[0]
    FF = wg_ref.shape[2]

    @pl.when(pl.program_id(1) == 0)
    def _():
        for f in range(FF // tf):
            wdb_ref[f * tf:(f + 1) * tf, :] = wd_ref[0, f * tf:(f + 1) * tf, :].astype(BF16)

    y = jnp.zeros((cap, wd_ref.shape[2]), F32)
    for f in range(FF // tf):
        hg = _dot(x, wg_ref[0, :, f * tf:(f + 1) * tf])
        hu = _dot(x, wu_ref[0, :, f * tf:(f + 1) * tf])
        hid = (hg * _sigmoid(hg) * hu).astype(BF16)
        y = y + _dot(hid, wdb_ref[f * tf:(f + 1) * tf, :])
    m = meta_ref[0, 0]
    g_row = m[0:1] + m[1:2] + m[2:3]
    eye = lax.broadcasted_iota(jnp.int32, (cap, cap), 0) == lax.broadcasted_iota(jnp.int32, (cap, cap), 1)
    g_col = jnp.sum(jnp.where(eye, g_row, 0.0), axis=1, keepdims=True)
    y_ref[0, 0] = (y * g_col).astype(BF16)


def _expert_ffn(xs, meta, w_gate, w_up, w_down, tf=512):
    B, E, cap, D = xs.shape
    FF = w_gate.shape[2]
    return pl.pallas_call(
        functools.partial(_ffn_kernel, tf=tf),
        grid=(E, B),
        in_specs=[pl.BlockSpec((1, 1, cap, D), lambda e, b: (b, e, 0, 0)),
                  pl.BlockSpec((1, 1, META_ROWS, cap), lambda e, b: (b, e, 0, 0)),
                  pl.BlockSpec((1, D, FF), lambda e, b: (e, 0, 0)),
                  pl.BlockSpec((1, D, FF), lambda e, b: (e, 0, 0)),
                  pl.BlockSpec((1, FF, D), lambda e, b: (e, 0, 0))],
        out_specs=pl.BlockSpec((1, 1, cap, D), lambda e, b: (b, e, 0, 0)),
        out_shape=jax.ShapeDtypeStruct((B, E, cap, D), BF16),
        scratch_shapes=[pltpu.VMEM((FF, D), BF16)],
        compiler_params=_cparams(("parallel", "arbitrary")),
        name="moe_ffn",
    )(xs, meta, w_gate.astype(BF16), w_up.astype(BF16), w_down)


def _scatter_kernel(e_ref, blk_ref, cnt_ref, y_ref, meta_ref, x1_ref, gt_ref, g_ref, o_ref, acc_ref):
    tt, R = SCATTER_TOKENS, SCATTER_SLOTS
    tile = pl.program_id(0) * pl.num_programs(1) + pl.program_id(1)
    tok = pl.program_id(1) * tt + lax.broadcasted_iota(jnp.int32, (tt, R), 0)
    acc_ref[...] = jnp.zeros_like(acc_ref)
    count = cnt_ref[tile]

    def contribution(j):
        e = e_ref[tile * SCATTER_LIST + j]
        r0 = pl.multiple_of(blk_ref[tile * SCATTER_LIST + j] * R, R)
        m = meta_ref[0, e, :, pl.ds(r0, R)]
        idx = jnp.where(j < count, (m[3:4] * 64.0 + m[4:5]).astype(jnp.int32), -1)
        onehot = jnp.where(tok == idx, 1.0, 0.0).astype(BF16)
        return _dot(onehot, y_ref[0, e, pl.ds(r0, R), :])

    def pair_body(i, carry):
        acc_ref[...] += contribution(2 * i) + contribution(2 * i + 1)
        return carry

    lax.fori_loop(0, (count + 1) // 2, pair_body, 0)
    o_ref[0] = _rms(x1_ref[0] + gt_ref[0] * acc_ref[...], g_ref[...])


def _scatter_final(s_e, s_blk, s_cnt, y, meta, x1, gt2, g_final):
    B, E, cap, D = y.shape
    S = x1.shape[1]
    tt = SCATTER_TOKENS
    tile = pl.BlockSpec((1, tt, D), lambda b, i, *_: (b, i, 0))
    return pl.pallas_call(
        _scatter_kernel,
        grid_spec=pltpu.PrefetchScalarGridSpec(
            num_scalar_prefetch=3,
            grid=(B, S // tt),
            in_specs=[pl.BlockSpec((1, E, cap, D), lambda b, i, *_: (b, 0, 0, 0)),
                      pl.BlockSpec((1, E, META_ROWS, cap), lambda b, i, *_: (b, 0, 0, 0)),
                      tile, pl.BlockSpec((1, 1, D), lambda b, i, *_: (b, 0, 0)),
                      pl.BlockSpec((1, D), lambda b, i, *_: (0, 0))],
            out_specs=tile,
            scratch_shapes=[pltpu.VMEM((tt, D), F32)]),
        out_shape=jax.ShapeDtypeStruct((B, S, D), F32),
        compiler_params=_cparams(("parallel", "parallel")),
        name="moe_scatter_final",
    )(s_e, s_blk, s_cnt, y, meta, x1, gt2.reshape(B, 1, D), g_final.reshape(1, D))


def kernel(x, c, w_ada, b_ada, g_mix, w_in, b_in, conv_qk_w, conv_qk_b, mlstm_norm_g, conv_hy_w, conv_hy_b,
           hy_w1, hy_b1, hy_w2, hy_b2, hy_w3, hy_freq, hy_deltas, hy_bias, hyena_norm_g, w_out, g_ffn,
           w_router, w_gate, w_up, w_down, g_final):
    S = x.shape[1]
    cap = EC_CAPACITY_FACTOR * S // N_EXPERTS
    assert w_ada.shape[0] == 1, "single-layer block: the final RMSNorm is fused into the MoE scatter"
    l = 0
    mod = _modulation(c, w_ada[l], b_ada[l])
    sh1, sc1, gt1, sh2, sc2, gt2 = jnp.split(mod, 6, axis=-1)
    qkvo, hx, gates_t = _in_projection(x, sc1, sh1, g_mix[l], w_in[l], b_in[l])
    y_m = _mlstm(qkvo, gates_t, conv_qk_w[l], conv_qk_b[l], mlstm_norm_g[l])
    y_h = _hyena(hx, conv_hy_w[l], conv_hy_b[l], hy_w1[l], hy_b1[l], hy_w2[l], hy_b2[l], hy_w3[l],
                 hy_freq[l], hy_deltas[l], hy_bias[l], hyena_norm_g[l])
    x1, hf, aff_t = _out_projection(y_m, y_h, x, gt1, sc2, sh2, g_ffn[l], w_out[l], w_router[l])
    pos, cnt = _route(aff_t, cap)
    g_blk, g_win, g_cnt, s_e, s_blk, s_cnt = _window_tables(cnt, S, cap)
    xs, meta = _gather(g_blk, g_win, g_cnt, pos, aff_t, hf, cap)
    y = _expert_ffn(xs, meta, w_gate[l], w_up[l], w_down[l])
    return _scatter_final(s_e, s_blk, s_cnt, y, meta, x1, gt2, g_final)
```

```python
import functools
import math

import numpy as np
import jax
import jax.numpy as jnp
from jax import lax
from jax.experimental import pallas as pl
from jax.experimental.pallas import tpu as pltpu

F32 = jnp.float32
BF16 = jnp.bfloat16

MLSTM_HEADS = 4
HEAD_DIM = 128
MLSTM_CHUNK = 128
MLSTM_M_INIT = -1e30
HYENA_GROUP_DIM = 128
HYENA_ORDER = 2
HYENA_BANDS = 16
HYENA_WINDOW_SHIFT = 0.05
N_GATE_COLS = 4 * MLSTM_HEADS
N_EXPERTS = 16
EC_CAPACITY_FACTOR = 2
RMS_EPS = 1e-6

LANES = 128
SUBLANES = 8
FFT_N2 = 64
FFT_PITCH = 72
NEG_BIG = -1e30
VMEM_LIMIT = 56 * 1024 * 1024


def _cparams(sem, vmem=None, flags=None):
    return pltpu.CompilerParams(dimension_semantics=sem, vmem_limit_bytes=vmem or VMEM_LIMIT, flags=flags)


def _split(a):
    hi = a.astype(BF16)
    lo = (a - hi.astype(F32)).astype(BF16)
    return hi, lo


def _dot(a, b):
    return jnp.dot(a, b, preferred_element_type=F32)


def _dot_nt(a, b):
    return lax.dot_general(a, b, (((1,), (1,)), ((), ())), preferred_element_type=F32)


def _dot3(a, b):
    ah, al = _split(a)
    bh, bl = _split(b)
    return _dot(ah, bh) + _dot(ah, bl) + _dot(al, bh)


def _dot3_nt(a, b):
    ah, al = _split(a)
    bh, bl = _split(b)
    return _dot_nt(ah, bh) + _dot_nt(ah, bl) + _dot_nt(al, bh)


def _dotc(w_hi, w_lo, x, passes):
    xh, xl = _split(x)
    out = _dot(w_hi, xh)
    if passes == 3:
        out = out + _dot(w_hi, xl) + _dot(w_lo, xh)
    return out


def _rms(x, g):
    return x * lax.rsqrt(jnp.mean(x * x, axis=-1, keepdims=True) + RMS_EPS) * g


def _sigmoid(x):
    return 1.0 / (1.0 + jnp.exp(-x))


def _log_sigmoid(x):
    return jnp.minimum(x, 0.0) - jnp.log(1.0 + jnp.exp(-jnp.abs(x)))


def _mod_kernel(c_ref, w_ref, b_ref, o_ref):
    o_ref[...] = _dot3(c_ref[...], w_ref[...]) + b_ref[...]


def _modulation(c, w_ada, b_ada):
    B, D = c.shape
    n = w_ada.shape[1]
    tn = 768
    return pl.pallas_call(
        _mod_kernel,
        grid=(n // tn,),
        in_specs=[pl.BlockSpec((B, D), lambda j: (0, 0)),
                  pl.BlockSpec((D, tn), lambda j: (0, j)),
                  pl.BlockSpec((1, tn), lambda j: (0, j))],
        out_specs=pl.BlockSpec((B, tn), lambda j: (0, j)),
        out_shape=jax.ShapeDtypeStruct((B, n), F32),
        compiler_params=_cparams(("parallel",)),
        name="adaln_mod",
    )(c, w_ada, b_ada.reshape(1, n))


def _inproj_kernel(x_ref, sc_ref, sh_ref, g_ref, wq_ref, bq_ref, wh_ref, bh_ref, wgt_ref, bgt_ref,
                   qkvo_ref, hx_ref, gt_ref):
    x = x_ref[0]
    h = _rms(x, g_ref[...]) * (1.0 + sc_ref[0]) + sh_ref[0]
    hb = h.astype(BF16)
    qkvo_ref[0] = _dot(hb, wq_ref[...]) + bq_ref[...]
    hx_ref[0] = _dot(hb, wh_ref[...]) + bh_ref[...]
    gt_ref[0] = _dot3_nt(wgt_ref[...], h) + bgt_ref[...]


def _in_projection(x, sc1, sh1, g_mix, w_in, b_in, ts=512):
    B, S, D = x.shape
    nq = 4 * MLSTM_HEADS * HEAD_DIM
    nh = w_in.shape[1] - nq - N_GATE_COLS
    wq = w_in[:, :nq].astype(BF16)
    wh = w_in[:, nq + N_GATE_COLS:].astype(BF16)
    bq = b_in[:nq].reshape(1, nq)
    bh = b_in[nq + N_GATE_COLS:].reshape(1, nh)
    ng = MLSTM_HEADS * SUBLANES
    pad = lambda t: jnp.pad(t.reshape(4, MLSTM_HEADS, -1).transpose(1, 0, 2), ((0, 0), (0, 4), (0, 0))).reshape(ng, -1)
    wgt = pad(w_in[:, nq:nq + N_GATE_COLS].T)
    bgt = pad(b_in[nq:nq + N_GATE_COLS].reshape(N_GATE_COLS, 1))
    const = lambda b, i: (0, 0)
    return pl.pallas_call(
        _inproj_kernel,
        grid=(B, S // ts),
        in_specs=[pl.BlockSpec((1, ts, D), lambda b, i: (b, i, 0)),
                  pl.BlockSpec((1, 1, D), lambda b, i: (b, 0, 0)),
                  pl.BlockSpec((1, 1, D), lambda b, i: (b, 0, 0)),
                  pl.BlockSpec((1, D), const),
                  pl.BlockSpec((D, nq), const), pl.BlockSpec((1, nq), const),
                  pl.BlockSpec((D, nh), const), pl.BlockSpec((1, nh), const),
                  pl.BlockSpec((ng, D), const), pl.BlockSpec((ng, 1), const)],
        out_specs=[pl.BlockSpec((1, ts, nq), lambda b, i: (b, i, 0)),
                   pl.BlockSpec((1, ts, nh), lambda b, i: (b, i, 0)),
                   pl.BlockSpec((1, ng, ts), lambda b, i: (b, 0, i))],
        out_shape=[jax.ShapeDtypeStruct((B, S, nq), F32),
                   jax.ShapeDtypeStruct((B, S, nh), F32),
                   jax.ShapeDtypeStruct((B, ng, S), F32)],
        compiler_params=_cparams(("parallel", "parallel")),
        name="in_proj",
    )(x, sc1.reshape(B, 1, D), sh1.reshape(B, 1, D), g_mix.reshape(1, D), wq, bq, wh, bh, wgt, bgt)


def _short_conv_chunk(src_ref, lead, c, nc, w_ref, b_ref, rows):
    L = rows.shape[0]
    S = nc * L
    t0 = pl.multiple_of(c * L, L)
    cur = src_ref[lead, pl.ds(t0, L), :]
    p0 = pl.multiple_of(jnp.maximum(t0 - SUBLANES, 0), SUBLANES)
    n0 = pl.multiple_of(jnp.minimum(t0 + L, S - SUBLANES), SUBLANES)
    prev_row = src_ref[lead, pl.ds(p0, SUBLANES), :][SUBLANES - 1:SUBLANES, :]
    next_row = src_ref[lead, pl.ds(n0, SUBLANES), :][0:1, :]
    prev_row = jnp.where(c > 0, prev_row, 0.0)
    next_row = jnp.where(c < nc - 1, next_row, 0.0)
    up = jnp.where(rows == 0, prev_row, pltpu.roll(cur, 1, 0))
    dn = jnp.where(rows == L - 1, next_row, pltpu.roll(cur, L - 1, 0))
    return b_ref[...] + up * w_ref[0:1, :] + cur * w_ref[1:2, :] + dn * w_ref[2:3, :]


def _mlstm_kernel(q_ref, k_ref, v_ref, o_ref, gt_ref, wq_ref, wk_ref, bq_ref, bk_ref, ng_ref,
                  y_ref, qt_ref, kb_ref, vt_ref, c_ref, cl_ref, ld_ref, st_ref, pc_ref, cs_ref, kq_ref, *, nc):
    L = MLSTM_CHUNK
    rows = lax.broadcasted_iota(jnp.int32, (L, LANES), 0)
    cols = lax.broadcasted_iota(jnp.int32, (L, LANES), 1)
    pad = 2 * SUBLANES
    piece_row = lax.broadcasted_iota(jnp.int32, (SUBLANES, LANES), 0)
    k_scale = HEAD_DIM ** -0.5

    def conv_body(c, carry):
        t0 = pl.multiple_of(c * L, L)
        yq = _short_conv_chunk(q_ref, 0, c, nc, wq_ref, bq_ref, rows)
        qt_ref[c] = (yq * _sigmoid(yq)).T.astype(BF16)
        yk = _short_conv_chunk(k_ref, 0, c, nc, wk_ref, bk_ref, rows)
        kb_ref[pl.ds(t0, L), :] = (yk * _sigmoid(yk) * k_scale).astype(BF16)
        vt_ref[c] = v_ref[0, pl.ds(t0, L), :].T
        gates = gt_ref[0, 0, :, pl.ds(t0, L)]
        lf = _log_sigmoid(jnp.where(piece_row < 3, gates[1:2], gates[3:4]))
        hi = lf.astype(BF16).astype(F32)
        mid = (lf - hi).astype(BF16).astype(F32)
        kind = piece_row % 3
        pc_ref[pl.ds(pl.multiple_of(c * SUBLANES, SUBLANES), SUBLANES), :] = jnp.where(
            piece_row >= 6, 0.0, jnp.where(kind == 0, hi, jnp.where(kind == 1, mid, lf - hi - mid)))
        return carry

    lax.fori_loop(0, nc, conv_body, 0, unroll=2)

    pieces = pc_ref[...].astype(BF16)
    fwd_rows = lax.broadcasted_iota(jnp.int32, pc_ref.shape, 0) % SUBLANES < 3
    cs_ref[...] = jnp.where(fwd_rows, _dot(pieces, (rows <= cols).astype(BF16)),
                            _dot(pieces, (rows >= cols).astype(BF16)))

    ROW_B, ROW_MAX, ROW_G, ROW_A, ROW_N, ROW_M = range(6)

    def stat_tile(rows_by_index, base=None):
        tile = jnp.zeros((SUBLANES, LANES), F32) if base is None else base
        for r, value in rows_by_index.items():
            tile = jnp.where(piece_row == r, value, tile)
        return tile

    def local_pass(c, rev):
        t0 = pl.multiple_of(c * L, L)
        slot = 2 * c + (1 if rev else 0)
        kb = kb_ref[pl.ds(t0, L), :]
        gates = gt_ref[0, 0, :, pl.ds(t0, L)]
        i_row = gates[(2 if rev else 0):(3 if rev else 1)]
        cs = cs_ref[pl.ds(pl.multiple_of(c * SUBLANES, SUBLANES), SUBLANES), :]
        o = 3 if rev else 0
        b_row = cs[o:o + 1] + cs[o + 1:o + 2] + cs[o + 2:o + 3]
        keep = (rows >= cols) if rev else (rows <= cols)
        g = b_row[:, 0:1] if rev else b_row[:, L - 1:L]
        w_tile = jnp.broadcast_to(i_row - b_row, (L, LANES)).T
        log_d = jnp.where(keep, w_tile + b_row, NEG_BIG)
        ld_ref[slot] = log_d
        w_end = g - b_row + i_row
        a = jnp.max(w_end, axis=1, keepdims=True)
        e_end = jnp.exp(w_end - a)
        ve = jnp.concatenate([vt_ref[c] * e_end, jnp.broadcast_to(e_end, (pad, L))], axis=0).astype(BF16)
        cn = _dot(ve, kb)
        cl_ref[slot] = cn[:HEAD_DIM]
        st_ref[slot] = stat_tile({ROW_B: b_row, ROW_MAX: jnp.max(log_d, axis=0, keepdims=True), ROW_G: g,
                                  ROW_A: a, ROW_N: cn[HEAD_DIM:HEAD_DIM + 1]})

    def local_body(c, carry):
        kq_ref[c] = _dot(kb_ref[pl.ds(pl.multiple_of(c * L, L), L), :], qt_ref[c])
        local_pass(c, False)
        local_pass(c, True)
        return carry

    lax.fori_loop(0, nc, local_body, 0, unroll=8)

    def scan_step(c, rev, n, m):
        slot = 2 * c + (1 if rev else 0)
        side = 1 if rev else 0
        stats = st_ref[slot]
        g = stats[ROW_G:ROW_G + 1]
        a = stats[ROW_A:ROW_A + 1]
        m_new = jnp.maximum(g + m, a)
        s_prev = jnp.exp(g + m - m_new)
        s_loc = jnp.exp(a - m_new)
        c_in = c_ref[side]
        c_ref[side] = s_prev * c_in + s_loc * cl_ref[slot]
        cl_ref[slot] = c_in
        n_new = s_prev * n + s_loc * stats[ROW_N:ROW_N + 1]
        st_ref[slot] = stat_tile({ROW_N: n, ROW_M: m}, base=stats)
        return n_new, m_new

    c_ref[...] = jnp.zeros_like(c_ref)
    n0 = jnp.zeros((1, HEAD_DIM), F32)
    m0 = jnp.full((1, LANES), MLSTM_M_INIT, F32)

    def scan_body(j, carry):
        n_f, m_f, n_b, m_b = carry
        n_f, m_f = scan_step(j, False, n_f, m_f)
        n_b, m_b = scan_step(nc - 1 - j, True, n_b, m_b)
        return n_f, m_f, n_b, m_b

    lax.fori_loop(0, nc, scan_body, (n0, m0, n0, m0), unroll=2)

    def output_pass(c, rev):
        t0 = pl.multiple_of(c * L, L)
        slot = 2 * c + (1 if rev else 0)
        qt = qt_ref[c]
        stats = st_ref[slot]
        log_inter = stats[ROW_B:ROW_B + 1] + stats[ROW_M:ROW_M + 1]
        m_t = jnp.maximum(log_inter, stats[ROW_MAX:ROW_MAX + 1])
        d = jnp.exp(ld_ref[slot] - m_t)
        e_inter = jnp.exp(log_inter - m_t)
        s = (kq_ref[c] * d).astype(BF16)
        vn = _dot(jnp.concatenate([vt_ref[c].astype(BF16), jnp.ones((pad, L), BF16)], axis=0), s)
        n_in = stats[ROW_N:ROW_N + 1]
        cq = _dot(jnp.concatenate([cl_ref[slot], jnp.broadcast_to(n_in, (pad, LANES))], axis=0).astype(BF16), qt)
        den = vn[HEAD_DIM:HEAD_DIM + 1] + e_inter * cq[HEAD_DIM:HEAD_DIM + 1]
        scale = 1.0 / jnp.maximum(jnp.abs(den), jnp.exp(-m_t))
        return ((vn[:HEAD_DIM] + e_inter * cq[:HEAD_DIM]) * scale).T

    def output_body(c, carry):
        t0 = pl.multiple_of(c * L, L)
        hs = output_pass(c, False) + output_pass(c, True)
        y_ref[0, pl.ds(t0, L), :] = _sigmoid(o_ref[0, pl.ds(t0, L), :]) * _rms(hs, ng_ref[...])
        return carry

    lax.fori_loop(0, nc, output_body, 0, unroll=8)


def _mlstm(qkvo, gates_t, conv_w, conv_b, norm_g):
    B, S, _ = qkvo.shape
    H, Dh = MLSTM_HEADS, HEAD_DIM
    nc = S // MLSTM_CHUNK
    seq = lambda off: pl.BlockSpec((1, S, Dh), lambda b, h: (b, 0, off + h))
    vec = lambda r, off: pl.BlockSpec((r, Dh), lambda b, h: (0, off + h))
    return pl.pallas_call(
        functools.partial(_mlstm_kernel, nc=nc),
        grid=(B, H),
        in_specs=[seq(0), seq(H), seq(2 * H), seq(3 * H),
                  pl.BlockSpec((1, 1, SUBLANES, S), lambda b, h: (b, h, 0, 0)),
                  vec(3, 0), vec(3, H), vec(1, 0), vec(1, H), vec(1, 0)],
        out_specs=pl.BlockSpec((1, S, Dh), lambda b, h: (b, 0, h)),
        out_shape=jax.ShapeDtypeStruct((B, S, H * Dh), F32),
        scratch_shapes=[pltpu.VMEM((nc, Dh, MLSTM_CHUNK), BF16), pltpu.VMEM((S, Dh), BF16),
                        pltpu.VMEM((nc, Dh, MLSTM_CHUNK), F32), pltpu.VMEM((2, Dh, Dh), F32),
                        pltpu.VMEM((2 * nc, Dh, Dh), F32), pltpu.VMEM((2 * nc, MLSTM_CHUNK, MLSTM_CHUNK), F32),
                        pltpu.VMEM((2 * nc, SUBLANES, LANES), F32), pltpu.VMEM((nc * SUBLANES, LANES), F32),
                        pltpu.VMEM((nc * SUBLANES, LANES), F32), pltpu.VMEM((nc, MLSTM_CHUNK, MLSTM_CHUNK), F32)],
        compiler_params=_cparams(("parallel", "arbitrary")),
        name="mlstm",
    )(qkvo, qkvo, qkvo, qkvo, gates_t.reshape(B, H, SUBLANES, S), conv_w, conv_w, conv_b.reshape(1, -1), conv_b.reshape(1, -1),
      norm_g.reshape(1, -1))


FILTER_PASSES = 3
FFT_UNROLL = 16


def _hilo(m):
    m32 = jnp.asarray(m, F32)
    hi = m32.astype(BF16)
    return hi, (m32 - hi.astype(F32)).astype(BF16)


def _stack_complex(m):
    return np.block([[m.real, -m.imag], [m.imag, m.real]])


def _dft_constants(S):
    N = 2 * S
    N2 = FFT_N2
    N1 = N // N2
    k1 = np.arange(N1)
    n2 = np.arange(N2)
    f1 = np.exp(-2j * np.pi * np.outer(k1, np.arange(N1)) / N1)
    f2 = np.exp(-2j * np.pi * np.outer(n2, n2) / N2)
    tw = np.exp(-2j * np.pi * np.outer(k1, n2) / N)
    g1 = np.conj(f1).T[:N1 // 2] / N
    consts = dict(
        f1_real=_hilo(np.concatenate([f1.real, f1.imag], axis=0)),
        f1_half=_hilo(_stack_complex(f1[:, :N1 // 2])),
        f2=_hilo(_stack_complex(f2)),
        g1=_hilo(_stack_complex(g1)),
        f2_tw=_hilo(np.stack([_stack_complex(f2 * tw[k][None, :]) for k in k1]))[0],
        f2_tw_inv=_hilo(np.stack([_stack_complex(np.conj(tw[k])[:, None] * np.conj(f2)) for k in k1]))[0],
        tw_re=jnp.asarray(np.broadcast_to(tw.real[:, :, None], (N1, N2, LANES)), F32),
        tw_im=jnp.asarray(np.broadcast_to(tw.imag[:, :, None], (N1, N2, LANES)), F32),
    )
    return N1, N2, consts


def _hy_hidden_kernel(w1t_ref, w1c_ref, w1s_ref, b1_ref, w2_ref, b2_ref, fr_ref, o_ref, *, S, T):
    j = pl.program_id(0) * T + lax.broadcasted_iota(jnp.int32, (T, 1), 0)
    p = jnp.where(j < S, j, 2 * S - j).astype(F32)
    t = p / (S - 1)
    w = (2.0 * math.pi) * p / S
    band = lax.broadcasted_iota(jnp.int32, (1, HYENA_BANDS), 1).astype(F32)
    bands = 1e-4 + band * ((HYENA_BANDS - 1 - 1e-4) / (HYENA_BANDS - 1))
    arg = bands * w
    pre = t * w1t_ref[...] + _dot3(jnp.cos(arg), w1c_ref[...]) + _dot3(-jnp.sin(arg), w1s_ref[...]) + b1_ref[...]
    hid = jnp.sin(fr_ref[...] * pre)
    o_ref[...] = jnp.sin(fr_ref[...] * (_dot3(hid, w2_ref[...]) + b2_ref[...]))


def _hy_hidden(S, w1, b1, w2, b2, freq):
    N = 2 * S
    T = 1024
    Hd = w2.shape[0]
    full = lambda a: pl.BlockSpec(a.shape, lambda i: (0,) * a.ndim)
    args = (w1[0:1], w1[1:1 + HYENA_BANDS], w1[1 + HYENA_BANDS:], b1.reshape(1, Hd), w2, b2.reshape(1, Hd),
            freq.reshape(1, Hd))
    return pl.pallas_call(
        functools.partial(_hy_hidden_kernel, S=S, T=T),
        grid=(N // T,),
        in_specs=[full(a) for a in args],
        out_specs=pl.BlockSpec((T, Hd), lambda i: (i, 0)),
        out_shape=jax.ShapeDtypeStruct((N, Hd), F32),
        compiler_params=_cparams(("parallel",)),
        name="hyena_hidden",
    )(*args)


def _fft_stage2(a_re, a_im, k1, twr_ref, twi_ref, f2_hi, f2_lo, passes):
    r0 = pl.multiple_of(k1 * FFT_PITCH, SUBLANES)
    ar = a_re[pl.ds(r0, FFT_N2), :]
    ai = a_im[pl.ds(r0, FFT_N2), :]
    twr = twr_ref[k1]
    twi = twi_ref[k1]
    t = jnp.concatenate([ar * twr - ai * twi, ar * twi + ai * twr], axis=0)
    return _dotc(f2_hi, f2_lo, t, passes)


def _hy_filter_kernel(hid_ref, w3f_ref, w3b_ref, df_ref, db_ref, bias_ref, f1h_ref, f1l_ref, f2h_ref, f2l_ref,
                      twr_ref, twi_ref, kr_ref, ki_ref, u_ref, a_re, a_im, *, S, N1):
    N2, P = FFT_N2, FFT_PITCH
    T = 512
    nb = T // N2

    def gen_body(i, carry):
        r0 = pl.multiple_of(i * T, T)
        hid = hid_ref[pl.ds(r0, T), :]
        j = r0 + lax.broadcasted_iota(jnp.int32, (T, 1), 0)
        t = jnp.where(j < S, j, 2 * S - j).astype(F32) / (S - 1)
        kf = _dot3(hid, w3f_ref[...]) * (jnp.exp(-t * jnp.abs(df_ref[...])) + HYENA_WINDOW_SHIFT)
        kb = _dot3(hid, w3b_ref[...]) * (jnp.exp(-t * jnp.abs(db_ref[...])) + HYENA_WINDOW_SHIFT)
        ker = (jnp.where(j < S, kf, 0.0) + jnp.where((j > S) | (j == 0), kb, 0.0)
               + jnp.where(j == 0, bias_ref[0], 0.0))
        for b in range(nb):
            u_ref[pl.ds(pl.multiple_of((i * nb + b) * P, SUBLANES), N2), :] = ker[b * N2:(b + 1) * N2]
        return carry

    lax.fori_loop(0, 2 * S // T, gen_body, 0)

    def stage1(n2, carry):
        x = u_ref[pl.ds(n2, N1, stride=P), :]
        a = _dotc(f1h_ref[...], f1l_ref[...], x, FILTER_PASSES)
        a_re[pl.ds(n2, N1, stride=P), :] = a[:N1]
        a_im[pl.ds(n2, N1, stride=P), :] = a[N1:]
        return carry

    lax.fori_loop(0, N2, stage1, 0, unroll=FFT_UNROLL)

    def stage2(k1, carry):
        x = _fft_stage2(a_re, a_im, k1, twr_ref, twi_ref, f2h_ref[...], f2l_ref[...], FILTER_PASSES)
        r0 = pl.multiple_of(k1 * N2, N2)
        kr_ref[0, pl.ds(r0, N2), :] = x[:N2]
        ki_ref[0, pl.ds(r0, N2), :] = x[N2:]
        return carry

    lax.fori_loop(0, N1, stage2, 0, unroll=2 * FFT_UNROLL)


def _hy_filter_spectrum(S, hid, w3, deltas, bias, N1, consts):
    N = 2 * S
    N2, P = FFT_N2, FFT_PITCH
    Hd = hid.shape[1]
    nblk = w3.shape[1] // (2 * HYENA_ORDER * LANES)
    f1h, f1l = consts["f1_real"]
    f2h, f2l = consts["f2"]
    full = lambda a: pl.BlockSpec(a.shape, lambda o, j: (0,) * a.ndim, pipeline_mode=pl.Buffered(1))
    colf = lambda o, j: (0, (2 * o) * nblk + j)
    colb = lambda o, j: (0, (2 * o + 1) * nblk + j)
    out = jax.ShapeDtypeStruct((HYENA_ORDER, N, nblk * LANES), F32)
    return pl.pallas_call(
        functools.partial(_hy_filter_kernel, S=S, N1=N1),
        grid=(HYENA_ORDER, nblk),
        in_specs=[full(hid),
                  pl.BlockSpec((Hd, LANES), colf), pl.BlockSpec((Hd, LANES), colb),
                  pl.BlockSpec((1, LANES), colf), pl.BlockSpec((1, LANES), colb),
                  pl.BlockSpec((1, 1, LANES), lambda o, j: (o, 0, j)),
                  full(f1h), full(f1l), full(f2h), full(f2l), full(consts["tw_re"]), full(consts["tw_im"])],
        out_specs=[pl.BlockSpec((1, N, LANES), lambda o, j: (o, 0, j))] * 2,
        out_shape=[out, out],
        scratch_shapes=[pltpu.VMEM((N1 * P, LANES), F32)] * 3,
        compiler_params=_cparams(("parallel", "parallel")),
        name="hyena_filter_spectrum",
    )(hid, w3, w3, deltas.reshape(1, -1), deltas.reshape(1, -1), bias.reshape(HYENA_ORDER, 1, -1),
      f1h, f1l, f2h, f2l, consts["tw_re"], consts["tw_im"])


def _hy_conv_kernel(u_ref, x_ref, kr_ref, ki_ref, wu_ref, bu_ref, wx_ref, bx_ref, ng_ref,
                    f1_ref, g1_ref, mf_ref, mi_ref, o_ref, u_re, u_im, a_re, a_im, *, N1, conv_u, last):
    N2, P = FFT_N2, FFT_PITCH
    nh = N1 // 2
    rows = lax.broadcasted_iota(jnp.int32, (N2, LANES), 0)

    def load_body(c, carry):
        r0 = pl.multiple_of(c * P, SUBLANES)
        for b, dst in ((0, u_re), (1, u_im)):
            if conv_u:
                dst[pl.ds(r0, N2), :] = _short_conv_chunk(u_ref, b, c, nh, wu_ref, bu_ref, rows)
            else:
                dst[pl.ds(r0, N2), :] = u_ref[b, pl.ds(pl.multiple_of(c * N2, N2), N2), :]
        return carry

    lax.fori_loop(0, nh, load_body, 0, unroll=4)

    def stage1(n2, carry):
        x = jnp.concatenate([u_re[pl.ds(n2, nh, stride=P), :], u_im[pl.ds(n2, nh, stride=P), :]], axis=0)
        a = _dot(f1_ref[...], x.astype(BF16))
        a_re[pl.ds(n2, N1, stride=P), :] = a[:N1]
        a_im[pl.ds(n2, N1, stride=P), :] = a[N1:]
        return carry

    lax.fori_loop(0, N2, stage1, 0, unroll=FFT_UNROLL)

    def stage2(k1, carry):
        r0 = pl.multiple_of(k1 * P, SUBLANES)
        a = jnp.concatenate([a_re[pl.ds(r0, N2), :], a_im[pl.ds(r0, N2), :]], axis=0)
        x = _dot(mf_ref[k1], a.astype(BF16))
        s0 = pl.multiple_of(k1 * N2, N2)
        kr = kr_ref[0, pl.ds(s0, N2), :]
        ki = ki_ref[0, pl.ds(s0, N2), :]
        xr, xi = x[:N2], x[N2:]
        y = jnp.concatenate([xr * kr - xi * ki, xr * ki + xi * kr], axis=0)
        b = _dot(mi_ref[k1], y.astype(BF16))
        a_re[pl.ds(r0, N2), :] = b[:N2]
        a_im[pl.ds(r0, N2), :] = b[N2:]
        return carry

    lax.fori_loop(0, N1, stage2, 0, unroll=2 * FFT_UNROLL)

    def inv2(n2, carry):
        b = jnp.concatenate([a_re[pl.ds(n2, N1, stride=P), :], a_im[pl.ds(n2, N1, stride=P), :]], axis=0)
        y = _dot(g1_ref[...], b.astype(BF16))
        u_re[pl.ds(n2, nh, stride=P), :] = y[:nh]
        u_im[pl.ds(n2, nh, stride=P), :] = y[nh:]
        return carry

    lax.fori_loop(0, N2, inv2, 0, unroll=FFT_UNROLL)

    def out_body(c, carry):
        r0 = pl.multiple_of(c * P, SUBLANES)
        t0 = pl.multiple_of(c * N2, N2)
        for b, src in ((0, u_re), (1, u_im)):
            z = _short_conv_chunk(x_ref, b, c, nh, wx_ref, bx_ref, rows) * src[pl.ds(r0, N2), :]
            o_ref[b, pl.ds(t0, N2), :] = _rms(z, ng_ref[...]) if last else z
        return carry

    lax.fori_loop(0, nh, out_body, 0, unroll=4)


def _hy_conv(u, u_blk, x, x_blk, kr, ki, order, conv_w, conv_b, norm_g, N1, consts, conv_u, last):
    B, S, _ = u.shape
    N2, P = FFT_N2, FFT_PITCH
    N = 2 * S
    nblk = kr.shape[2] // LANES
    pair = 2
    single = pl.Buffered(1)
    full = lambda a: pl.BlockSpec(a.shape, lambda j, p: (0,) * a.ndim, pipeline_mode=single)
    seq = lambda off: pl.BlockSpec((pair, S, LANES), lambda j, p: (p, 0, off + j))
    vec = lambda r, off: pl.BlockSpec((r, LANES), lambda j, p: (0, off + j))
    spec = pl.BlockSpec((1, N, LANES), lambda j, p: (order, 0, j), pipeline_mode=single)
    mats = [consts["f1_half"][0], consts["g1"][0], consts["f2_tw"], consts["f2_tw_inv"]]
    return pl.pallas_call(
        functools.partial(_hy_conv_kernel, N1=N1, conv_u=conv_u, last=last),
        grid=(nblk, B // pair),
        in_specs=[seq(u_blk), seq(x_blk), spec, spec,
                  vec(3, u_blk if conv_u else 0), vec(1, u_blk if conv_u else 0), vec(3, x_blk), vec(1, x_blk),
                  pl.BlockSpec((1, LANES), lambda j, p: (0, j))] + [full(m) for m in mats],
        out_specs=pl.BlockSpec((pair, S, LANES), lambda j, p: (p, 0, j)),
        out_shape=jax.ShapeDtypeStruct((B, S, nblk * LANES), F32),
        scratch_shapes=[pltpu.VMEM((N1 // 2 * P, LANES), F32)] * 2 + [pltpu.VMEM((N1 * P, LANES), F32)] * 2,
        compiler_params=_cparams(("parallel", "parallel")),
        name=f"hyena_conv{order}",
    )(u, x, kr, ki, conv_w, conv_b.reshape(1, -1), conv_w, conv_b.reshape(1, -1), norm_g.reshape(1, -1), *mats)


def _hyena(hx, conv_w, conv_b, w1, b1, w2, b2, w3, freq, deltas, bias, norm_g):
    B, S, C3 = hx.shape
    nblk = C3 // (HYENA_ORDER + 1) // LANES
    N1, N2, consts = _dft_constants(S)
    hid = _hy_hidden(S, w1, b1, w2, b2, freq)
    kr, ki = _hy_filter_spectrum(S, hid, w3, deltas, bias, N1, consts)
    z1 = _hy_conv(hx, 2 * nblk, hx, 0, kr, ki, 0, conv_w, conv_b, norm_g, N1, consts, True, False)
    return _hy_conv(z1, 0, hx, nblk, kr, ki, 1, conv_w, conv_b, norm_g, N1, consts, False, True)


def _outproj_kernel(ym_ref, yh_ref, x_ref, gt_ref, sc_ref, sh_ref, g_ref, wm_ref, wh_ref, wr_ref,
                    x1_ref, hf_ref, aff_ref):
    mixed = _dot(ym_ref[0].astype(BF16), wm_ref[...]) + _dot(yh_ref[0].astype(BF16), wh_ref[...])
    x1 = x_ref[0] + gt_ref[0] * mixed
    x1_ref[0] = x1
    hf = _rms(x1, g_ref[...]) * (1.0 + sc_ref[0]) + sh_ref[0]
    hf_ref[0] = hf.astype(BF16)
    logits = _dot3_nt(wr_ref[...], hf)
    e = jnp.exp(logits - jnp.max(logits, axis=0, keepdims=True))
    aff_ref[0] = e / jnp.sum(e, axis=0, keepdims=True)


def _out_projection(y_m, y_h, x, gt1, sc2, sh2, g_ffn, w_out, w_router, ts=512):
    B, S, D = x.shape
    Wm = y_m.shape[2]
    E = w_router.shape[1]
    wm = w_out[:Wm].astype(BF16)
    wh = w_out[Wm:].astype(BF16)
    tile = lambda w: pl.BlockSpec((1, ts, w), lambda b, i: (b, i, 0))
    row = pl.BlockSpec((1, 1, D), lambda b, i: (b, 0, 0))
    full = lambda a: pl.BlockSpec(a.shape, lambda b, i: (0,) * a.ndim)
    wr = w_router.T
    g = g_ffn.reshape(1, D)
    return pl.pallas_call(
        _outproj_kernel,
        grid=(B, S // ts),
        in_specs=[tile(Wm), tile(y_h.shape[2]), tile(D), row, row, row, full(g), full(wm), full(wh), full(wr)],
        out_specs=[tile(D), tile(D), pl.BlockSpec((1, E, ts), lambda b, i: (b, 0, i))],
        out_shape=[jax.ShapeDtypeStruct((B, S, D), F32), jax.ShapeDtypeStruct((B, S, D), BF16),
                   jax.ShapeDtypeStruct((B, E, S), F32)],
        compiler_params=_cparams(("parallel", "parallel")),
        name="out_proj_router",
    )(y_m, y_h, x, gt1.reshape(B, 1, D), sc2.reshape(B, 1, D), sh2.reshape(B, 1, D), g, wm, wh, wr)


def _route_kernel(aff_ref, pos_ref, cnt_ref, *, S, cap):
    aff = aff_ref[0]
    E = aff.shape[0]
    tpos = lax.broadcasted_iota(jnp.int32, (E, S), 1)
    count = lambda mask: jnp.sum(jnp.where(mask, 1.0, 0.0), axis=1, keepdims=True)
    as_float = lambda word: lax.bitcast_convert_type(word, F32)

    def value_step(i, prefix):
        cand = prefix | jnp.left_shift(1, 30 - i)
        return jnp.where(count(aff >= as_float(cand)) >= cap, cand, prefix)

    thresh = as_float(lax.fori_loop(0, 31, value_step, jnp.zeros((E, 1), jnp.int32)))
    gt = aff > thresh
    eq = aff == thresh
    need = cap - count(gt)
    nbits = S.bit_length() - 1

    def index_step(i, x):
        cand = x | jnp.left_shift(1, nbits - 1 - i)
        return jnp.where(count(eq & (tpos < cand)) < need, cand, x)

    last_tie = lax.fori_loop(0, nbits, index_step, jnp.zeros((E, 1), jnp.int32))
    sel = jnp.where(gt | (eq & (tpos <= last_tie)), 1.0, 0.0)
    before = (lax.broadcasted_iota(jnp.int32, (LANES, LANES), 0)
              < lax.broadcasted_iota(jnp.int32, (LANES, LANES), 1)).astype(BF16)
    carry = jnp.zeros((E, 1), F32)
    lane = lax.broadcasted_iota(jnp.int32, (E, LANES), 1)
    starts = jnp.zeros((E, LANES), F32)
    for c in range(S // LANES):
        sc = sel[:, c * LANES:(c + 1) * LANES]
        rank = _dot(sc.astype(BF16), before) + carry
        pos_ref[0, :, c * LANES:(c + 1) * LANES] = jnp.where(sc > 0.0, rank, -1.0).astype(jnp.int32)
        starts = jnp.where(lane == c, carry, starts)
        carry = carry + jnp.sum(sc, axis=1, keepdims=True)
    cnt_ref[0] = jnp.where(lane >= S // LANES, carry, starts).astype(jnp.int32)


def _route(aff_t, cap):
    B, E, S = aff_t.shape
    assert S // LANES < LANES
    R = B * E
    blk = pl.BlockSpec((1, R, S), lambda i: (0, 0, 0))
    pos, cnt = pl.pallas_call(
        functools.partial(_route_kernel, S=S, cap=cap),
        grid=(1,),
        in_specs=[blk],
        out_specs=[blk, pl.BlockSpec((1, R, LANES), lambda i: (0, 0, 0))],
        out_shape=[jax.ShapeDtypeStruct((1, R, S), jnp.int32), jax.ShapeDtypeStruct((1, R, LANES), jnp.int32)],
        compiler_params=_cparams(("arbitrary",)),
        name="route_topk",
    )(aff_t.reshape(1, R, S))
    return pos.reshape(B, E, S), cnt.reshape(B, E, LANES)


def _window_tables(cnt, S, cap):
    B, E, _ = cnt.shape
    nch = S // LANES
    bounds = cnt[:, :, 1:nch + 1]
    first = jnp.arange(cap // GATHER_SLOTS, dtype=jnp.int32) * GATHER_SLOTS
    chunk_of = lambda slot: jnp.sum((bounds[:, :, None, :] <= slot[None, None, :, None]).astype(jnp.int32), axis=-1)
    per = GATHER_ALIGN // LANES
    step = GATHER_TOKENS // GATHER_ALIGN
    lo_a = chunk_of(first) // per
    hi_a = chunk_of(first + GATHER_SLOTS - 1) // per
    g_n = (hi_a - lo_a) // step + 1
    g_lo = jnp.minimum(lo_a, S // GATHER_ALIGN - step * g_n)
    edges = cnt[:, :, 0:nch + 1:SCATTER_TOKENS // LANES]
    lo, hi = edges[:, :, :-1], edges[:, :, 1:]
    s_lo = jnp.swapaxes(lo // SCATTER_SLOTS, 1, 2)
    s_n = jnp.swapaxes(jnp.where(hi > lo, (hi - 1) // SCATTER_SLOTS - lo // SCATTER_SLOTS + 1, 0), 1, 2)

    def flat_runs(starts, counts, stride, length):
        groups = counts.shape[-1]
        ends = jnp.cumsum(counts, axis=-1)
        j = jnp.arange(length, dtype=jnp.int32)
        grp = jnp.minimum(jnp.sum((ends[..., None, :] <= j[:, None]).astype(jnp.int32), axis=-1), groups - 1)
        chosen = grp[..., None] == jnp.arange(groups, dtype=jnp.int32)
        pick = lambda t: jnp.sum(jnp.where(chosen, t[..., None, :], 0), axis=-1)
        valid = j < ends[..., -1:]
        val = jnp.where(valid, pick(starts) + (j - pick(ends - counts)) * stride, 0)
        flat = lambda t: t.reshape(-1).astype(jnp.int32)
        return flat(jnp.where(valid, grp, 0)), flat(val), flat(ends[..., -1])

    return flat_runs(g_lo, g_n, step, GATHER_LIST) + flat_runs(s_lo, s_n, 1, SCATTER_LIST)


META_ROWS = 16


GATHER_SLOTS = 128
GATHER_TOKENS = 512
GATHER_ALIGN = 256
GATHER_LIST = 12
SCATTER_SLOTS = 256
SCATTER_TOKENS = 512
SCATTER_LIST = N_EXPERTS * ((SCATTER_TOKENS - 1) // SCATTER_SLOTS + 2)


def _gather_kernel(blk_ref, win_ref, cnt_ref, pos_ref, aff_ref, hf_ref, xs_ref, meta_ref, acc_ref, macc_ref):
    R, tk = GATHER_SLOTS, GATHER_TOKENS
    base = (pl.program_id(0) * pl.num_programs(1) + pl.program_id(1)) * GATHER_LIST
    count = cnt_ref[pl.program_id(0) * pl.num_programs(1) + pl.program_id(1)]
    mrow = lax.broadcasted_iota(jnp.int32, (META_ROWS, tk), 0)
    lane = lax.broadcasted_iota(jnp.int32, (1, tk), 1)
    srow = lax.broadcasted_iota(jnp.int32, (R, tk), 0)
    acc_ref[...] = jnp.zeros_like(acc_ref)
    macc_ref[...] = jnp.zeros_like(macc_ref)

    def contribution(j):
        r0 = pl.multiple_of(blk_ref[base + j] * R, R)
        t0 = pl.multiple_of(win_ref[base + j] * GATHER_ALIGN, GATHER_ALIGN)
        first_slot = jnp.where(j < count, r0, -(R + 1))
        onehot = jnp.where(first_slot + srow == pos_ref[0, 0, :, pl.ds(t0, tk)], 1.0, 0.0).astype(BF16)
        a = aff_ref[0, 0, :, pl.ds(t0, tk)]
        hi = a.astype(BF16).astype(F32)
        mid = (a - hi).astype(BF16).astype(F32)
        lo = (a - hi - mid).astype(BF16).astype(F32)
        t = t0 + lane
        pieces = (hi, mid, lo, (t // 64).astype(F32), (t % 64).astype(F32))
        meta = jnp.zeros((META_ROWS, tk), F32)
        for r, piece in enumerate(pieces):
            meta = jnp.where(mrow == r, piece, meta)
        return r0, _dot(onehot, hf_ref[0, pl.ds(t0, tk), :]), _dot_nt(meta.astype(BF16), onehot)

    def pair_body(i, carry):
        for r0, rows, meta in (contribution(2 * i), contribution(2 * i + 1)):
            acc_ref[pl.ds(r0, R), :] += rows
            macc_ref[:, pl.ds(r0, R)] += meta
        return carry

    lax.fori_loop(0, (count + 1) // 2, pair_body, 0)
    xs_ref[0, 0] = acc_ref[...].astype(BF16)
    meta_ref[0, 0] = macc_ref[...]


def _gather(g_blk, g_win, g_cnt, pos, aff_t, hf, cap):
    B, E, S = pos.shape
    D = hf.shape[2]
    most = cap // GATHER_SLOTS + (S // GATHER_ALIGN - 1) // (GATHER_TOKENS // GATHER_ALIGN)
    assert most <= GATHER_LIST and GATHER_LIST % 2 == 0
    row = pl.BlockSpec((1, 1, 1, S), lambda b, e, *_: (b, e, 0, 0))
    return pl.pallas_call(
        _gather_kernel,
        grid_spec=pltpu.PrefetchScalarGridSpec(
            num_scalar_prefetch=3,
            grid=(B, E),
            in_specs=[row, row, pl.BlockSpec((1, S, D), lambda b, e, *_: (b, 0, 0))],
            out_specs=[pl.BlockSpec((1, 1, cap, D), lambda b, e, *_: (b, e, 0, 0)),
                       pl.BlockSpec((1, 1, META_ROWS, cap), lambda b, e, *_: (b, e, 0, 0))],
            scratch_shapes=[pltpu.VMEM((cap, D), F32), pltpu.VMEM((META_ROWS, cap), F32)]),
        out_shape=[jax.ShapeDtypeStruct((B, E, cap, D), BF16), jax.ShapeDtypeStruct((B, E, META_ROWS, cap), F32)],
        compiler_params=_cparams(("parallel", "arbitrary")),
        name="moe_gather",
    )(g_blk, g_win, g_cnt, pos.reshape(B, E, 1, S), aff_t.reshape(B, E, 1, S), hf)


def _ffn_kernel(xs_ref, meta_ref, wg_ref, wu_ref, wd_ref, y_ref, wdb_ref, *, tf):
    x = xs_ref[0, 0]
    cap = x.shape[0]
    FF = wg_ref.shape[2]

    @pl.when(pl.program_id(1) == 0)
    def _():
        for f in range(FF // tf):
            wdb_ref[f * tf:(f + 1) * tf, :] = wd_ref[0, f * tf:(f + 1) * tf, :].astype(BF16)

    y = jnp.zeros((cap, wd_ref.shape[2]), F32)
    for f in range(FF // tf):
        hg = _dot(x, wg_ref[0, :, f * tf:(f + 1) * tf])
        hu = _dot(x, wu_ref[0, :, f * tf:(f + 1) * tf])
        hid = (hg * _sigmoid(hg) * hu).astype(BF16)
        y = y + _dot(hid, wdb_ref[f * tf:(f + 1) * tf, :])
    m = meta_ref[0, 0]
    g_row = m[0:1] + m[1:2] + m[2:3]
    eye = lax.broadcasted_iota(jnp.int32, (cap, cap), 0) == lax.broadcasted_iota(jnp.int32, (cap, cap), 1)
    g_col = jnp.sum(jnp.where(eye, g_row, 0.0), axis=1, keepdims=True)
    y_ref[0, 0] = (y * g_col).astype(BF16)


def _expert_ffn(xs, meta, w_gate, w_up, w_down, tf=512):
    B, E, cap, D = xs.shape
    FF = w_gate.shape[2]
    return pl.pallas_call(
        functools.partial(_ffn_kernel, tf=tf),
        grid=(E, B),
        in_specs=[pl.BlockSpec((1, 1, cap, D), lambda e, b: (b, e, 0, 0)),
                  pl.BlockSpec((1, 1, META_ROWS, cap), lambda e, b: (b, e, 0, 0)),
                  pl.BlockSpec((1, D, FF), lambda e, b: (e, 0, 0)),
                  pl.BlockSpec((1, D, FF), lambda e, b: (e, 0, 0)),
                  pl.BlockSpec((1, FF, D), lambda e, b: (e, 0, 0))],
        out_specs=pl.BlockSpec((1, 1, cap, D), lambda e, b: (b, e, 0, 0)),
        out_shape=jax.ShapeDtypeStruct((B, E, cap, D), BF16),
        scratch_shapes=[pltpu.VMEM((FF, D), BF16)],
        compiler_params=_cparams(("parallel", "arbitrary")),
        name="moe_ffn",
    )(xs, meta, w_gate.astype(BF16), w_up.astype(BF16), w_down)


def _scatter_kernel(e_ref, blk_ref, cnt_ref, y_ref, meta_ref, x1_ref, gt_ref, g_ref, o_ref, acc_ref):
    tt, R = SCATTER_TOKENS, SCATTER_SLOTS
    tile = pl.program_id(0) * pl.num_programs(1) + pl.program_id(1)
    tok = pl.program_id(1) * tt + lax.broadcasted_iota(jnp.int32, (tt, R), 0)
    acc_ref[...] = jnp.zeros_like(acc_ref)
    count = cnt_ref[tile]

    def contribution(j):
        e = e_ref[tile * SCATTER_LIST + j]
        r0 = pl.multiple_of(blk_ref[tile * SCATTER_LIST + j] * R, R)
        m = meta_ref[0, e, :, pl.ds(r0, R)]
        idx = jnp.where(j < count, (m[3:4] * 64.0 + m[4:5]).astype(jnp.int32), -1)
        onehot = jnp.where(tok == idx, 1.0, 0.0).astype(BF16)
        return _dot(onehot, y_ref[0, e, pl.ds(r0, R), :])

    def pair_body(i, carry):
        acc_ref[...] += contribution(2 * i) + contribution(2 * i + 1)
        return carry

    lax.fori_loop(0, (count + 1) // 2, pair_body, 0)
    o_ref[0] = _rms(x1_ref[0] + gt_ref[0] * acc_ref[...], g_ref[...])


def _scatter_final(s_e, s_blk, s_cnt, y, meta, x1, gt2, g_final):
    B, E, cap, D = y.shape
    S = x1.shape[1]
    tt = SCATTER_TOKENS
    tile = pl.BlockSpec((1, tt, D), lambda b, i, *_: (b, i, 0))
    return pl.pallas_call(
        _scatter_kernel,
        grid_spec=pltpu.PrefetchScalarGridSpec(
            num_scalar_prefetch=3,
            grid=(B, S // tt),
            in_specs=[pl.BlockSpec((1, E, cap, D), lambda b, i, *_: (b, 0, 0, 0)),
                      pl.BlockSpec((1, E, META_ROWS, cap), lambda b, i, *_: (b, 0, 0, 0)),
                      tile, pl.BlockSpec((1, 1, D), lambda b, i, *_: (b, 0, 0)),
                      pl.BlockSpec((1, D), lambda b, i, *_: (0, 0))],
            out_specs=tile,
            scratch_shapes=[pltpu.VMEM((tt, D), F32)]),
        out_shape=jax.ShapeDtypeStruct((B, S, D), F32),
        compiler_params=_cparams(("parallel", "parallel")),
        name="moe_scatter_final",
    )(s_e, s_blk, s_cnt, y, meta, x1, gt2.reshape(B, 1, D), g_final.reshape(1, D))


def kernel(x, c, w_ada, b_ada, g_mix, w_in, b_in, conv_qk_w, conv_qk_b, mlstm_norm_g, conv_hy_w, conv_hy_b,
           hy_w1, hy_b1, hy_w2, hy_b2, hy_w3, hy_freq, hy_deltas, hy_bias, hyena_norm_g, w_out, g_ffn,
           w_router, w_gate, w_up, w_down, g_final):
    S = x.shape[1]
    cap = EC_CAPACITY_FACTOR * S // N_EXPERTS
    assert w_ada.shape[0] == 1, "single-layer block: the final RMSNorm is fused into the MoE scatter"
    l = 0
    mod = _modulation(c, w_ada[l], b_ada[l])
    sh1, sc1, gt1, sh2, sc2, gt2 = jnp.split(mod, 6, axis=-1)
    qkvo, hx, gates_t = _in_projection(x, sc1, sh1, g_mix[l], w_in[l], b_in[l])
    y_m = _mlstm(qkvo, gates_t, conv_qk_w[l], conv_qk_b[l], mlstm_norm_g[l])
    y_h = _hyena(hx, conv_hy_w[l], conv_hy_b[l], hy_w1[l], hy_b1[l], hy_w2[l], hy_b2[l], hy_w3[l],
                 hy_freq[l], hy_deltas[l], hy_bias[l], hyena_norm_g[l])
    x1, hf, aff_t = _out_projection(y_m, y_h, x, gt1, sc2, sh2, g_ffn[l], w_out[l], w_router[l])
    pos, cnt = _route(aff_t, cap)
    g_blk, g_win, g_cnt, s_e, s_blk, s_cnt = _window_tables(cnt, S, cap)
    xs, meta = _gather(g_blk, g_win, g_cnt, pos, aff_t, hf, cap)
    y = _expert_ffn(xs, meta, w_gate[l], w_up[l], w_down[l])
    return _scatter_final(s_e, s_blk, s_cnt, y, meta, x1, gt2, g_final)
```

```python
import functools
import math

import numpy as np
import jax
import jax.numpy as jnp
from jax import lax
from jax.experimental import pallas as pl
from jax.experimental.pallas import tpu as pltpu

F32 = jnp.float32
BF16 = jnp.bfloat16

MLSTM_HEADS = 4
HEAD_DIM = 128
MLSTM_CHUNK = 128
MLSTM_M_INIT = -1e30
HYENA_GROUP_DIM = 128
HYENA_ORDER = 2
HYENA_BANDS = 16
HYENA_WINDOW_SHIFT = 0.05
N_GATE_COLS = 4 * MLSTM_HEADS
N_EXPERTS = 16
EC_CAPACITY_FACTOR = 2
RMS_EPS = 1e-6

LANES = 128
SUBLANES = 8
FFT_N2 = 64
FFT_PITCH = 72
NEG_BIG = -1e30
VMEM_LIMIT = 56 * 1024 * 1024


def _cparams(sem, vmem=None, flags=None):
    return pltpu.CompilerParams(dimension_semantics=sem, vmem_limit_bytes=vmem or VMEM_LIMIT, flags=flags)


def _split(a):
    hi = a.astype(BF16)
    lo = (a - hi.astype(F32)).astype(BF16)
    return hi, lo


def _dot(a, b):
    return jnp.dot(a, b, preferred_element_type=F32)


def _dot_nt(a, b):
    return lax.dot_general(a, b, (((1,), (1,)), ((), ())), preferred_element_type=F32)


def _dot3(a, b):
    ah, al = _split(a)
    bh, bl = _split(b)
    return _dot(ah, bh) + _dot(ah, bl) + _dot(al, bh)


def _dot3_nt(a, b):
    ah, al = _split(a)
    bh, bl = _split(b)
    return _dot_nt(ah, bh) + _dot_nt(ah, bl) + _dot_nt(al, bh)


def _dotc(w_hi, w_lo, x, passes):
    xh, xl = _split(x)
    out = _dot(w_hi, xh)
    if passes == 3:
        out = out + _dot(w_hi, xl) + _dot(w_lo, xh)
    return out


def _rms(x, g):
    return x * lax.rsqrt(jnp.mean(x * x, axis=-1, keepdims=True) + RMS_EPS) * g


def _sigmoid(x):
    return 1.0 / (1.0 + jnp.exp(-x))


def _log_sigmoid(x):
    return jnp.minimum(x, 0.0) - jnp.log(1.0 + jnp.exp(-jnp.abs(x)))


def _mod_kernel(c_ref, w_ref, b_ref, o_ref):
    o_ref[...] = _dot3(c_ref[...], w_ref[...]) + b_ref[...]


def _modulation(c, w_ada, b_ada):
    B, D = c.shape
    n = w_ada.shape[1]
    tn = 768
    return pl.pallas_call(
        _mod_kernel,
        grid=(n // tn,),
        in_specs=[pl.BlockSpec((B, D), lambda j: (0, 0)),
                  pl.BlockSpec((D, tn), lambda j: (0, j)),
                  pl.BlockSpec((1, tn), lambda j: (0, j))],
        out_specs=pl.BlockSpec((B, tn), lambda j: (0, j)),
        out_shape=jax.ShapeDtypeStruct((B, n), F32),
        compiler_params=_cparams(("parallel",)),
        name="adaln_mod",
    )(c, w_ada, b_ada.reshape(1, n))


def _inproj_kernel(x_ref, sc_ref, sh_ref, g_ref, wq_ref, bq_ref, wh_ref, bh_ref, wgt_ref, bgt_ref,
                   qkvo_ref, hx_ref, gt_ref):
    x = x_ref[0]
    h = _rms(x, g_ref[...]) * (1.0 + sc_ref[0]) + sh_ref[0]
    hb = h.astype(BF16)
    qkvo_ref[0] = _dot(hb, wq_ref[...]) + bq_ref[...]
    hx_ref[0] = _dot(hb, wh_ref[...]) + bh_ref[...]
    gt_ref[0] = _dot3_nt(wgt_ref[...], h) + bgt_ref[...]


def _in_projection(x, sc1, sh1, g_mix, w_in, b_in, ts=512):
    B, S, D = x.shape
    nq = 4 * MLSTM_HEADS * HEAD_DIM
    nh = w_in.shape[1] - nq - N_GATE_COLS
    wq = w_in[:, :nq].astype(BF16)
    wh = w_in[:, nq + N_GATE_COLS:].astype(BF16)
    bq = b_in[:nq].reshape(1, nq)
    bh = b_in[nq + N_GATE_COLS:].reshape(1, nh)
    ng = MLSTM_HEADS * SUBLANES
    pad = lambda t: jnp.pad(t.reshape(4, MLSTM_HEADS, -1).transpose(1, 0, 2), ((0, 0), (0, 4), (0, 0))).reshape(ng, -1)
    wgt = pad(w_in[:, nq:nq + N_GATE_COLS].T)
    bgt = pad(b_in[nq:nq + N_GATE_COLS].reshape(N_GATE_COLS, 1))
    const = lambda b, i: (0, 0)
    return pl.pallas_call(
        _inproj_kernel,
        grid=(B, S // ts),
        in_specs=[pl.BlockSpec((1, ts, D), lambda b, i: (b, i, 0)),
                  pl.BlockSpec((1, 1, D), lambda b, i: (b, 0, 0)),
                  pl.BlockSpec((1, 1, D), lambda b, i: (b, 0, 0)),
                  pl.BlockSpec((1, D), const),
                  pl.BlockSpec((D, nq), const), pl.BlockSpec((1, nq), const),
                  pl.BlockSpec((D, nh), const), pl.BlockSpec((1, nh), const),
                  pl.BlockSpec((ng, D), const), pl.BlockSpec((ng, 1), const)],
        out_specs=[pl.BlockSpec((1, ts, nq), lambda b, i: (b, i, 0)),
                   pl.BlockSpec((1, ts, nh), lambda b, i: (b, i, 0)),
                   pl.BlockSpec((1, ng, ts), lambda b, i: (b, 0, i))],
        out_shape=[jax.ShapeDtypeStruct((B, S, nq), F32),
                   jax.ShapeDtypeStruct((B, S, nh), F32),
                   jax.ShapeDtypeStruct((B, ng, S), F32)],
        compiler_params=_cparams(("parallel", "parallel")),
        name="in_proj",
    )(x, sc1.reshape(B, 1, D), sh1.reshape(B, 1, D), g_mix.reshape(1, D), wq, bq, wh, bh, wgt, bgt)


def _short_conv_chunk(src_ref, lead, c, nc, w_ref, b_ref, rows):
    L = rows.shape[0]
    S = nc * L
    t0 = pl.multiple_of(c * L, L)
    cur = src_ref[lead, pl.ds(t0, L), :]
    p0 = pl.multiple_of(jnp.maximum(t0 - SUBLANES, 0), SUBLANES)
    n0 = pl.multiple_of(jnp.minimum(t0 + L, S - SUBLANES), SUBLANES)
    prev_row = src_ref[lead, pl.ds(p0, SUBLANES), :][SUBLANES - 1:SUBLANES, :]
    next_row = src_ref[lead, pl.ds(n0, SUBLANES), :][0:1, :]
    prev_row = jnp.where(c > 0, prev_row, 0.0)
    next_row = jnp.where(c < nc - 1, next_row, 0.0)
    up = jnp.where(rows == 0, prev_row, pltpu.roll(cur, 1, 0))
    dn = jnp.where(rows == L - 1, next_row, pltpu.roll(cur, L - 1, 0))
    return b_ref[...] + up * w_ref[0:1, :] + cur * w_ref[1:2, :] + dn * w_ref[2:3, :]


def _mlstm_kernel(q_ref, k_ref, v_ref, o_ref, gt_ref, wq_ref, wk_ref, bq_ref, bk_ref, ng_ref,
                  y_ref, qt_ref, kb_ref, vt_ref, c_ref, cl_ref, ld_ref, st_ref, pc_ref, cs_ref, kq_ref, *, nc):
    L = MLSTM_CHUNK
    rows = lax.broadcasted_iota(jnp.int32, (L, LANES), 0)
    cols = lax.broadcasted_iota(jnp.int32, (L, LANES), 1)
    pad = 2 * SUBLANES
    piece_row = lax.broadcasted_iota(jnp.int32, (SUBLANES, LANES), 0)
    k_scale = HEAD_DIM ** -0.5

    def conv_body(c, carry):
        t0 = pl.multiple_of(c * L, L)
        yq = _short_conv_chunk(q_ref, 0, c, nc, wq_ref, bq_ref, rows)
        qt_ref[c] = (yq * _sigmoid(yq)).T.astype(BF16)
        yk = _short_conv_chunk(k_ref, 0, c, nc, wk_ref, bk_ref, rows)
        kb_ref[pl.ds(t0, L), :] = (yk * _sigmoid(yk) * k_scale).astype(BF16)
        vt_ref[c] = v_ref[0, pl.ds(t0, L), :].T
        gates = gt_ref[0, 0, :, pl.ds(t0, L)]
        lf = _log_sigmoid(jnp.where(piece_row < 3, gates[1:2], gates[3:4]))
        hi = lf.astype(BF16).astype(F32)
        mid = (lf - hi).astype(BF16).astype(F32)
        kind = piece_row % 3
        pc_ref[pl.ds(pl.multiple_of(c * SUBLANES, SUBLANES), SUBLANES), :] = jnp.where(
            piece_row >= 6, 0.0, jnp.where(kind == 0, hi, jnp.where(kind == 1, mid, lf - hi - mid)))
        return carry

    lax.fori_loop(0, nc, conv_body, 0, unroll=4)

    pieces = pc_ref[...].astype(BF16)
    fwd_rows = lax.broadcasted_iota(jnp.int32, pc_ref.shape, 0) % SUBLANES < 3
    cs_ref[...] = jnp.where(fwd_rows, _dot(pieces, (rows <= cols).astype(BF16)),
                            _dot(pieces, (rows >= cols).astype(BF16)))

    ROW_B, ROW_MAX, ROW_G, ROW_A, ROW_N, ROW_M = range(6)

    def stat_tile(rows_by_index, base=None):
        tile = jnp.zeros((SUBLANES, LANES), F32) if base is None else base
        for r, value in rows_by_index.items():
            tile = jnp.where(piece_row == r, value, tile)
        return tile

    def local_pass(c, rev):
        t0 = pl.multiple_of(c * L, L)
        slot = 2 * c + (1 if rev else 0)
        kb = kb_ref[pl.ds(t0, L), :]
        gates = gt_ref[0, 0, :, pl.ds(t0, L)]
        i_row = gates[(2 if rev else 0):(3 if rev else 1)]
        cs = cs_ref[pl.ds(pl.multiple_of(c * SUBLANES, SUBLANES), SUBLANES), :]
        o = 3 if rev else 0
        b_row = cs[o:o + 1] + cs[o + 1:o + 2] + cs[o + 2:o + 3]
        keep = (rows >= cols) if rev else (rows <= cols)
        g = b_row[:, 0:1] if rev else b_row[:, L - 1:L]
        w_tile = jnp.broadcast_to(i_row - b_row, (L, LANES)).T
        log_d = jnp.where(keep, w_tile + b_row, NEG_BIG)
        ld_ref[slot] = log_d
        w_end = g - b_row + i_row
        a = jnp.max(w_end, axis=1, keepdims=True)
        e_end = jnp.exp(w_end - a)
        ve = jnp.concatenate([vt_ref[c] * e_end, jnp.broadcast_to(e_end, (pad, L))], axis=0).astype(BF16)
        cn = _dot(ve, kb)
        cl_ref[slot] = cn[:HEAD_DIM]
        st_ref[slot] = stat_tile({ROW_B: b_row, ROW_MAX: jnp.max(log_d, axis=0, keepdims=True), ROW_G: g,
                                  ROW_A: a, ROW_N: cn[HEAD_DIM:HEAD_DIM + 1]})

    def local_body(c, carry):
        kq_ref[c] = _dot(kb_ref[pl.ds(pl.multiple_of(c * L, L), L), :], qt_ref[c])
        local_pass(c, False)
        local_pass(c, True)
        return carry

    lax.fori_loop(0, nc, local_body, 0, unroll=8)

    def scan_step(c, rev, n, m):
        slot = 2 * c + (1 if rev else 0)
        side = 1 if rev else 0
        stats = st_ref[slot]
        g = stats[ROW_G:ROW_G + 1]
        a = stats[ROW_A:ROW_A + 1]
        m_new = jnp.maximum(g + m, a)
        s_prev = jnp.exp(g + m - m_new)
        s_loc = jnp.exp(a - m_new)
        c_in = c_ref[side]
        c_ref[side] = s_prev * c_in + s_loc * cl_ref[slot]
        cl_ref[slot] = c_in
        n_new = s_prev * n + s_loc * stats[ROW_N:ROW_N + 1]
        st_ref[slot] = stat_tile({ROW_N: n, ROW_M: m}, base=stats)
        return n_new, m_new

    c_ref[...] = jnp.zeros_like(c_ref)
    n0 = jnp.zeros((1, HEAD_DIM), F32)
    m0 = jnp.full((1, LANES), MLSTM_M_INIT, F32)

    def scan_body(j, carry):
        n_f, m_f, n_b, m_b = carry
        n_f, m_f = scan_step(j, False, n_f, m_f)
        n_b, m_b = scan_step(nc - 1 - j, True, n_b, m_b)
        return n_f, m_f, n_b, m_b

    lax.fori_loop(0, nc, scan_body, (n0, m0, n0, m0), unroll=2)

    def output_pass(c, rev):
        t0 = pl.multiple_of(c * L, L)
        slot = 2 * c + (1 if rev else 0)
        qt = qt_ref[c]
        stats = st_ref[slot]
        log_inter = stats[ROW_B:ROW_B + 1] + stats[ROW_M:ROW_M + 1]
        m_t = jnp.maximum(log_inter, stats[ROW_MAX:ROW_MAX + 1])
        d = jnp.exp(ld_ref[slot] - m_t)
        e_inter = jnp.exp(log_inter - m_t)
        s = (kq_ref[c] * d).astype(BF16)
        vn = _dot(jnp.concatenate([vt_ref[c].astype(BF16), jnp.ones((pad, L), BF16)], axis=0), s)
        n_in = stats[ROW_N:ROW_N + 1]
        cq = _dot(jnp.concatenate([cl_ref[slot], jnp.broadcast_to(n_in, (pad, LANES))], axis=0).astype(BF16), qt)
        den = vn[HEAD_DIM:HEAD_DIM + 1] + e_inter * cq[HEAD_DIM:HEAD_DIM + 1]
        scale = 1.0 / jnp.maximum(jnp.abs(den), jnp.exp(-m_t))
        return ((vn[:HEAD_DIM] + e_inter * cq[:HEAD_DIM]) * scale).T

    def output_body(c, carry):
        t0 = pl.multiple_of(c * L, L)
        hs = output_pass(c, False) + output_pass(c, True)
        y_ref[0, pl.ds(t0, L), :] = _sigmoid(o_ref[0, pl.ds(t0, L), :]) * _rms(hs, ng_ref[...])
        return carry

    lax.fori_loop(0, nc, output_body, 0, unroll=8)


def _mlstm(qkvo, gates_t, conv_w, conv_b, norm_g):
    B, S, _ = qkvo.shape
    H, Dh = MLSTM_HEADS, HEAD_DIM
    nc = S // MLSTM_CHUNK
    seq = lambda off: pl.BlockSpec((1, S, Dh), lambda b, h: (b, 0, off + h))
    vec = lambda r, off: pl.BlockSpec((r, Dh), lambda b, h: (0, off + h))
    return pl.pallas_call(
        functools.partial(_mlstm_kernel, nc=nc),
        grid=(B, H),
        in_specs=[seq(0), seq(H), seq(2 * H), seq(3 * H),
                  pl.BlockSpec((1, 1, SUBLANES, S), lambda b, h: (b, h, 0, 0)),
                  vec(3, 0), vec(3, H), vec(1, 0), vec(1, H), vec(1, 0)],
        out_specs=pl.BlockSpec((1, S, Dh), lambda b, h: (b, 0, h)),
        out_shape=jax.ShapeDtypeStruct((B, S, H * Dh), F32),
        scratch_shapes=[pltpu.VMEM((nc, Dh, MLSTM_CHUNK), BF16), pltpu.VMEM((S, Dh), BF16),
                        pltpu.VMEM((nc, Dh, MLSTM_CHUNK), F32), pltpu.VMEM((2, Dh, Dh), F32),
                        pltpu.VMEM((2 * nc, Dh, Dh), F32), pltpu.VMEM((2 * nc, MLSTM_CHUNK, MLSTM_CHUNK), F32),
                        pltpu.VMEM((2 * nc, SUBLANES, LANES), F32), pltpu.VMEM((nc * SUBLANES, LANES), F32),
                        pltpu.VMEM((nc * SUBLANES, LANES), F32), pltpu.VMEM((nc, MLSTM_CHUNK, MLSTM_CHUNK), F32)],
        compiler_params=_cparams(("parallel", "arbitrary")),
        name="mlstm",
    )(qkvo, qkvo, qkvo, qkvo, gates_t.reshape(B, H, SUBLANES, S), conv_w, conv_w, conv_b.reshape(1, -1), conv_b.reshape(1, -1),
      norm_g.reshape(1, -1))


FILTER_PASSES = 3
FFT_UNROLL = 16


def _hilo(m):
    m32 = jnp.asarray(m, F32)
    hi = m32.astype(BF16)
    return hi, (m32 - hi.astype(F32)).astype(BF16)


def _stack_complex(m):
    return np.block([[m.real, -m.imag], [m.imag, m.real]])


def _dft_constants(S):
    N = 2 * S
    N2 = FFT_N2
    N1 = N // N2
    k1 = np.arange(N1)
    n2 = np.arange(N2)
    f1 = np.exp(-2j * np.pi * np.outer(k1, np.arange(N1)) / N1)
    f2 = np.exp(-2j * np.pi * np.outer(n2, n2) / N2)
    tw = np.exp(-2j * np.pi * np.outer(k1, n2) / N)
    g1 = np.conj(f1).T[:N1 // 2] / N
    consts = dict(
        f1_real=_hilo(np.concatenate([f1.real, f1.imag], axis=0)),
        f1_half=_hilo(_stack_complex(f1[:, :N1 // 2])),
        f2=_hilo(_stack_complex(f2)),
        g1=_hilo(_stack_complex(g1)),
        f2_tw=_hilo(np.stack([_stack_complex(f2 * tw[k][None, :]) for k in k1]))[0],
        f2_tw_inv=_hilo(np.stack([_stack_complex(np.conj(tw[k])[:, None] * np.conj(f2)) for k in k1]))[0],
        tw_re=jnp.asarray(np.broadcast_to(tw.real[:, :, None], (N1, N2, LANES)), F32),
        tw_im=jnp.asarray(np.broadcast_to(tw.imag[:, :, None], (N1, N2, LANES)), F32),
    )
    return N1, N2, consts


def _hy_hidden_kernel(w1t_ref, w1c_ref, w1s_ref, b1_ref, w2_ref, b2_ref, fr_ref, o_ref, *, S, T):
    j = pl.program_id(0) * T + lax.broadcasted_iota(jnp.int32, (T, 1), 0)
    p = jnp.where(j < S, j, 2 * S - j).astype(F32)
    t = p / (S - 1)
    w = (2.0 * math.pi) * p / S
    band = lax.broadcasted_iota(jnp.int32, (1, HYENA_BANDS), 1).astype(F32)
    bands = 1e-4 + band * ((HYENA_BANDS - 1 - 1e-4) / (HYENA_BANDS - 1))
    arg = bands * w
    pre = t * w1t_ref[...] + _dot3(jnp.cos(arg), w1c_ref[...]) + _dot3(-jnp.sin(arg), w1s_ref[...]) + b1_ref[...]
    hid = jnp.sin(fr_ref[...] * pre)
    o_ref[...] = jnp.sin(fr_ref[...] * (_dot3(hid, w2_ref[...]) + b2_ref[...]))


def _hy_hidden(S, w1, b1, w2, b2, freq):
    N = 2 * S
    T = 1024
    Hd = w2.shape[0]
    full = lambda a: pl.BlockSpec(a.shape, lambda i: (0,) * a.ndim)
    args = (w1[0:1], w1[1:1 + HYENA_BANDS], w1[1 + HYENA_BANDS:], b1.reshape(1, Hd), w2, b2.reshape(1, Hd),
            freq.reshape(1, Hd))
    return pl.pallas_call(
        functools.partial(_hy_hidden_kernel, S=S, T=T),
        grid=(N // T,),
        in_specs=[full(a) for a in args],
        out_specs=pl.BlockSpec((T, Hd), lambda i: (i, 0)),
        out_shape=jax.ShapeDtypeStruct((N, Hd), F32),
        compiler_params=_cparams(("parallel",)),
        name="hyena_hidden",
    )(*args)


def _fft_stage2(a_re, a_im, k1, twr_ref, twi_ref, f2_hi, f2_lo, passes):
    r0 = pl.multiple_of(k1 * FFT_PITCH, SUBLANES)
    ar = a_re[pl.ds(r0, FFT_N2), :]
    ai = a_im[pl.ds(r0, FFT_N2), :]
    twr = twr_ref[k1]
    twi = twi_ref[k1]
    t = jnp.concatenate([ar * twr - ai * twi, ar * twi + ai * twr], axis=0)
    return _dotc(f2_hi, f2_lo, t, passes)


def _hy_filter_kernel(hid_ref, w3f_ref, w3b_ref, df_ref, db_ref, bias_ref, f1h_ref, f1l_ref, f2h_ref, f2l_ref,
                      twr_ref, twi_ref, kr_ref, ki_ref, u_ref, a_re, a_im, *, S, N1):
    N2, P = FFT_N2, FFT_PITCH
    T = 512
    nb = T // N2

    def gen_body(i, carry):
        r0 = pl.multiple_of(i * T, T)
        hid = hid_ref[pl.ds(r0, T), :]
        j = r0 + lax.broadcasted_iota(jnp.int32, (T, 1), 0)
        t = jnp.where(j < S, j, 2 * S - j).astype(F32) / (S - 1)
        kf = _dot3(hid, w3f_ref[...]) * (jnp.exp(-t * jnp.abs(df_ref[...])) + HYENA_WINDOW_SHIFT)
        kb = _dot3(hid, w3b_ref[...]) * (jnp.exp(-t * jnp.abs(db_ref[...])) + HYENA_WINDOW_SHIFT)
        ker = (jnp.where(j < S, kf, 0.0) + jnp.where((j > S) | (j == 0), kb, 0.0)
               + jnp.where(j == 0, bias_ref[0], 0.0))
        for b in range(nb):
            u_ref[pl.ds(pl.multiple_of((i * nb + b) * P, SUBLANES), N2), :] = ker[b * N2:(b + 1) * N2]
        return carry

    lax.fori_loop(0, 2 * S // T, gen_body, 0)

    def stage1(n2, carry):
        x = u_ref[pl.ds(n2, N1, stride=P), :]
        a = _dotc(f1h_ref[...], f1l_ref[...], x, FILTER_PASSES)
        a_re[pl.ds(n2, N1, stride=P), :] = a[:N1]
        a_im[pl.ds(n2, N1, stride=P), :] = a[N1:]
        return carry

    lax.fori_loop(0, N2, stage1, 0, unroll=FFT_UNROLL)

    def stage2(k1, carry):
        x = _fft_stage2(a_re, a_im, k1, twr_ref, twi_ref, f2h_ref[...], f2l_ref[...], FILTER_PASSES)
        r0 = pl.multiple_of(k1 * N2, N2)
        kr_ref[0, pl.ds(r0, N2), :] = x[:N2]
        ki_ref[0, pl.ds(r0, N2), :] = x[N2:]
        return carry

    lax.fori_loop(0, N1, stage2, 0, unroll=2 * FFT_UNROLL)


def _hy_filter_spectrum(S, hid, w3, deltas, bias, N1, consts):
    N = 2 * S
    N2, P = FFT_N2, FFT_PITCH
    Hd = hid.shape[1]
    nblk = w3.shape[1] // (2 * HYENA_ORDER * LANES)
    f1h, f1l = consts["f1_real"]
    f2h, f2l = consts["f2"]
    full = lambda a: pl.BlockSpec(a.shape, lambda o, j: (0,) * a.ndim, pipeline_mode=pl.Buffered(1))
    colf = lambda o, j: (0, (2 * o) * nblk + j)
    colb = lambda o, j: (0, (2 * o + 1) * nblk + j)
    out = jax.ShapeDtypeStruct((HYENA_ORDER, N, nblk * LANES), F32)
    return pl.pallas_call(
        functools.partial(_hy_filter_kernel, S=S, N1=N1),
        grid=(HYENA_ORDER, nblk),
        in_specs=[full(hid),
                  pl.BlockSpec((Hd, LANES), colf), pl.BlockSpec((Hd, LANES), colb),
                  pl.BlockSpec((1, LANES), colf), pl.BlockSpec((1, LANES), colb),
                  pl.BlockSpec((1, 1, LANES), lambda o, j: (o, 0, j)),
                  full(f1h), full(f1l), full(f2h), full(f2l), full(consts["tw_re"]), full(consts["tw_im"])],
        out_specs=[pl.BlockSpec((1, N, LANES), lambda o, j: (o, 0, j))] * 2,
        out_shape=[out, out],
        scratch_shapes=[pltpu.VMEM((N1 * P, LANES), F32)] * 3,
        compiler_params=_cparams(("parallel", "parallel")),
        name="hyena_filter_spectrum",
    )(hid, w3, w3, deltas.reshape(1, -1), deltas.reshape(1, -1), bias.reshape(HYENA_ORDER, 1, -1),
      f1h, f1l, f2h, f2l, consts["tw_re"], consts["tw_im"])


def _hy_conv_kernel(u_ref, x_ref, kr_ref, ki_ref, wu_ref, bu_ref, wx_ref, bx_ref, ng_ref,
                    f1_ref, g1_ref, mf_ref, mi_ref, o_ref, u_re, u_im, a_re, a_im, *, N1, conv_u, last):
    N2, P = FFT_N2, FFT_PITCH
    nh = N1 // 2
    rows = lax.broadcasted_iota(jnp.int32, (N2, LANES), 0)

    def load_body(c, carry):
        r0 = pl.multiple_of(c * P, SUBLANES)
        for b, dst in ((0, u_re), (1, u_im)):
            if conv_u:
                dst[pl.ds(r0, N2), :] = _short_conv_chunk(u_ref, b, c, nh, wu_ref, bu_ref, rows)
            else:
                dst[pl.ds(r0, N2), :] = u_ref[b, pl.ds(pl.multiple_of(c * N2, N2), N2), :]
        return carry

    lax.fori_loop(0, nh, load_body, 0, unroll=4)

    def stage1(n2, carry):
        cols = [jnp.concatenate([u_re[pl.ds(2 * n2 + i, nh, stride=P), :], u_im[pl.ds(2 * n2 + i, nh, stride=P), :]],
                                axis=0).astype(BF16) for i in range(2)]
        a = _dot(f1_ref[...], jnp.concatenate(cols, axis=1))
        for i in range(2):
            a_re[pl.ds(2 * n2 + i, N1, stride=P), :] = a[:N1, i * LANES:(i + 1) * LANES]
            a_im[pl.ds(2 * n2 + i, N1, stride=P), :] = a[N1:, i * LANES:(i + 1) * LANES]
        return carry

    lax.fori_loop(0, N2 // 2, stage1, 0, unroll=FFT_UNROLL // 2)

    def stage2(k1, carry):
        r0 = pl.multiple_of(k1 * P, SUBLANES)
        a = jnp.concatenate([a_re[pl.ds(r0, N2), :], a_im[pl.ds(r0, N2), :]], axis=0)
        x = _dot(mf_ref[k1], a.astype(BF16))
        s0 = pl.multiple_of(k1 * N2, N2)
        kr = kr_ref[0, pl.ds(s0, N2), :]
        ki = ki_ref[0, pl.ds(s0, N2), :]
        xr, xi = x[:N2], x[N2:]
        y = jnp.concatenate([xr * kr - xi * ki, xr * ki + xi * kr], axis=0)
        b = _dot(mi_ref[k1], y.astype(BF16))
        a_re[pl.ds(r0, N2), :] = b[:N2]
        a_im[pl.ds(r0, N2), :] = b[N2:]
        return carry

    lax.fori_loop(0, N1, stage2, 0, unroll=2 * FFT_UNROLL)

    def inv2(n2, carry):
        cols = [jnp.concatenate([a_re[pl.ds(2 * n2 + i, N1, stride=P), :], a_im[pl.ds(2 * n2 + i, N1, stride=P), :]],
                                axis=0).astype(BF16) for i in range(2)]
        y = _dot(g1_ref[...], jnp.concatenate(cols, axis=1))
        for i in range(2):
            u_re[pl.ds(2 * n2 + i, nh, stride=P), :] = y[:nh, i * LANES:(i + 1) * LANES]
            u_im[pl.ds(2 * n2 + i, nh, stride=P), :] = y[nh:, i * LANES:(i + 1) * LANES]
        return carry

    lax.fori_loop(0, N2 // 2, inv2, 0, unroll=FFT_UNROLL // 2)

    def out_body(c, carry):
        r0 = pl.multiple_of(c * P, SUBLANES)
        t0 = pl.multiple_of(c * N2, N2)
        for b, src in ((0, u_re), (1, u_im)):
            z = _short_conv_chunk(x_ref, b, c, nh, wx_ref, bx_ref, rows) * src[pl.ds(r0, N2), :]
            o_ref[b, pl.ds(t0, N2), :] = _rms(z, ng_ref[...]) if last else z
        return carry

    lax.fori_loop(0, nh, out_body, 0, unroll=4)


def _hy_conv(u, u_blk, x, x_blk, kr, ki, order, conv_w, conv_b, norm_g, N1, consts, conv_u, last):
    B, S, _ = u.shape
    N2, P = FFT_N2, FFT_PITCH
    N = 2 * S
    nblk = kr.shape[2] // LANES
    pair = 2
    single = pl.Buffered(1)
    full = lambda a: pl.BlockSpec(a.shape, lambda j, p: (0,) * a.ndim, pipeline_mode=single)
    seq = lambda off: pl.BlockSpec((pair, S, LANES), lambda j, p: (p, 0, off + j))
    vec = lambda r, off: pl.BlockSpec((r, LANES), lambda j, p: (0, off + j))
    spec = pl.BlockSpec((1, N, LANES), lambda j, p: (order, 0, j), pipeline_mode=single)
    mats = [consts["f1_half"][0], consts["g1"][0], consts["f2_tw"], consts["f2_tw_inv"]]
    return pl.pallas_call(
        functools.partial(_hy_conv_kernel, N1=N1, conv_u=conv_u, last=last),
        grid=(nblk, B // pair),
        in_specs=[seq(u_blk), seq(x_blk), spec, spec,
                  vec(3, u_blk if conv_u else 0), vec(1, u_blk if conv_u else 0), vec(3, x_blk), vec(1, x_blk),
                  pl.BlockSpec((1, LANES), lambda j, p: (0, j))] + [full(m) for m in mats],
        out_specs=pl.BlockSpec((pair, S, LANES), lambda j, p: (p, 0, j)),
        out_shape=jax.ShapeDtypeStruct((B, S, nblk * LANES), F32),
        scratch_shapes=[pltpu.VMEM((N1 // 2 * P, LANES), F32)] * 2 + [pltpu.VMEM((N1 * P, LANES), F32)] * 2,
        compiler_params=_cparams(("parallel", "parallel")),
        name=f"hyena_conv{order}",
    )(u, x, kr, ki, conv_w, conv_b.reshape(1, -1), conv_w, conv_b.reshape(1, -1), norm_g.reshape(1, -1), *mats)


def _hyena(hx, conv_w, conv_b, w1, b1, w2, b2, w3, freq, deltas, bias, norm_g):
    B, S, C3 = hx.shape
    nblk = C3 // (HYENA_ORDER + 1) // LANES
    N1, N2, consts = _dft_constants(S)
    hid = _hy_hidden(S, w1, b1, w2, b2, freq)
    kr, ki = _hy_filter_spectrum(S, hid, w3, deltas, bias, N1, consts)
    z1 = _hy_conv(hx, 2 * nblk, hx, 0, kr, ki, 0, conv_w, conv_b, norm_g, N1, consts, True, False)
    return _hy_conv(z1, 0, hx, nblk, kr, ki, 1, conv_w, conv_b, norm_g, N1, consts, False, True)


def _outproj_kernel(ym_ref, yh_ref, x_ref, gt_ref, sc_ref, sh_ref, g_ref, wm_ref, wh_ref, wr_ref,
                    x1_ref, hf_ref, aff_ref):
    mixed = _dot(ym_ref[0].astype(BF16), wm_ref[...]) + _dot(yh_ref[0].astype(BF16), wh_ref[...])
    x1 = x_ref[0] + gt_ref[0] * mixed
    x1_ref[0] = x1
    hf = _rms(x1, g_ref[...]) * (1.0 + sc_ref[0]) + sh_ref[0]
    hf_ref[0] = hf.astype(BF16)
    logits = _dot3_nt(wr_ref[...], hf)
    e = jnp.exp(logits - jnp.max(logits, axis=0, keepdims=True))
    aff_ref[0] = e / jnp.sum(e, axis=0, keepdims=True)


def _out_projection(y_m, y_h, x, gt1, sc2, sh2, g_ffn, w_out, w_router, ts=512):
    B, S, D = x.shape
    Wm = y_m.shape[2]
    E = w_router.shape[1]
    wm = w_out[:Wm].astype(BF16)
    wh = w_out[Wm:].astype(BF16)
    tile = lambda w: pl.BlockSpec((1, ts, w), lambda b, i: (b, i, 0))
    row = pl.BlockSpec((1, 1, D), lambda b, i: (b, 0, 0))
    full = lambda a: pl.BlockSpec(a.shape, lambda b, i: (0,) * a.ndim)
    wr = w_router.T
    g = g_ffn.reshape(1, D)
    return pl.pallas_call(
        _outproj_kernel,
        grid=(B, S // ts),
        in_specs=[tile(Wm), tile(y_h.shape[2]), tile(D), row, row, row, full(g), full(wm), full(wh), full(wr)],
        out_specs=[tile(D), tile(D), pl.BlockSpec((1, E, ts), lambda b, i: (b, 0, i))],
        out_shape=[jax.ShapeDtypeStruct((B, S, D), F32), jax.ShapeDtypeStruct((B, S, D), BF16),
                   jax.ShapeDtypeStruct((B, E, S), F32)],
        compiler_params=_cparams(("parallel", "parallel")),
        name="out_proj_router",
    )(y_m, y_h, x, gt1.reshape(B, 1, D), sc2.reshape(B, 1, D), sh2.reshape(B, 1, D), g, wm, wh, wr)


def _route_kernel(aff_ref, pos_ref, cnt_ref, *, S, cap):
    aff = aff_ref[0]
    E = aff.shape[0]
    tpos = lax.broadcasted_iota(jnp.int32, (E, S), 1)
    count = lambda mask: jnp.sum(jnp.where(mask, 1.0, 0.0), axis=1, keepdims=True)
    as_float = lambda word: lax.bitcast_convert_type(word, F32)

    def value_step(i, prefix):
        cand = prefix | jnp.left_shift(1, 30 - i)
        return jnp.where(count(aff >= as_float(cand)) >= cap, cand, prefix)

    thresh = as_float(lax.fori_loop(0, 31, value_step, jnp.zeros((E, 1), jnp.int32)))
    gt = aff > thresh
    eq = aff == thresh
    need = cap - count(gt)
    nbits = S.bit_length() - 1

    def index_step(i, x):
        cand = x | jnp.left_shift(1, nbits - 1 - i)
        return jnp.where(count(eq & (tpos < cand)) < need, cand, x)

    last_tie = lax.fori_loop(0, nbits, index_step, jnp.zeros((E, 1), jnp.int32))
    sel = jnp.where(gt | (eq & (tpos <= last_tie)), 1.0, 0.0)
    before = (lax.broadcasted_iota(jnp.int32, (LANES, LANES), 0)
              < lax.broadcasted_iota(jnp.int32, (LANES, LANES), 1)).astype(BF16)
    carry = jnp.zeros((E, 1), F32)
    lane = lax.broadcasted_iota(jnp.int32, (E, LANES), 1)
    starts = jnp.zeros((E, LANES), F32)
    for c in range(S // LANES):
        sc = sel[:, c * LANES:(c + 1) * LANES]
        rank = _dot(sc.astype(BF16), before) + carry
        pos_ref[0, :, c * LANES:(c + 1) * LANES] = jnp.where(sc > 0.0, rank, -1.0).astype(jnp.int32)
        starts = jnp.where(lane == c, carry, starts)
        carry = carry + jnp.sum(sc, axis=1, keepdims=True)
    cnt_ref[0] = jnp.where(lane >= S // LANES, carry, starts).astype(jnp.int32)


def _route(aff_t, cap):
    B, E, S = aff_t.shape
    assert S // LANES < LANES
    R = B * E
    blk = pl.BlockSpec((1, R, S), lambda i: (0, 0, 0))
    pos, cnt = pl.pallas_call(
        functools.partial(_route_kernel, S=S, cap=cap),
        grid=(1,),
        in_specs=[blk],
        out_specs=[blk, pl.BlockSpec((1, R, LANES), lambda i: (0, 0, 0))],
        out_shape=[jax.ShapeDtypeStruct((1, R, S), jnp.int32), jax.ShapeDtypeStruct((1, R, LANES), jnp.int32)],
        compiler_params=_cparams(("arbitrary",)),
        name="route_topk",
    )(aff_t.reshape(1, R, S))
    return pos.reshape(B, E, S), cnt.reshape(B, E, LANES)


def _window_tables(cnt, S, cap):
    B, E, _ = cnt.shape
    nch = S // LANES
    bounds = cnt[:, :, 1:nch + 1]
    first = jnp.arange(cap // GATHER_SLOTS, dtype=jnp.int32) * GATHER_SLOTS
    chunk_of = lambda slot: jnp.sum((bounds[:, :, None, :] <= slot[None, None, :, None]).astype(jnp.int32), axis=-1)
    per = GATHER_ALIGN // LANES
    step = GATHER_TOKENS // GATHER_ALIGN
    lo_a = chunk_of(first) // per
    hi_a = chunk_of(first + GATHER_SLOTS - 1) // per
    g_n = (hi_a - lo_a) // step + 1
    g_lo = jnp.minimum(lo_a, S // GATHER_ALIGN - step * g_n)
    edges = cnt[:, :, 0:nch + 1:SCATTER_TOKENS // LANES]
    lo, hi = edges[:, :, :-1], edges[:, :, 1:]
    s_lo = jnp.swapaxes(lo // SCATTER_SLOTS, 1, 2)
    s_n = jnp.swapaxes(jnp.where(hi > lo, (hi - 1) // SCATTER_SLOTS - lo // SCATTER_SLOTS + 1, 0), 1, 2)

    def flat_runs(starts, counts, stride, length):
        groups = counts.shape[-1]
        ends = jnp.cumsum(counts, axis=-1)
        j = jnp.arange(length, dtype=jnp.int32)
        grp = jnp.minimum(jnp.sum((ends[..., None, :] <= j[:, None]).astype(jnp.int32), axis=-1), groups - 1)
        chosen = grp[..., None] == jnp.arange(groups, dtype=jnp.int32)
        pick = lambda t: jnp.sum(jnp.where(chosen, t[..., None, :], 0), axis=-1)
        valid = j < ends[..., -1:]
        val = jnp.where(valid, pick(starts) + (j - pick(ends - counts)) * stride, 0)
        flat = lambda t: t.reshape(-1).astype(jnp.int32)
        return flat(jnp.where(valid, grp, 0)), flat(val), flat(ends[..., -1])

    return flat_runs(g_lo, g_n, step, GATHER_LIST) + flat_runs(s_lo, s_n, 1, SCATTER_LIST)


META_ROWS = 16


GATHER_SLOTS = 128
GATHER_TOKENS = 512
GATHER_ALIGN = 256
GATHER_LIST = 12
SCATTER_SLOTS = 256
SCATTER_TOKENS = 512
SCATTER_LIST = N_EXPERTS * ((SCATTER_TOKENS - 1) // SCATTER_SLOTS + 2)


def _gather_kernel(blk_ref, win_ref, cnt_ref, pos_ref, aff_ref, hf_ref, xs_ref, meta_ref, acc_ref, macc_ref):
    R, tk = GATHER_SLOTS, GATHER_TOKENS
    base = (pl.program_id(0) * pl.num_programs(1) + pl.program_id(1)) * GATHER_LIST
    count = cnt_ref[pl.program_id(0) * pl.num_programs(1) + pl.program_id(1)]
    mrow = lax.broadcasted_iota(jnp.int32, (META_ROWS, tk), 0)
    lane = lax.broadcasted_iota(jnp.int32, (1, tk), 1)
    srow = lax.broadcasted_iota(jnp.int32, (R, tk), 0)
    acc_ref[...] = jnp.zeros_like(acc_ref)
    macc_ref[...] = jnp.zeros_like(macc_ref)

    def contribution(j):
        r0 = pl.multiple_of(blk_ref[base + j] * R, R)
        t0 = pl.multiple_of(win_ref[base + j] * GATHER_ALIGN, GATHER_ALIGN)
        first_slot = jnp.where(j < count, r0, -(R + 1))
        onehot = jnp.where(first_slot + srow == pos_ref[0, 0, :, pl.ds(t0, tk)], 1.0, 0.0).astype(BF16)
        a = aff_ref[0, 0, :, pl.ds(t0, tk)]
        hi = a.astype(BF16).astype(F32)
        mid = (a - hi).astype(BF16).astype(F32)
        lo = (a - hi - mid).astype(BF16).astype(F32)
        t = t0 + lane
        pieces = (hi, mid, lo, (t // 64).astype(F32), (t % 64).astype(F32))
        meta = jnp.zeros((META_ROWS, tk), F32)
        for r, piece in enumerate(pieces):
            meta = jnp.where(mrow == r, piece, meta)
        return r0, _dot(onehot, hf_ref[0, pl.ds(t0, tk), :]), _dot_nt(meta.astype(BF16), onehot)

    def pair_body(i, carry):
        for r0, rows, meta in (contribution(2 * i), contribution(2 * i + 1)):
            acc_ref[pl.ds(r0, R), :] += rows
            macc_ref[:, pl.ds(r0, R)] += meta
        return carry

    lax.fori_loop(0, (count + 1) // 2, pair_body, 0)
    xs_ref[0, 0] = acc_ref[...].astype(BF16)
    meta_ref[0, 0] = macc_ref[...]


def _gather(g_blk, g_win, g_cnt, pos, aff_t, hf, cap):
    B, E, S = pos.shape
    D = hf.shape[2]
    most = cap // GATHER_SLOTS + (S // GATHER_ALIGN - 1) // (GATHER_TOKENS // GATHER_ALIGN)
    assert most <= GATHER_LIST and GATHER_LIST % 2 == 0
    row = pl.BlockSpec((1, 1, 1, S), lambda b, e, *_: (b, e, 0, 0))
    return pl.pallas_call(
        _gather_kernel,
        grid_spec=pltpu.PrefetchScalarGridSpec(
            num_scalar_prefetch=3,
            grid=(B, E),
            in_specs=[row, row, pl.BlockSpec((1, S, D), lambda b, e, *_: (b, 0, 0))],
            out_specs=[pl.BlockSpec((1, 1, cap, D), lambda b, e, *_: (b, e, 0, 0)),
                       pl.BlockSpec((1, 1, META_ROWS, cap), lambda b, e, *_: (b, e, 0, 0))],
            scratch_shapes=[pltpu.VMEM((cap, D), F32), pltpu.VMEM((META_ROWS, cap), F32)]),
        out_shape=[jax.ShapeDtypeStruct((B, E, cap, D), BF16), jax.ShapeDtypeStruct((B, E, META_ROWS, cap), F32)],
        compiler_params=_cparams(("parallel", "arbitrary")),
        name="moe_gather",
    )(g_blk, g_win, g_cnt, pos.reshape(B, E, 1, S), aff_t.reshape(B, E, 1, S), hf)


def _ffn_kernel(xs_ref, meta_ref, wg_ref, wu_ref, wd_ref, y_ref, wdb_ref, *, tf):
    x = xs_ref[0, 0]
    cap = x.shape[0]
    FF = wg_ref.shape[2]

    @pl.when(pl.program_id(1) == 0)
    def _():
        for f in range(FF // tf):
            wdb_ref[f * tf:(f + 1) * tf, :] = wd_ref[0, f * tf:(f + 1) * tf, :].astype(BF16)

    y = jnp.zeros((cap, wd_ref.shape[2]), F32)
    for f in range(FF // tf):
        hg = _dot(x, wg_ref[0, :, f * tf:(f + 1) * tf])
        hu = _dot(x, wu_ref[0, :, f * tf:(f + 1) * tf])
        hid = (hg * _sigmoid(hg) * hu).astype(BF16)
        y = y + _dot(hid, wdb_ref[f * tf:(f + 1) * tf, :])
    m = meta_ref[0, 0]
    g_row = m[0:1] + m[1:2] + m[2:3]
    eye = lax.broadcasted_iota(jnp.int32, (cap, cap), 0) == lax.broadcasted_iota(jnp.int32, (cap, cap), 1)
    g_col = jnp.sum(jnp.where(eye, g_row, 0.0), axis=1, keepdims=True)
    y_ref[0, 0] = (y * g_col).astype(BF16)


def _expert_ffn(xs, meta, w_gate, w_up, w_down, tf=512):
    B, E, cap, D = xs.shape
    FF = w_gate.shape[2]
    return pl.pallas_call(
        functools.partial(_ffn_kernel, tf=tf),
        grid=(E, B),
        in_specs=[pl.BlockSpec((1, 1, cap, D), lambda e, b: (b, e, 0, 0)),
                  pl.BlockSpec((1, 1, META_ROWS, cap), lambda e, b: (b, e, 0, 0)),
                  pl.BlockSpec((1, D, FF), lambda e, b: (e, 0, 0)),
                  pl.BlockSpec((1, D, FF), lambda e, b: (e, 0, 0)),
                  pl.BlockSpec((1, FF, D), lambda e, b: (e, 0, 0))],
        out_specs=pl.BlockSpec((1, 1, cap, D), lambda e, b: (b, e, 0, 0)),
        out_shape=jax.ShapeDtypeStruct((B, E, cap, D), BF16),
        scratch_shapes=[pltpu.VMEM((FF, D), BF16)],
        compiler_params=_cparams(("parallel", "arbitrary")),
        name="moe_ffn",
    )(xs, meta, w_gate.astype(BF16), w_up.astype(BF16), w_down)


def _scatter_kernel(e_ref, blk_ref, cnt_ref, y_ref, meta_ref, x1_ref, gt_ref, g_ref, o_ref, acc_ref):
    tt, R = SCATTER_TOKENS, SCATTER_SLOTS
    tile = pl.program_id(0) * pl.num_programs(1) + pl.program_id(1)
    tok = pl.program_id(1) * tt + lax.broadcasted_iota(jnp.int32, (tt, R), 0)
    acc_ref[...] = jnp.zeros_like(acc_ref)
    count = cnt_ref[tile]

    def contribution(j):
        e = e_ref[tile * SCATTER_LIST + j]
        r0 = pl.multiple_of(blk_ref[tile * SCATTER_LIST + j] * R, R)
        m = meta_ref[0, e, :, pl.ds(r0, R)]
        idx = jnp.where(j < count, (m[3:4] * 64.0 + m[4:5]).astype(jnp.int32), -1)
        onehot = jnp.where(tok == idx, 1.0, 0.0).astype(BF16)
        return _dot(onehot, y_ref[0, e, pl.ds(r0, R), :])

    def pair_body(i, carry):
        acc_ref[...] += contribution(2 * i) + contribution(2 * i + 1)
        return carry

    lax.fori_loop(0, (count + 1) // 2, pair_body, 0)
    o_ref[0] = _rms(x1_ref[0] + gt_ref[0] * acc_ref[...], g_ref[...])


def _scatter_final(s_e, s_blk, s_cnt, y, meta, x1, gt2, g_final):
    B, E, cap, D = y.shape
    S = x1.shape[1]
    tt = SCATTER_TOKENS
    tile = pl.BlockSpec((1, tt, D), lambda b, i, *_: (b, i, 0))
    return pl.pallas_call(
        _scatter_kernel,
        grid_spec=pltpu.PrefetchScalarGridSpec(
            num_scalar_prefetch=3,
            grid=(B, S // tt),
            in_specs=[pl.BlockSpec((1, E, cap, D), lambda b, i, *_: (b, 0, 0, 0)),
                      pl.BlockSpec((1, E, META_ROWS, cap), lambda b, i, *_: (b, 0, 0, 0)),
                      tile, pl.BlockSpec((1, 1, D), lambda b, i, *_: (b, 0, 0)),
                      pl.BlockSpec((1, D), lambda b, i, *_: (0, 0))],
            out_specs=tile,
            scratch_shapes=[pltpu.VMEM((tt, D), F32)]),
        out_shape=jax.ShapeDtypeStruct((B, S, D), F32),
        compiler_params=_cparams(("parallel", "parallel")),
        name="moe_scatter_final",
    )(s_e, s_blk, s_cnt, y, meta, x1, gt2.reshape(B, 1, D), g_final.reshape(1, D))


def kernel(x, c, w_ada, b_ada, g_mix, w_in, b_in, conv_qk_w, conv_qk_b, mlstm_norm_g, conv_hy_w, conv_hy_b,
           hy_w1, hy_b1, hy_w2, hy_b2, hy_w3, hy_freq, hy_deltas, hy_bias, hyena_norm_g, w_out, g_ffn,
           w_router, w_gate, w_up, w_down, g_final):
    S = x.shape[1]
    cap = EC_CAPACITY_FACTOR * S // N_EXPERTS
    assert w_ada.shape[0] == 1, "single-layer block: the final RMSNorm is fused into the MoE scatter"
    l = 0
    mod = _modulation(c, w_ada[l], b_ada[l])
    sh1, sc1, gt1, sh2, sc2, gt2 = jnp.split(mod, 6, axis=-1)
    qkvo, hx, gates_t = _in_projection(x, sc1, sh1, g_mix[l], w_in[l], b_in[l])
    y_m = _mlstm(qkvo, gates_t, conv_qk_w[l], conv_qk_b[l], mlstm_norm_g[l])
    y_h = _hyena(hx, conv_hy_w[l], conv_hy_b[l], hy_w1[l], hy_b1[l], hy_w2[l], hy_b2[l], hy_w3[l],
                 hy_freq[l], hy_deltas[l], hy_bias[l], hyena_norm_g[l])
    x1, hf, aff_t = _out_projection(y_m, y_h, x, gt1, sc2, sh2, g_ffn[l], w_out[l], w_router[l])
    pos, cnt = _route(aff_t, cap)
    g_blk, g_win, g_cnt, s_e, s_blk, s_cnt = _window_tables(cnt, S, cap)
    xs, meta = _gather(g_blk, g_win, g_cnt, pos, aff_t, hf, cap)
    y = _expert_ffn(xs, meta, w_gate[l], w_up[l], w_down[l])
    return _scatter_final(s_e, s_blk, s_cnt, y, meta, x1, gt2, g_final)
```
